```python
import math
import jax
import jax.numpy as jnp
from jax import lax
import numpy as np

D_MODEL = 1024
BATCH = 4
SEQ = 8192
DEPTH = 4

MIX = D_MODEL
A_WIDTH = MIX // 2
M_WIDTH = MIX - A_WIDTH
A_HEADS = 4
A_DH = A_WIDTH // (2 * A_HEADS)
A_DV = 2 * A_DH
M_HEADS = 4
M_DH = M_WIDTH // M_HEADS
CONV_K = 4
CHUNK = 64
Q_BLOCK = 128
N_BUCKETS = 32
MAX_DIST = 128
D_FF = (7 * D_MODEL) // 2
N_EXPERTS = 8
TOP_K = 2
EPS = 1e-6
IN_SIZES = (A_WIDTH, A_WIDTH, A_WIDTH, M_WIDTH, M_WIDTH, M_WIDTH, M_WIDTH, M_HEADS, M_HEADS)
IN_W = sum(IN_SIZES)
IN_SPLITS = tuple(int(s) for s in np.cumsum(IN_SIZES)[:-1])

kernel_name = "hymba_diffattn_mlstm_moe_trunk"

F32 = jnp.float32


def rmsnorm(x, g):
    xf = x.astype(F32)
    y = xf * lax.rsqrt(jnp.mean(xf * xf, axis=-1, keepdims=True) + EPS)
    return (y * g.astype(F32)).astype(x.dtype)


def t5_bucket(n):
    max_exact = N_BUCKETS // 2
    nf = jnp.maximum(n, 1).astype(F32)
    large = max_exact + (jnp.log(nf / max_exact) / math.log(MAX_DIST / max_exact)
                         * (N_BUCKETS - max_exact)).astype(jnp.int32)
    large = jnp.minimum(large, N_BUCKETS - 1)
    return jnp.where(n < max_exact, n, large)


def causal_dwconv(x, w, b):
    S = x.shape[1]
    xp = jnp.pad(x, ((0, 0), (CONV_K - 1, 0), (0, 0)))
    y = b
    for j in range(CONV_K):
        y = y + xp[:, j:j + S] * w[j]
    return y


def diff_attention(q, k, v, lam, lam_init, dist_bias, norm_g):
    B, S = q.shape[0], q.shape[1]
    nb = S // Q_BLOCK
    q = q * (A_DH ** -0.5)
    kpos = jnp.arange(S)

    def block(i):
        q0 = i * Q_BLOCK
        qb = lax.dynamic_slice_in_dim(q, q0, Q_BLOCK, axis=1)
        dist = (q0 + jnp.arange(Q_BLOCK))[:, None] - kpos[None, :]
        bias = jnp.take(dist_bias, jnp.clip(dist, 0, S - 1), axis=1)
        s = jnp.einsum('bqhcd,bkhcd->bhcqk', qb, k).astype(F32) + bias[None, :, None]
        s = jnp.where(dist >= 0, s, -jnp.inf)
        p = jax.nn.softmax(s, axis=-1)
        a = p[:, :, 0] - lam * p[:, :, 1]
        return jnp.einsum('bhqk,bkhe->bqhe', a.astype(v.dtype), v)

    o = lax.map(block, jnp.arange(nb))
    o = jnp.moveaxis(o, 0, 1).reshape(B, S, A_HEADS, A_DV)
    return rmsnorm(o, norm_g) * (1.0 - lam_init)


def mlstm_chunkwise(q, k, v, i_pre, f_pre):
    B, S, H, d = q.shape
    nc = S // CHUNK

    def chunks(t):
        t = t.astype(F32).reshape((B, nc, CHUNK, H) + t.shape[3:])
        return jnp.moveaxis(t, 3, 1)

    qc, kc, vc = chunks(q), chunks(k * (d ** -0.5)), chunks(v)
    ig = chunks(i_pre)
    lf = jax.nn.log_sigmoid(chunks(f_pre))
    b = jnp.cumsum(lf, axis=-1)
    b_last = b[..., -1]
    a = b_last[..., None] - b + ig

    def step(carry, xs):
        C, n, m = carry
        a_c, bl_c, k_c, v_c = xs
        m_new = jnp.maximum(bl_c + m, jnp.max(a_c, axis=-1))
        decay = jnp.exp(bl_c + m - m_new)
        w = jnp.exp(a_c - m_new[..., None])
        C_new = decay[..., None, None] * C + jnp.einsum('bhl,bhlk,bhlv->bhkv', w, k_c, v_c)
        n_new = decay[..., None] * n + jnp.einsum('bhl,bhlk->bhk', w, k_c)
        return (C_new, n_new, m_new), (C, n, m)

    init = (jnp.zeros((B, H, d, d), F32), jnp.zeros((B, H, d), F32), jnp.zeros((B, H), F32))
    xs = (jnp.moveaxis(a, 2, 0), jnp.moveaxis(b_last, 2, 0),
          jnp.moveaxis(kc, 2, 0), jnp.moveaxis(vc, 2, 0))
    _, (C_prev, n_prev, m_prev) = lax.scan(step, init, xs)
    C_prev = jnp.moveaxis(C_prev, 0, 2)
    n_prev = jnp.moveaxis(n_prev, 0, 2)
    m_prev = jnp.moveaxis(m_prev, 0, 2)

    tri = jnp.tril(jnp.ones((CHUNK, CHUNK), dtype=bool))
    Dm = jnp.where(tri, b[..., :, None] - b[..., None, :] + ig[..., None, :], -jnp.inf)
    inter = b + m_prev[..., None]
    m_t = jnp.maximum(inter, jnp.max(Dm, axis=-1))
    Sw = jnp.einsum('bhnld,bhnsd->bhnls', qc, kc) * jnp.exp(Dm - m_t[..., None])
    dec = jnp.exp(inter - m_t)
    num = (dec[..., None] * jnp.einsum('bhnld,bhnde->bhnle', qc, C_prev)
           + jnp.einsum('bhnls,bhnse->bhnle', Sw, vc))
    den = dec * jnp.einsum('bhnld,bhnd->bhnl', qc, n_prev) + jnp.sum(Sw, axis=-1)
    h = num / jnp.maximum(jnp.abs(den), jnp.exp(-m_t))[..., None]
    return jnp.moveaxis(h, 1, 3).reshape(B, S, H, d)


def swiglu(t, w1, w3, w2):
    return (jax.nn.silu(t @ w1) * (t @ w3)) @ w2


def moe_swiglu(h, router_w, w1, w3, w2):
    B, S, D = h.shape
    t = h.reshape(B * S, D)
    logits = (t @ router_w).astype(F32)
    top_v, top_i = lax.top_k(logits, TOP_K)
    wts = jax.nn.softmax(top_v, axis=-1)
    gates = jnp.sum(jax.nn.one_hot(top_i, N_EXPERTS, dtype=F32) * wts[..., None], axis=1)
    out = jnp.zeros_like(t)
    for e in range(N_EXPERTS):
        out = out + gates[:, e:e + 1].astype(t.dtype) * swiglu(t, w1[e], w3[e], w2[e])
    return out.reshape(B, S, D)


def setup_inputs(seed: int = 0) -> dict:
    key = jax.random.key(seed)
    ks = jax.random.split(key, 24)
    n_dense = (DEPTH + 1) // 2
    n_moe = DEPTH // 2
    D = D_MODEL
    nrm = jax.random.normal
    gate_b = jnp.stack([
        0.1 * nrm(ks[8], (DEPTH, M_HEADS), F32),
        jnp.linspace(3.0, 6.0, M_HEADS, dtype=F32)[None, :] + 0.1 * nrm(ks[9], (DEPTH, M_HEADS), F32),
    ], axis=1)
    return {
        "x": nrm(ks[0], (BATCH, SEQ, D), F32),
        "c": nrm(ks[1], (BATCH, D), F32),
        "rel_bias": 0.5 * nrm(ks[2], (N_BUCKETS, A_HEADS), F32),
        "ada_w": 0.5 * D ** -0.5 * nrm(ks[3], (DEPTH, D, 6 * D), F32),
        "ada_b": 0.02 * nrm(ks[4], (DEPTH, 6 * D), F32),
        "norm_g": 1.0 + 0.05 * nrm(ks[5], (DEPTH, 4, D), F32),
        "w_in": D ** -0.5 * nrm(ks[6], (DEPTH, D, IN_W), F32),
        "gate_b": gate_b,
        "conv_w": CONV_K ** -0.5 * nrm(ks[10], (DEPTH, CONV_K, 2 * M_WIDTH), F32),
        "conv_b": 0.02 * nrm(ks[11], (DEPTH, 2 * M_WIDTH), F32),
        "lam_vec": 0.1 * nrm(ks[12], (DEPTH, 4, A_DH), F32),
        "attn_norm_g": 1.0 + 0.05 * nrm(ks[13], (DEPTH, A_DV), F32),
        "mlstm_norm_g": 1.0 + 0.05 * nrm(ks[14], (DEPTH, M_WIDTH), F32),
        "w_out": MIX ** -0.5 * nrm(ks[15], (DEPTH, MIX, D), F32),
        "ffn_w1": D ** -0.5 * nrm(ks[16], (n_dense, D, D_FF), F32),
        "ffn_w3": D ** -0.5 * nrm(ks[17], (n_dense, D, D_FF), F32),
        "ffn_w2": D_FF ** -0.5 * nrm(ks[18], (n_dense, D_FF, D), F32),
        "router_w": D ** -0.5 * nrm(ks[19], (n_moe, D, N_EXPERTS), F32),
        "moe_w1": D ** -0.5 * nrm(ks[20], (n_moe, N_EXPERTS, D, D_FF), F32),
        "moe_w3": D ** -0.5 * nrm(ks[21], (n_moe, N_EXPERTS, D, D_FF), F32),
        "moe_w2": D_FF ** -0.5 * nrm(ks[22], (n_moe, N_EXPERTS, D_FF, D), F32),
    }


def reference(x, c, rel_bias, ada_w, ada_b, norm_g, w_in, gate_b, conv_w, conv_b, lam_vec,
              attn_norm_g, mlstm_norm_g, w_out, ffn_w1, ffn_w3, ffn_w2, router_w,
              moe_w1, moe_w3, moe_w2):
    B, S, D = x.shape
    dist_bias = rel_bias[t5_bucket(jnp.arange(S, dtype=jnp.int32))].T.astype(F32)
    c_act = jax.nn.silu(c)
    for i in range(DEPTH):
        mod = c_act @ ada_w[i] + ada_b[i]
        sh1, sc1, g1, sh2, sc2, g2 = jnp.split(mod, 6, axis=-1)

        h = rmsnorm(x, norm_g[i, 0]) * (1.0 + sc1[:, None]) + sh1[:, None]
        proj = h @ w_in[i]
        aq, ak, av, mq, mk, mv, mo, mi, mf = jnp.split(proj, IN_SPLITS, axis=-1)

        lv = lam_vec[i].astype(F32)
        lam_init = 0.8 - 0.6 * math.exp(-0.3 * i)
        lam = jnp.exp(jnp.sum(lv[0] * lv[1])) - jnp.exp(jnp.sum(lv[2] * lv[3])) + lam_init
        a_out = diff_attention(aq.reshape(B, S, A_HEADS, 2, A_DH),
                               ak.reshape(B, S, A_HEADS, 2, A_DH),
                               av.reshape(B, S, A_HEADS, A_DV),
                               lam, lam_init, dist_bias, attn_norm_g[i])

        qk = jax.nn.silu(causal_dwconv(jnp.concatenate([mq, mk], axis=-1), conv_w[i], conv_b[i]))
        mq_c, mk_c = jnp.split(qk, 2, axis=-1)
        h_t = mlstm_chunkwise(mq_c.reshape(B, S, M_HEADS, M_DH),
                              mk_c.reshape(B, S, M_HEADS, M_DH),
                              mv.reshape(B, S, M_HEADS, M_DH),
                              mi + gate_b[i, 0], mf + gate_b[i, 1])
        h_m = jax.nn.sigmoid(mo.astype(F32)).reshape(B, S, M_HEADS, M_DH) * h_t
        h_m = rmsnorm(h_m, mlstm_norm_g[i].reshape(M_HEADS, M_DH)).astype(x.dtype)

        mixed = jnp.concatenate([a_out.reshape(B, S, A_WIDTH), h_m.reshape(B, S, M_WIDTH)], axis=-1)
        y = mixed @ w_out[i]
        x = x + g1[:, None] * rmsnorm(y, norm_g[i, 1])

        h = rmsnorm(x, norm_g[i, 2]) * (1.0 + sc2[:, None]) + sh2[:, None]
        if i % 2 == 0:
            j = i // 2
            y = swiglu(h, ffn_w1[j], ffn_w3[j], ffn_w2[j])
        else:
            j = i // 2
            y = moe_swiglu(h, router_w[j], moe_w1[j], moe_w3[j], moe_w2[j])
        x = x + g2[:, None] * rmsnorm(y, norm_g[i, 3])
    return x
```

```python
import functools
import math

import numpy as np
import jax
import jax.numpy as jnp
from jax import lax
from jax.experimental import pallas as pl
from jax.experimental.pallas import tpu as pltpu

F32 = jnp.float32
BF16 = jnp.bfloat16
HIGHEST = lax.Precision.HIGHEST

A_HEADS = 4
A_DH = 64
M_HEADS = 4
M_DH = 128
CONV_K = 4
N_BUCKETS = 32
MAX_DIST = 128
N_EXPERTS = 8
EPS = 1e-6

LANES = 128
HEAD_GROUP_W = 512
N_SLABS = 7
NEG = -1e30
VMEM_LIMIT_BYTES = 56 * 1024 * 1024

ATTN_BLOCK = 512
MLSTM_CHUNK = 256
ROW_TILE = 512
FFN_ROW_TILE = 1024
FFN_COL_TILE = 512
ADA_COL_TILE = 1536


def _params(*sem):
    return pltpu.CompilerParams(dimension_semantics=sem, vmem_limit_bytes=VMEM_LIMIT_BYTES)


def _rms(x, g):
    return x * lax.rsqrt(jnp.mean(x * x, axis=-1, keepdims=True) + EPS) * g


def _sigmoid(x):
    return 1.0 / (1.0 + jnp.exp(-x))


def _log_sigmoid(x):
    return jnp.minimum(x, 0.0) - jnp.log(1.0 + jnp.exp(-jnp.abs(x)))


def _pick_lane(x, idx):
    lane = lax.broadcasted_iota(jnp.int32, x.shape, 1)
    return jnp.sum(jnp.where(lane == idx, x, 0.0), axis=1, keepdims=True)


def _pick_row(x, idx):
    row = lax.broadcasted_iota(jnp.int32, x.shape, 0)
    return jnp.sum(jnp.where(row == idx, x, 0.0), axis=0, keepdims=True)


def _ada_kernel(c_ref, w_ref, b_ref, o_ref):
    c = c_ref[...]
    ca = c * _sigmoid(c)
    o_ref[0] = jnp.dot(ca, w_ref[0], precision=HIGHEST, preferred_element_type=F32) + b_ref[0]


def _ada(c, ada_w, ada_b):
    depth, d, d6 = ada_w.shape
    b = c.shape[0]
    rows = 8
    cp = jnp.zeros((rows, d), F32).at[:b].set(c)
    out = pl.pallas_call(
        _ada_kernel,
        out_shape=jax.ShapeDtypeStruct((depth, rows, d6), F32),
        grid=(depth, d6 // ADA_COL_TILE),
        in_specs=[
            pl.BlockSpec((rows, d), lambda l, j: (0, 0)),
            pl.BlockSpec((1, d, ADA_COL_TILE), lambda l, j: (l, 0, j)),
            pl.BlockSpec((1, 1, ADA_COL_TILE), lambda l, j: (l, 0, j)),
        ],
        out_specs=pl.BlockSpec((1, rows, ADA_COL_TILE), lambda l, j: (l, 0, j)),
        compiler_params=_params("parallel", "parallel"),
        name="ada_mod",
    )(cp, ada_w, ada_b.reshape(depth, 1, d6))
    return out[:, :b].reshape(depth, b, 6, d)


def _inproj_kernel(x_ref, mod_ref, g_ref, w_ref, wg_ref, wgt_ref, gb_ref, gbt_ref,
                   aq, ak, av, mq, mk, mv, mo, grow, gcol):
    mod = mod_ref[0]
    h = _rms(x_ref[...], g_ref[...]) * (1.0 + mod[1:2]) + mod[0:1]
    hb = h.astype(BF16)
    for s, o in enumerate((aq, ak, av, mq, mk, mv, mo)):
        w = w_ref[:, s * HEAD_GROUP_W:(s + 1) * HEAD_GROUP_W]
        o[...] = jnp.dot(hb, w, preferred_element_type=F32).astype(BF16)
    grow[...] = jnp.dot(hb, wg_ref[...], preferred_element_type=F32) + gb_ref[...]
    gcol[0] = lax.dot_general(wgt_ref[...], hb, (((1,), (1,)), ((), ())),
                              preferred_element_type=F32) + gbt_ref[...]


def _inproj(xf, mod, g, w_main, wg, wgt, gb, gbt, batch, seq):
    n, d = xf.shape
    tm = ROW_TILE
    tpb = seq // tm
    slab = jax.ShapeDtypeStruct((n, HEAD_GROUP_W), BF16)
    row_spec = pl.BlockSpec((tm, HEAD_GROUP_W), lambda i: (i, 0))
    return pl.pallas_call(
        _inproj_kernel,
        out_shape=[slab] * N_SLABS + [jax.ShapeDtypeStruct((n, LANES), F32),
                                      jax.ShapeDtypeStruct((batch, 8, seq), F32)],
        grid=(n // tm,),
        in_specs=[
            pl.BlockSpec((tm, d), lambda i: (i, 0)),
            pl.BlockSpec((1, 6, d), lambda i: (i // tpb, 0, 0)),
            pl.BlockSpec((1, d), lambda i: (0, 0)),
            pl.BlockSpec(w_main.shape, lambda i: (0, 0)),
            pl.BlockSpec(wg.shape, lambda i: (0, 0)),
            pl.BlockSpec(wgt.shape, lambda i: (0, 0)),
            pl.BlockSpec(gb.shape, lambda i: (0, 0)),
            pl.BlockSpec(gbt.shape, lambda i: (0, 0)),
        ],
        out_specs=[row_spec] * N_SLABS + [pl.BlockSpec((tm, LANES), lambda i: (i, 0)),
                                          pl.BlockSpec((1, 8, tm), lambda i: (i // tpb, 0, i % tpb))],
        compiler_params=_params("parallel"),
        name="in_proj",
    )(xf, mod, g, w_main, wg, wgt, gb, gbt)


def _attn_kernel(scal_ref, q_ref, k_ref, v_ref, bias_ref, lv_ref, g_ref, o_ref, m_sc, acc_sc, *, blk):
    h = pl.program_id(1)
    qi = pl.program_id(2)
    far_bias = scal_ref[h]
    lam_init = scal_ref[A_HEADS]

    lane = lax.broadcasted_iota(jnp.int32, (blk, LANES), 1)
    qs = q_ref[0] * jnp.asarray(A_DH ** -0.5, BF16)
    zero = jnp.zeros_like(qs)
    qq = jnp.concatenate([jnp.where(lane < A_DH, qs, zero), jnp.where(lane >= A_DH, qs, zero)], axis=0)
    ones_col = (lane == 0).astype(BF16)

    m_sc[...] = jnp.full(m_sc.shape, NEG, F32)
    acc_sc[...] = jnp.zeros(acc_sc.shape, F32)

    def tile(kj, bias):
        start = pl.multiple_of(kj * blk, blk)
        kt = k_ref[0, pl.ds(start, blk), :]
        vt = v_ref[0, pl.ds(start, blk), :]
        s = lax.dot_general(qq, kt, (((1,), (1,)), ((), ())), preferred_element_type=F32)
        if bias is None:
            off = far_bias
        else:
            s = (s.reshape(2, blk, blk) + bias[None]).reshape(2 * blk, blk)
            off = 0.0
        m_old = m_sc[...]
        m_new = jnp.maximum(m_old, jnp.max(s, axis=1, keepdims=True) + off)
        alpha = jnp.exp(m_old - m_new)
        p = jnp.exp(s - (m_new - off))
        vext = jnp.concatenate([vt, ones_col], axis=1)
        pv = jnp.dot(p.astype(BF16), vext, preferred_element_type=F32)
        acc_sc[...] = alpha * acc_sc[...] + pv
        m_sc[...] = m_new

    def far_body(kj, carry):
        tile(kj, None)
        return carry

    lax.fori_loop(0, qi - 1, far_body, 0)

    @pl.when(qi >= 1)
    def _():
        tile(qi - 1, bias_ref[0, 1])

    tile(qi, bias_ref[0, 0])

    acc = acc_sc[...]
    o0 = acc[:blk, :LANES] * (1.0 / acc[:blk, LANES:LANES + 1])
    o1 = acc[blk:, :LANES] * (1.0 / acc[blk:, LANES:LANES + 1])
    lv = lv_ref[...]
    lam = (jnp.exp(jnp.sum(lv[0:1] * lv[1:2], axis=1, keepdims=True))
           - jnp.exp(jnp.sum(lv[2:3] * lv[3:4], axis=1, keepdims=True)) + lam_init)
    a = o0 - lam * o1
    o_ref[0] = (_rms(a, g_ref[...]) * (1.0 - lam_init)).astype(BF16)


def _attention(aq, ak, av, scal, bias_tiles, lam_vec, norm_g):
    batch, seq, _ = aq.shape
    blk = ATTN_BLOCK
    kernel = functools.partial(_attn_kernel, blk=blk)
    grid_spec = pltpu.PrefetchScalarGridSpec(
        num_scalar_prefetch=1,
        grid=(batch, A_HEADS, seq // blk),
        in_specs=[
            pl.BlockSpec((1, blk, LANES), lambda b, h, i, s: (b, i, h)),
            pl.BlockSpec((1, seq, LANES), lambda b, h, i, s: (b, 0, h)),
            pl.BlockSpec((1, seq, LANES), lambda b, h, i, s: (b, 0, h)),
            pl.BlockSpec((1, 2, blk, blk), lambda b, h, i, s: (h, 0, 0, 0)),
            pl.BlockSpec(lam_vec.shape, lambda b, h, i, s: (0, 0)),
            pl.BlockSpec(norm_g.shape, lambda b, h, i, s: (0, 0)),
        ],
        out_specs=pl.BlockSpec((1, blk, LANES), lambda b, h, i, s: (b, i, h)),
        scratch_shapes=[pltpu.VMEM((2 * blk, 1), F32), pltpu.VMEM((2 * blk, 2 * LANES), F32)],
    )
    return pl.pallas_call(
        kernel,
        out_shape=jax.ShapeDtypeStruct(aq.shape, BF16),
        grid_spec=grid_spec,
        compiler_params=_params("parallel", "parallel", "arbitrary"),
        name="diff_attn",
    )(scal, aq, ak, av, bias_tiles, lam_vec, norm_g)


def _t5_bucket(n):
    max_exact = N_BUCKETS // 2
    nf = jnp.maximum(n, 1).astype(F32)
    large = max_exact + (jnp.log(nf / max_exact) / math.log(MAX_DIST / max_exact)
                         * (N_BUCKETS - max_exact)).astype(jnp.int32)
    large = jnp.minimum(large, N_BUCKETS - 1)
    return jnp.where(n < max_exact, n, large)


def _bias_tables(rel_bias, seq):
    blk = ATTN_BLOCK
    far = np.arange(blk + 1, max(seq, blk + 2), dtype=np.float64)
    sat = (N_BUCKETS // 2) + np.log(far / (N_BUCKETS // 2)) / math.log(MAX_DIST / (N_BUCKETS // 2)) * (N_BUCKETS // 2)
    assert np.all(sat >= N_BUCKETS - 0.5), "far tiles need a saturated distance bucket"
    dist_bias = rel_bias[_t5_bucket(jnp.arange(seq, dtype=jnp.int32))].T.astype(F32)
    i = jnp.arange(blk)[:, None]
    j = jnp.arange(blk)[None, :]
    tiles = []
    for d in (0, 1):
        dd = d * blk + i - j
        t = dist_bias[:, jnp.clip(dd, 0, seq - 1)]
        tiles.append(jnp.where(dd >= 0, t, NEG))
    return jnp.stack(tiles, axis=1), dist_bias[:, seq - 1]


def _mlstm_kernel(mq_ref, mk_ref, mv_ref, mo_ref, grow_ref, gcol_ref, cwq_ref, cwk_ref, cbq_ref, cbk_ref,
                  ng_ref, o_ref, xq_sc, xk_sc, c_sc, n_sc, m_sc, *, chunk):
    L = chunk
    h = pl.program_id(1)
    ci = pl.program_id(2)

    @pl.when(ci == 0)
    def _():
        xq_sc[0:8, :] = jnp.zeros((8, LANES), F32)
        xk_sc[0:8, :] = jnp.zeros((8, LANES), F32)
        c_sc[...] = jnp.zeros(c_sc.shape, F32)
        n_sc[...] = jnp.zeros(n_sc.shape, F32)
        m_sc[...] = jnp.zeros(m_sc.shape, F32)

    def conv_silu(x_ref, sc, w_ref, b_ref):
        sc[8:8 + L, :] = x_ref[0].astype(F32)
        w = w_ref[0]
        y = b_ref[0]
        for j in range(CONV_K):
            y = y + sc[5 + j:5 + j + L, :] * w[j:j + 1]
        sc[0:8, :] = sc[L:L + 8, :]
        return y * _sigmoid(y)

    q = conv_silu(mq_ref, xq_sc, cwq_ref, cbq_ref)
    k = conv_silu(mk_ref, xk_sc, cwk_ref, cbk_ref) * (M_DH ** -0.5)
    v = mv_ref[0]

    r = lax.broadcasted_iota(jnp.int32, (L, L), 0)
    cidx = lax.broadcasted_iota(jnp.int32, (L, L), 1)
    causal = r >= cidx
    lower = causal.astype(F32)
    upper = (r <= cidx).astype(F32)

    g_rows = grow_ref[...]
    cum_rows = jnp.dot(lower, _log_sigmoid(g_rows), precision=HIGHEST, preferred_element_type=F32)
    ig_col = _pick_lane(g_rows, h)
    b_col = _pick_lane(cum_rows, M_HEADS + h)
    g_cols = gcol_ref[0]
    cum_cols = jnp.dot(_log_sigmoid(g_cols), upper, precision=HIGHEST, preferred_element_type=F32)
    ig_row = _pick_row(g_cols, h)
    b_row = _pick_row(cum_cols, M_HEADS + h)
    b_last = b_col[L - 1:L, :]
    m_prev = m_sc[...]

    dlog = jnp.where(causal, b_col - b_row + ig_row, NEG)
    inter = b_col + m_prev
    m_t = jnp.maximum(inter, jnp.max(dlog, axis=1, keepdims=True))
    qb = q.astype(BF16)
    kb = k.astype(BF16)
    sw = lax.dot_general(qb, kb, (((1,), (1,)), ((), ())), preferred_element_type=F32) * jnp.exp(dlog - m_t)
    dec = jnp.exp(inter - m_t)
    num = (dec * jnp.dot(qb, c_sc[...].astype(BF16), preferred_element_type=F32)
           + jnp.dot(sw.astype(BF16), v, preferred_element_type=F32))
    den = dec * jnp.sum(q * n_sc[...], axis=1, keepdims=True) + jnp.sum(sw, axis=1, keepdims=True)
    h_t = num * (1.0 / jnp.maximum(jnp.abs(den), jnp.exp(-m_t)))

    a_col = b_last - b_col + ig_col
    m_new = jnp.maximum(b_last + m_prev, jnp.max(a_col, axis=0, keepdims=True))
    decay = jnp.exp(b_last + m_prev - m_new)
    kw = k * jnp.exp(a_col - m_new)
    c_sc[...] = decay * c_sc[...] + lax.dot_general(kw.astype(BF16), v, (((0,), (0,)), ((), ())),
                                                    preferred_element_type=F32)
    n_sc[...] = decay * n_sc[...] + jnp.sum(kw, axis=0, keepdims=True)
    m_sc[...] = m_new

    hm = _sigmoid(mo_ref[0].astype(F32)) * h_t
    o_ref[0] = _rms(hm, ng_ref[0]).astype(BF16)


def _mlstm(mq, mk, mv, mo, grow, gcol, conv_w, conv_b, norm_g):
    batch, seq, _ = mq.shape
    L = MLSTM_CHUNK
    kernel = functools.partial(_mlstm_kernel, chunk=L)
    slab = pl.BlockSpec((1, L, LANES), lambda b, h, c: (b, c, h))
    nchunk = seq // L
    return pl.pallas_call(
        kernel,
        out_shape=jax.ShapeDtypeStruct(mq.shape, BF16),
        grid=(batch, M_HEADS, nchunk),
        in_specs=[
            slab, slab, slab, slab,
            pl.BlockSpec((L, LANES), lambda b, h, c: (b * nchunk + c, 0)),
            pl.BlockSpec((1, 8, L), lambda b, h, c: (b, 0, c)),
            pl.BlockSpec((1, CONV_K, LANES), lambda b, h, c: (h, 0, 0)),
            pl.BlockSpec((1, CONV_K, LANES), lambda b, h, c: (M_HEADS + h, 0, 0)),
            pl.BlockSpec((1, 1, LANES), lambda b, h, c: (h, 0, 0)),
            pl.BlockSpec((1, 1, LANES), lambda b, h, c: (M_HEADS + h, 0, 0)),
            pl.BlockSpec((1, 1, LANES), lambda b, h, c: (h, 0, 0)),
        ],
        out_specs=slab,
        scratch_shapes=[pltpu.VMEM((L + 8, LANES), F32), pltpu.VMEM((L + 8, LANES), F32),
                        pltpu.VMEM((M_DH, M_DH), F32), pltpu.VMEM((1, M_DH), F32), pltpu.VMEM((1, 1), F32)],
        compiler_params=_params("parallel", "parallel", "arbitrary"),
        name="mlstm",
    )(mq, mk, mv, mo, grow, gcol, conv_w, conv_w, conv_b, conv_b, norm_g)


def _outproj_kernel(*refs, moe):
    if moe:
        a_ref, hm_ref, x_ref, mod_ref, ng_ref, wt_ref, wb_ref, rw_ref, x1_ref, h2_ref, gates_ref = refs
    else:
        a_ref, hm_ref, x_ref, mod_ref, ng_ref, wt_ref, wb_ref, x1_ref, h2_ref = refs
    y = (jnp.dot(a_ref[...], wt_ref[...], preferred_element_type=F32)
         + jnp.dot(hm_ref[...], wb_ref[...], preferred_element_type=F32))
    mod = mod_ref[0]
    ng = ng_ref[...]
    x1 = x_ref[...] + mod[2:3] * _rms(y, ng[1:2])
    x1_ref[...] = x1
    h2 = _rms(x1, ng[2:3]) * (1.0 + mod[4:5]) + mod[3:4]
    h2_ref[...] = h2.astype(BF16)
    if moe:
        logits = jnp.dot(h2, rw_ref[...], precision=HIGHEST, preferred_element_type=F32)
        lane = lax.broadcasted_iota(jnp.int32, logits.shape, 1)
        lg = jnp.where(lane < N_EXPERTS, logits, NEG)
        v1 = jnp.max(lg, axis=1, keepdims=True)
        i1 = jnp.min(jnp.where(lg == v1, lane, LANES), axis=1, keepdims=True)
        lg2 = jnp.where(lane == i1, NEG, lg)
        v2 = jnp.max(lg2, axis=1, keepdims=True)
        i2 = jnp.min(jnp.where(lg2 == v2, lane, LANES), axis=1, keepdims=True)
        e2 = jnp.exp(v2 - v1)
        w1 = 1.0 / (1.0 + e2)
        gates_ref[...] = jnp.where(lane == i1, w1, 0.0) + jnp.where(lane == i2, e2 * w1, 0.0)


def _outproj(a_out, h_m, xf, mod, ng, w_top, w_bot, router_w, batch, seq):
    n, d = xf.shape
    tm = ROW_TILE
    tpb = seq // tm
    moe = router_w is not None
    half = pl.BlockSpec((tm, HEAD_GROUP_W), lambda i: (i, 0))
    full = pl.BlockSpec((tm, d), lambda i: (i, 0))
    in_specs = [half, half, full,
                pl.BlockSpec((1, 6, d), lambda i: (i // tpb, 0, 0)),
                pl.BlockSpec(ng.shape, lambda i: (0, 0)),
                pl.BlockSpec(w_top.shape, lambda i: (0, 0)),
                pl.BlockSpec(w_bot.shape, lambda i: (0, 0))]
    args = [a_out, h_m, xf, mod, ng, w_top, w_bot]
    out_shape = [jax.ShapeDtypeStruct((n, d), F32), jax.ShapeDtypeStruct((n, d), BF16)]
    out_specs = [full, full]
    if moe:
        in_specs.append(pl.BlockSpec(router_w.shape, lambda i: (0, 0)))
        args.append(router_w)
        out_shape.append(jax.ShapeDtypeStruct((n, LANES), F32))
        out_specs.append(pl.BlockSpec((tm, LANES), lambda i: (i, 0)))
    return pl.pallas_call(
        functools.partial(_outproj_kernel, moe=moe),
        out_shape=out_shape,
        grid=(n // tm,),
        in_specs=in_specs,
        out_specs=out_specs,
        compiler_params=_params("parallel"),
        name="out_proj_moe" if moe else "out_proj",
    )(*args)


def _ffn_kernel(*refs, moe, n_col, n_exp):
    if moe:
        h_ref, x_ref, mod_ref, ng_ref, gates_ref, w1_ref, w3_ref, w2_ref, o_ref, acc_sc = refs
        e = pl.program_id(1)
        f = pl.program_id(2)
        first = jnp.logical_and(e == 0, f == 0)
        last = jnp.logical_and(e == n_exp - 1, f == n_col - 1)
    else:
        h_ref, x_ref, mod_ref, ng_ref, w1_ref, w3_ref, w2_ref, o_ref, acc_sc = refs
        f = pl.program_id(1)
        first = f == 0
        last = f == n_col - 1

    @pl.when(first)
    def _():
        acc_sc[...] = jnp.zeros(acc_sc.shape, F32)

    hb = h_ref[...]
    a = jnp.dot(hb, w1_ref[0], preferred_element_type=F32)
    b = jnp.dot(hb, w3_ref[0], preferred_element_type=F32)
    hh = a * _sigmoid(a) * b
    if moe:
        hh = hh * _pick_lane(gates_ref[...], e)
    acc_sc[...] += jnp.dot(hh.astype(BF16), w2_ref[0], preferred_element_type=F32)

    @pl.when(last)
    def _():
        mod = mod_ref[0]
        o_ref[...] = x_ref[...] + mod[5:6] * _rms(acc_sc[...], ng_ref[3:4, :])


def _ffn(h2, x1, mod, ng, gates, w1, w3, w2, batch, seq):
    n, d = x1.shape
    n_exp, _, dff = w1.shape
    tm, tf = FFN_ROW_TILE, FFN_COL_TILE
    tpb = seq // tm
    n_col = dff // tf
    moe = gates is not None
    if moe:
        grid = (n // tm, n_exp, n_col)
        row = lambda i, e, f: (i, 0)
        modm = lambda i, e, f: (i // tpb, 0, 0)
        const = lambda i, e, f: (0, 0)
        wup = lambda i, e, f: (e, 0, f)
        wdn = lambda i, e, f: (e, f, 0)
        sem = ("parallel", "arbitrary", "arbitrary")
    else:
        grid = (n // tm, n_col)
        row = lambda i, f: (i, 0)
        modm = lambda i, f: (i // tpb, 0, 0)
        const = lambda i, f: (0, 0)
        wup = lambda i, f: (0, 0, f)
        wdn = lambda i, f: (0, f, 0)
        sem = ("parallel", "arbitrary")
    in_specs = [pl.BlockSpec((tm, d), row), pl.BlockSpec((tm, d), row),
                pl.BlockSpec((1, 6, d), modm), pl.BlockSpec(ng.shape, const)]
    args = [h2, x1, mod, ng]
    if moe:
        in_specs.append(pl.BlockSpec((tm, LANES), row))
        args.append(gates)
    in_specs += [pl.BlockSpec((1, d, tf), wup), pl.BlockSpec((1, d, tf), wup), pl.BlockSpec((1, tf, d), wdn)]
    args += [w1, w3, w2]
    return pl.pallas_call(
        functools.partial(_ffn_kernel, moe=moe, n_col=n_col, n_exp=n_exp),
        out_shape=jax.ShapeDtypeStruct((n, d), F32),
        grid=grid,
        in_specs=in_specs,
        out_specs=pl.BlockSpec((tm, d), row),
        scratch_shapes=[pltpu.VMEM((tm, d), F32)],
        compiler_params=_params(*sem),
        name="moe_ffn" if moe else "dense_ffn",
    )(*args)


@jax.jit
def _forward(x, c, rel_bias, ada_w, ada_b, norm_g, w_in, gate_b, conv_w, conv_b, lam_vec,
             attn_norm_g, mlstm_norm_g, w_out, ffn_w1, ffn_w3, ffn_w2, router_w, moe_w1, moe_w3, moe_w2):
    batch, seq, d = x.shape
    depth = ada_w.shape[0]
    n = batch * seq
    n_main = N_SLABS * HEAD_GROUP_W
    assert seq % ATTN_BLOCK == 0 and seq % FFN_ROW_TILE == 0 and seq % MLSTM_CHUNK == 0
    assert w_in.shape[2] == n_main + 2 * M_HEADS

    mod_all = _ada(c, ada_w, ada_b)
    bias_tiles, far_bias = _bias_tables(rel_bias, seq)

    xf = x.reshape(n, d)
    for i in range(depth):
        mod = mod_all[i]
        ng = norm_g[i]
        lam_init = 0.8 - 0.6 * math.exp(-0.3 * i)

        w_gate = w_in[i][:, n_main:]
        slabs = _inproj(
            xf, mod, ng[0:1], w_in[i][:, :n_main].astype(BF16),
            jnp.zeros((d, LANES), BF16).at[:, :2 * M_HEADS].set(w_gate.astype(BF16)),
            w_gate.T.astype(BF16),
            jnp.zeros((1, LANES), F32).at[0, :2 * M_HEADS].set(gate_b[i].reshape(-1)),
            gate_b[i].reshape(2 * M_HEADS, 1), batch, seq)
        aq, ak, av, mq, mk, mv, mo = (s.reshape(batch, seq, HEAD_GROUP_W) for s in slabs[:N_SLABS])
        grow, gcol = slabs[N_SLABS:]

        scal = jnp.concatenate([far_bias, jnp.full((1,), lam_init, F32)])
        a_out = _attention(aq, ak, av, scal, bias_tiles, lam_vec[i], attn_norm_g[i].reshape(1, LANES))

        h_m = _mlstm(mq, mk, mv, mo, grow, gcol,
                     conv_w[i].reshape(CONV_K, 2 * M_HEADS, LANES).transpose(1, 0, 2),
                     conv_b[i].reshape(2 * M_HEADS, 1, LANES),
                     mlstm_norm_g[i].reshape(M_HEADS, 1, LANES))

        w_o = w_out[i].astype(BF16)
        j = i // 2
        if i % 2 == 0:
            x1, h2 = _outproj(a_out.reshape(n, -1), h_m.reshape(n, -1), xf, mod, ng,
                              w_o[:HEAD_GROUP_W], w_o[HEAD_GROUP_W:], None, batch, seq)
            xf = _ffn(h2, x1, mod, ng, None, ffn_w1[j][None].astype(BF16), ffn_w3[j][None].astype(BF16),
                      ffn_w2[j][None].astype(BF16), batch, seq)
        else:
            rw = jnp.zeros((d, LANES), F32).at[:, :N_EXPERTS].set(router_w[j])
            x1, h2, gates = _outproj(a_out.reshape(n, -1), h_m.reshape(n, -1), xf, mod, ng,
                                     w_o[:HEAD_GROUP_W], w_o[HEAD_GROUP_W:], rw, batch, seq)
            xf = _ffn(h2, x1, mod, ng, gates, moe_w1[j].astype(BF16), moe_w3[j].astype(BF16),
                      moe_w2[j].astype(BF16), batch, seq)
    return xf.reshape(batch, seq, d)


def kernel(x, c, rel_bias, ada_w, ada_b, norm_g, w_in, gate_b, conv_w, conv_b, lam_vec, attn_norm_g,
           mlstm_norm_g, w_out, ffn_w1, ffn_w3, ffn_w2, router_w, moe_w1, moe_w3, moe_w2):
    return _forward(x, c, rel_bias, ada_w, ada_b, norm_g, w_in, gate_b, conv_w, conv_b, lam_vec,
                    attn_norm_g, mlstm_norm_g, w_out, ffn_w1, ffn_w3, ffn_w2, router_w, moe_w1, moe_w3, moe_w2)
```

```python
import functools
import math

import numpy as np
import jax
import jax.numpy as jnp
from jax import lax
from jax.experimental import pallas as pl
from jax.experimental.pallas import tpu as pltpu

F32 = jnp.float32
BF16 = jnp.bfloat16
HIGHEST = lax.Precision.HIGHEST

A_HEADS = 4
A_DH = 64
M_HEADS = 4
M_DH = 128
CONV_K = 4
N_BUCKETS = 32
MAX_DIST = 128
N_EXPERTS = 8
EPS = 1e-6

LANES = 128
HEAD_GROUP_W = 512
N_SLABS = 7
NEG = -1e30
VMEM_LIMIT_BYTES = 56 * 1024 * 1024

ATTN_BLOCK = 512
MLSTM_CHUNK = 256
ROW_TILE = 512
FFN_ROW_TILE = 1024
FFN_COL_TILE = 512
ADA_COL_TILE = 1536


def _params(*sem):
    return pltpu.CompilerParams(dimension_semantics=sem, vmem_limit_bytes=VMEM_LIMIT_BYTES)


def _rms(x, g):
    return x * lax.rsqrt(jnp.mean(x * x, axis=-1, keepdims=True) + EPS) * g


def _sigmoid(x):
    return 1.0 / (1.0 + jnp.exp(-x))


def _log_sigmoid(x):
    return jnp.minimum(x, 0.0) - jnp.log(1.0 + jnp.exp(-jnp.abs(x)))


def _pick_lane(x, idx):
    lane = lax.broadcasted_iota(jnp.int32, x.shape, 1)
    return jnp.sum(jnp.where(lane == idx, x, 0.0), axis=1, keepdims=True)


def _pick_row(x, idx):
    row = lax.broadcasted_iota(jnp.int32, x.shape, 0)
    return jnp.sum(jnp.where(row == idx, x, 0.0), axis=0, keepdims=True)


def _ada_kernel(c_ref, w_ref, b_ref, o_ref):
    c = c_ref[...]
    ca = c * _sigmoid(c)
    o_ref[0] = jnp.dot(ca, w_ref[0], precision=HIGHEST, preferred_element_type=F32) + b_ref[0]


def _ada(c, ada_w, ada_b):
    depth, d, d6 = ada_w.shape
    b = c.shape[0]
    rows = 8
    cp = jnp.zeros((rows, d), F32).at[:b].set(c)
    out = pl.pallas_call(
        _ada_kernel,
        out_shape=jax.ShapeDtypeStruct((depth, rows, d6), F32),
        grid=(depth, d6 // ADA_COL_TILE),
        in_specs=[
            pl.BlockSpec((rows, d), lambda l, j: (0, 0)),
            pl.BlockSpec((1, d, ADA_COL_TILE), lambda l, j: (l, 0, j)),
            pl.BlockSpec((1, 1, ADA_COL_TILE), lambda l, j: (l, 0, j)),
        ],
        out_specs=pl.BlockSpec((1, rows, ADA_COL_TILE), lambda l, j: (l, 0, j)),
        compiler_params=_params("parallel", "parallel"),
        name="ada_mod",
    )(cp, ada_w, ada_b.reshape(depth, 1, d6))
    return out[:, :b].reshape(depth, b, 6, d)


def _inproj_kernel(x_ref, mod_ref, g_ref, w_ref, wvt_ref, wg_ref, wgt_ref, gb_ref, gbt_ref,
                   aq, ak, mq, mk, mv, mo, avt, grow, gcol):
    mod = mod_ref[0]
    h = _rms(x_ref[...], g_ref[...]) * (1.0 + mod[1:2]) + mod[0:1]
    hb = h.astype(BF16)
    for s, o in enumerate((aq, ak, mq, mk, mv, mo)):
        w = w_ref[:, s * HEAD_GROUP_W:(s + 1) * HEAD_GROUP_W]
        o[...] = jnp.dot(hb, w, preferred_element_type=F32).astype(BF16)
    avt[0, 0] = lax.dot_general(wvt_ref[...], hb, (((1,), (1,)), ((), ())),
                                preferred_element_type=F32).astype(BF16)
    grow[...] = jnp.dot(hb, wg_ref[...], preferred_element_type=F32) + gb_ref[...]
    gcol[0] = lax.dot_general(wgt_ref[...], hb, (((1,), (1,)), ((), ())),
                              preferred_element_type=F32) + gbt_ref[...]


def _inproj(xf, mod, g, w_main, w_vt, wg, wgt, gb, gbt, batch, seq):
    n, d = xf.shape
    tm = ROW_TILE
    tpb = seq // tm
    n_row_slabs = N_SLABS - 1
    slab = jax.ShapeDtypeStruct((n, HEAD_GROUP_W), BF16)
    row_spec = pl.BlockSpec((tm, HEAD_GROUP_W), lambda i: (i, 0))
    return pl.pallas_call(
        _inproj_kernel,
        out_shape=[slab] * n_row_slabs + [jax.ShapeDtypeStruct((batch, tpb, HEAD_GROUP_W, tm), BF16),
                                          jax.ShapeDtypeStruct((n, LANES), F32),
                                          jax.ShapeDtypeStruct((batch, 8, seq), F32)],
        grid=(n // tm,),
        in_specs=[
            pl.BlockSpec((tm, d), lambda i: (i, 0)),
            pl.BlockSpec((1, 6, d), lambda i: (i // tpb, 0, 0)),
            pl.BlockSpec((1, d), lambda i: (0, 0)),
            pl.BlockSpec(w_main.shape, lambda i: (0, 0)),
            pl.BlockSpec(w_vt.shape, lambda i: (0, 0)),
            pl.BlockSpec(wg.shape, lambda i: (0, 0)),
            pl.BlockSpec(wgt.shape, lambda i: (0, 0)),
            pl.BlockSpec(gb.shape, lambda i: (0, 0)),
            pl.BlockSpec(gbt.shape, lambda i: (0, 0)),
        ],
        out_specs=[row_spec] * n_row_slabs + [
            pl.BlockSpec((1, 1, HEAD_GROUP_W, tm), lambda i: (i // tpb, i % tpb, 0, 0)),
            pl.BlockSpec((tm, LANES), lambda i: (i, 0)),
            pl.BlockSpec((1, 8, tm), lambda i: (i // tpb, 0, i % tpb))],
        compiler_params=_params("parallel"),
        name="in_proj",
    )(xf, mod, g, w_main, w_vt, wg, wgt, gb, gbt)


ACC_ROWS = LANES + 16


def _attn_kernel(scal_ref, q_ref, k_ref, vt_ref, bias_ref, lv_ref, g_ref, o_ref, m_sc, acc_sc,
                 sa_sc, sb_sc, mxa_sc, mxb_sc, *, blk):
    qi = pl.program_id(2)
    lam_init = scal_ref[0]

    lane = lax.broadcasted_iota(jnp.int32, (blk, LANES), 1)
    qs = q_ref[0] * jnp.asarray(A_DH ** -0.5, BF16)
    zero = jnp.zeros_like(qs)
    qz = (jnp.where(lane < A_DH, qs, zero), jnp.where(lane >= A_DH, qs, zero))
    ones_rows = jnp.ones((ACC_ROWS - LANES, blk), BF16)

    m_sc[...] = jnp.full(m_sc.shape, NEG, F32)
    acc_sc[...] = jnp.zeros(acc_sc.shape, F32)

    def scores(kj, s_sc, mx_sc):
        start = pl.multiple_of(kj * blk, blk)
        kt = k_ref[0, pl.ds(start, blk), :]
        bias = bias_ref[0, jnp.minimum(qi - kj, 2)]
        for c in range(2):
            s = lax.dot_general(kt, qz[c], (((1,), (1,)), ((), ())), preferred_element_type=F32) + bias
            s_sc[c] = s
            mx_sc[c] = jnp.max(s, axis=0, keepdims=True)

    def accumulate(kj, s_sc, mx_sc):
        vext = jnp.concatenate([vt_ref[0, kj], ones_rows], axis=0)
        for c in range(2):
            m_old = m_sc[c]
            m_new = jnp.maximum(m_old, mx_sc[c])
            alpha = jnp.exp(m_old - m_new)
            p = jnp.exp(s_sc[c] - m_new).astype(BF16)
            pv = jnp.dot(vext, p, preferred_element_type=F32)
            acc_sc[c] = alpha * acc_sc[c] + pv
            m_sc[c] = m_new

    scores(0, sa_sc, mxa_sc)

    def pair_body(u, carry):
        scores(2 * u + 1, sb_sc, mxb_sc)
        accumulate(2 * u, sa_sc, mxa_sc)
        scores(2 * u + 2, sa_sc, mxa_sc)
        accumulate(2 * u + 1, sb_sc, mxb_sc)
        return carry

    lax.fori_loop(0, qi // 2, pair_body, 0)

    @pl.when(qi % 2 == 1)
    def _():
        scores(qi, sb_sc, mxb_sc)
        accumulate(qi - 1, sa_sc, mxa_sc)
        accumulate(qi, sb_sc, mxb_sc)

    @pl.when(qi % 2 == 0)
    def _():
        accumulate(qi, sa_sc, mxa_sc)

    acc0 = acc_sc[0]
    acc1 = acc_sc[1]
    o0 = acc0[:LANES] * (1.0 / acc0[LANES:LANES + 1])
    o1 = acc1[:LANES] * (1.0 / acc1[LANES:LANES + 1])
    lv = lv_ref[...]
    lam = (jnp.exp(jnp.sum(lv[0:1] * lv[1:2], axis=1, keepdims=True))
           - jnp.exp(jnp.sum(lv[2:3] * lv[3:4], axis=1, keepdims=True)) + lam_init)
    a = o0 - lam * o1
    y = a * lax.rsqrt(jnp.mean(a * a, axis=0, keepdims=True) + EPS) * g_ref[...] * (1.0 - lam_init)
    o_ref[0] = y.T.astype(BF16)


def _attention(aq, ak, avt, scal, bias_tiles, lam_vec, norm_g):
    batch, seq, _ = aq.shape
    blk = ATTN_BLOCK
    nk = seq // blk
    kernel = functools.partial(_attn_kernel, blk=blk)
    grid_spec = pltpu.PrefetchScalarGridSpec(
        num_scalar_prefetch=1,
        grid=(batch, A_HEADS, nk),
        in_specs=[
            pl.BlockSpec((1, blk, LANES), lambda b, h, i, s: (b, i, h)),
            pl.BlockSpec((1, seq, LANES), lambda b, h, i, s: (b, 0, h)),
            pl.BlockSpec((1, nk, LANES, blk), lambda b, h, i, s: (b, 0, h, 0)),
            pl.BlockSpec((1, 3, blk, blk), lambda b, h, i, s: (h, 0, 0, 0)),
            pl.BlockSpec(lam_vec.shape, lambda b, h, i, s: (0, 0)),
            pl.BlockSpec(norm_g.shape, lambda b, h, i, s: (0, 0)),
        ],
        out_specs=pl.BlockSpec((1, blk, LANES), lambda b, h, i, s: (b, i, h)),
        scratch_shapes=[pltpu.VMEM((2, 1, blk), F32), pltpu.VMEM((2, ACC_ROWS, blk), F32),
                        pltpu.VMEM((2, blk, blk), F32), pltpu.VMEM((2, blk, blk), F32),
                        pltpu.VMEM((2, 1, blk), F32), pltpu.VMEM((2, 1, blk), F32)],
    )
    return pl.pallas_call(
        kernel,
        out_shape=jax.ShapeDtypeStruct(aq.shape, BF16),
        grid_spec=grid_spec,
        compiler_params=_params("parallel", "parallel", "arbitrary"),
        name="diff_attn",
    )(scal, aq, ak, avt, bias_tiles, lam_vec, norm_g)


def _t5_bucket(n):
    max_exact = N_BUCKETS // 2
    nf = jnp.maximum(n, 1).astype(F32)
    large = max_exact + (jnp.log(nf / max_exact) / math.log(MAX_DIST / max_exact)
                         * (N_BUCKETS - max_exact)).astype(jnp.int32)
    large = jnp.minimum(large, N_BUCKETS - 1)
    return jnp.where(n < max_exact, n, large)


def _bias_tables(rel_bias, seq):
    blk = ATTN_BLOCK
    far = np.arange(blk + 1, max(seq, blk + 2), dtype=np.float64)
    sat = (N_BUCKETS // 2) + np.log(far / (N_BUCKETS // 2)) / math.log(MAX_DIST / (N_BUCKETS // 2)) * (N_BUCKETS // 2)
    assert np.all(sat >= N_BUCKETS - 0.5), "far tiles need a saturated distance bucket"
    dist_bias = rel_bias[_t5_bucket(jnp.arange(seq, dtype=jnp.int32))].T.astype(F32)
    heads = dist_bias.shape[0]
    last = dist_bias[:, seq - 1:seq]
    padded = jnp.concatenate([jnp.full((heads, blk), NEG, F32), dist_bias,
                              jnp.broadcast_to(last, (heads, blk))], axis=1)
    tiles = []
    for d in (0, 1):
        base = blk + d * blk
        w = jnp.concatenate([padded[:, base:base + blk + 1], padded[:, base - blk + 1:base]], axis=1)
        t = jnp.tile(w, (1, blk))[:, :blk * (2 * blk - 1)].reshape(heads, blk, 2 * blk - 1)[:, :, :blk]
        tiles.append(t)
    tiles.append(jnp.broadcast_to(last[:, :, None], (heads, blk, blk)))
    return jnp.stack(tiles, axis=1)


def _mlstm_kernel(mq_ref, mk_ref, mv_ref, mo_ref, grow_ref, gcol_ref, cwq_ref, cwk_ref, cbq_ref, cbk_ref,
                  ng_ref, o_ref, xq_sc, xk_sc, c_sc, n_sc, m_sc, *, chunk):
    L = chunk
    h = pl.program_id(1)
    ci = pl.program_id(2)

    @pl.when(ci == 0)
    def _():
        xq_sc[0:8, :] = jnp.zeros((8, LANES), F32)
        xk_sc[0:8, :] = jnp.zeros((8, LANES), F32)
        c_sc[...] = jnp.zeros(c_sc.shape, F32)
        n_sc[...] = jnp.zeros(n_sc.shape, F32)
        m_sc[...] = jnp.zeros(m_sc.shape, F32)

    def conv_silu(x_ref, sc, w_ref, b_ref):
        sc[8:8 + L, :] = x_ref[0].astype(F32)
        w = w_ref[0]
        y = b_ref[0]
        for j in range(CONV_K):
            y = y + sc[5 + j:5 + j + L, :] * w[j:j + 1]
        sc[0:8, :] = sc[L:L + 8, :]
        return y * _sigmoid(y)

    q = conv_silu(mq_ref, xq_sc, cwq_ref, cbq_ref)
    k = conv_silu(mk_ref, xk_sc, cwk_ref, cbk_ref) * (M_DH ** -0.5)
    v = mv_ref[0]

    r = lax.broadcasted_iota(jnp.int32, (L, L), 0)
    cidx = lax.broadcasted_iota(jnp.int32, (L, L), 1)
    causal = r >= cidx
    lower = causal.astype(F32)
    upper = (r <= cidx).astype(F32)

    g_rows = grow_ref[...]
    cum_rows = jnp.dot(lower, _log_sigmoid(g_rows), precision=HIGHEST, preferred_element_type=F32)
    ig_col = _pick_lane(g_rows, h)
    b_col = _pick_lane(cum_rows, M_HEADS + h)
    g_cols = gcol_ref[0]
    cum_cols = jnp.dot(_log_sigmoid(g_cols), upper, precision=HIGHEST, preferred_element_type=F32)
    ig_row = _pick_row(g_cols, h)
    b_row = _pick_row(cum_cols, M_HEADS + h)
    b_last = b_col[L - 1:L, :]
    m_prev = m_sc[...]

    dlog = jnp.where(causal, b_col - b_row + ig_row, NEG)
    inter = b_col + m_prev
    m_t = jnp.maximum(inter, jnp.max(dlog, axis=1, keepdims=True))
    qb = q.astype(BF16)
    kb = k.astype(BF16)
    sw = lax.dot_general(qb, kb, (((1,), (1,)), ((), ())), preferred_element_type=F32) * jnp.exp(dlog - m_t)
    dec = jnp.exp(inter - m_t)
    num = (dec * jnp.dot(qb, c_sc[...].astype(BF16), preferred_element_type=F32)
           + jnp.dot(sw.astype(BF16), v, preferred_element_type=F32))
    den = dec * jnp.sum(q * n_sc[...], axis=1, keepdims=True) + jnp.sum(sw, axis=1, keepdims=True)
    h_t = num * (1.0 / jnp.maximum(jnp.abs(den), jnp.exp(-m_t)))

    a_col = b_last - b_col + ig_col
    m_new = jnp.maximum(b_last + m_prev, jnp.max(a_col, axis=0, keepdims=True))
    decay = jnp.exp(b_last + m_prev - m_new)
    kw = k * jnp.exp(a_col - m_new)
    c_sc[...] = decay * c_sc[...] + lax.dot_general(kw.astype(BF16), v, (((0,), (0,)), ((), ())),
                                                    preferred_element_type=F32)
    n_sc[...] = decay * n_sc[...] + jnp.sum(kw, axis=0, keepdims=True)
    m_sc[...] = m_new

    hm = _sigmoid(mo_ref[0].astype(F32)) * h_t
    o_ref[0] = _rms(hm, ng_ref[0]).astype(BF16)


def _mlstm(mq, mk, mv, mo, grow, gcol, conv_w, conv_b, norm_g):
    batch, seq, _ = mq.shape
    L = MLSTM_CHUNK
    kernel = functools.partial(_mlstm_kernel, chunk=L)
    slab = pl.BlockSpec((1, L, LANES), lambda b, h, c: (b, c, h))
    nchunk = seq // L
    return pl.pallas_call(
        kernel,
        out_shape=jax.ShapeDtypeStruct(mq.shape, BF16),
        grid=(batch, M_HEADS, nchunk),
        in_specs=[
            slab, slab, slab, slab,
            pl.BlockSpec((L, LANES), lambda b, h, c: (b * nchunk + c, 0)),
            pl.BlockSpec((1, 8, L), lambda b, h, c: (b, 0, c)),
            pl.BlockSpec((1, CONV_K, LANES), lambda b, h, c: (h, 0, 0)),
            pl.BlockSpec((1, CONV_K, LANES), lambda b, h, c: (M_HEADS + h, 0, 0)),
            pl.BlockSpec((1, 1, LANES), lambda b, h, c: (h, 0, 0)),
            pl.BlockSpec((1, 1, LANES), lambda b, h, c: (M_HEADS + h, 0, 0)),
            pl.BlockSpec((1, 1, LANES), lambda b, h, c: (h, 0, 0)),
        ],
        out_specs=slab,
        scratch_shapes=[pltpu.VMEM((L + 8, LANES), F32), pltpu.VMEM((L + 8, LANES), F32),
                        pltpu.VMEM((M_DH, M_DH), F32), pltpu.VMEM((1, M_DH), F32), pltpu.VMEM((1, 1), F32)],
        compiler_params=_params("parallel", "parallel", "arbitrary"),
        name="mlstm",
    )(mq, mk, mv, mo, grow, gcol, conv_w, conv_w, conv_b, conv_b, norm_g)


def _outproj_kernel(*refs, moe):
    if moe:
        a_ref, hm_ref, x_ref, mod_ref, ng_ref, wt_ref, wb_ref, rw_ref, x1_ref, h2_ref, gates_ref = refs
    else:
        a_ref, hm_ref, x_ref, mod_ref, ng_ref, wt_ref, wb_ref, x1_ref, h2_ref = refs
    y = (jnp.dot(a_ref[...], wt_ref[...], preferred_element_type=F32)
         + jnp.dot(hm_ref[...], wb_ref[...], preferred_element_type=F32))
    mod = mod_ref[0]
    ng = ng_ref[...]
    x1 = x_ref[...] + mod[2:3] * _rms(y, ng[1:2])
    x1_ref[...] = x1
    h2 = _rms(x1, ng[2:3]) * (1.0 + mod[4:5]) + mod[3:4]
    h2_ref[...] = h2.astype(BF16)
    if moe:
        logits = jnp.dot(h2, rw_ref[...], precision=HIGHEST, preferred_element_type=F32)
        lane = lax.broadcasted_iota(jnp.int32, logits.shape, 1)
        lg = jnp.where(lane < N_EXPERTS, logits, NEG)
        v1 = jnp.max(lg, axis=1, keepdims=True)
        i1 = jnp.min(jnp.where(lg == v1, lane, LANES), axis=1, keepdims=True)
        lg2 = jnp.where(lane == i1, NEG, lg)
        v2 = jnp.max(lg2, axis=1, keepdims=True)
        i2 = jnp.min(jnp.where(lg2 == v2, lane, LANES), axis=1, keepdims=True)
        e2 = jnp.exp(v2 - v1)
        w1 = 1.0 / (1.0 + e2)
        gates_ref[...] = jnp.where(lane == i1, w1, 0.0) + jnp.where(lane == i2, e2 * w1, 0.0)


def _outproj(a_out, h_m, xf, mod, ng, w_top, w_bot, router_w, batch, seq):
    n, d = xf.shape
    tm = ROW_TILE
    tpb = seq // tm
    moe = router_w is not None
    half = pl.BlockSpec((tm, HEAD_GROUP_W), lambda i: (i, 0))
    full = pl.BlockSpec((tm, d), lambda i: (i, 0))
    in_specs = [half, half, full,
                pl.BlockSpec((1, 6, d), lambda i: (i // tpb, 0, 0)),
                pl.BlockSpec(ng.shape, lambda i: (0, 0)),
                pl.BlockSpec(w_top.shape, lambda i: (0, 0)),
                pl.BlockSpec(w_bot.shape, lambda i: (0, 0))]
    args = [a_out, h_m, xf, mod, ng, w_top, w_bot]
    out_shape = [jax.ShapeDtypeStruct((n, d), F32), jax.ShapeDtypeStruct((n, d), BF16)]
    out_specs = [full, full]
    if moe:
        in_specs.append(pl.BlockSpec(router_w.shape, lambda i: (0, 0)))
        args.append(router_w)
        out_shape.append(jax.ShapeDtypeStruct((n, LANES), F32))
        out_specs.append(pl.BlockSpec((tm, LANES), lambda i: (i, 0)))
    return pl.pallas_call(
        functools.partial(_outproj_kernel, moe=moe),
        out_shape=out_shape,
        grid=(n // tm,),
        in_specs=in_specs,
        out_specs=out_specs,
        compiler_params=_params("parallel"),
        name="out_proj_moe" if moe else "out_proj",
    )(*args)


def _ffn_kernel(*refs, moe, n_col, n_exp):
    if moe:
        h_ref, x_ref, mod_ref, ng_ref, gates_ref, w1_ref, w3_ref, w2_ref, o_ref, acc_sc = refs
        e = pl.program_id(1)
        f = pl.program_id(2)
        first = jnp.logical_and(e == 0, f == 0)
        last = jnp.logical_and(e == n_exp - 1, f == n_col - 1)
    else:
        h_ref, x_ref, mod_ref, ng_ref, w1_ref, w3_ref, w2_ref, o_ref, acc_sc = refs
        f = pl.program_id(1)
        first = f == 0
        last = f == n_col - 1

    @pl.when(first)
    def _():
        acc_sc[...] = jnp.zeros(acc_sc.shape, F32)

    hb = h_ref[...]
    a = jnp.dot(hb, w1_ref[0], preferred_element_type=F32)
    b = jnp.dot(hb, w3_ref[0], preferred_element_type=F32)
    hh = a * _sigmoid(a) * b
    if moe:
        hh = hh * _pick_lane(gates_ref[...], e)
    acc_sc[...] += jnp.dot(hh.astype(BF16), w2_ref[0], preferred_element_type=F32)

    @pl.when(last)
    def _():
        mod = mod_ref[0]
        o_ref[...] = x_ref[...] + mod[5:6] * _rms(acc_sc[...], ng_ref[3:4, :])


def _ffn(h2, x1, mod, ng, gates, w1, w3, w2, batch, seq):
    n, d = x1.shape
    n_exp, _, dff = w1.shape
    tm, tf = FFN_ROW_TILE, FFN_COL_TILE
    tpb = seq // tm
    n_col = dff // tf
    moe = gates is not None
    if moe:
        grid = (n // tm, n_exp, n_col)
        row = lambda i, e, f: (i, 0)
        modm = lambda i, e, f: (i // tpb, 0, 0)
        const = lambda i, e, f: (0, 0)
        wup = lambda i, e, f: (e, 0, f)
        wdn = lambda i, e, f: (e, f, 0)
        sem = ("parallel", "arbitrary", "arbitrary")
    else:
        grid = (n // tm, n_col)
        row = lambda i, f: (i, 0)
        modm = lambda i, f: (i // tpb, 0, 0)
        const = lambda i, f: (0, 0)
        wup = lambda i, f: (0, 0, f)
        wdn = lambda i, f: (0, f, 0)
        sem = ("parallel", "arbitrary")
    in_specs = [pl.BlockSpec((tm, d), row), pl.BlockSpec((tm, d), row),
                pl.BlockSpec((1, 6, d), modm), pl.BlockSpec(ng.shape, const)]
    args = [h2, x1, mod, ng]
    if moe:
        in_specs.append(pl.BlockSpec((tm, LANES), row))
        args.append(gates)
    in_specs += [pl.BlockSpec((1, d, tf), wup), pl.BlockSpec((1, d, tf), wup), pl.BlockSpec((1, tf, d), wdn)]
    args += [w1, w3, w2]
    return pl.pallas_call(
        functools.partial(_ffn_kernel, moe=moe, n_col=n_col, n_exp=n_exp),
        out_shape=jax.ShapeDtypeStruct((n, d), F32),
        grid=grid,
        in_specs=in_specs,
        out_specs=pl.BlockSpec((tm, d), row),
        scratch_shapes=[pltpu.VMEM((tm, d), F32)],
        compiler_params=_params(*sem),
        name="moe_ffn" if moe else "dense_ffn",
    )(*args)


@jax.jit
def _forward(x, c, rel_bias, ada_w, ada_b, norm_g, w_in, gate_b, conv_w, conv_b, lam_vec,
             attn_norm_g, mlstm_norm_g, w_out, ffn_w1, ffn_w3, ffn_w2, router_w, moe_w1, moe_w3, moe_w2):
    batch, seq, d = x.shape
    depth = ada_w.shape[0]
    n = batch * seq
    n_main = N_SLABS * HEAD_GROUP_W
    assert seq % ATTN_BLOCK == 0 and seq % FFN_ROW_TILE == 0 and seq % MLSTM_CHUNK == 0
    assert seq >= 2 * ATTN_BLOCK and ROW_TILE == ATTN_BLOCK
    assert w_in.shape[2] == n_main + 2 * M_HEADS

    mod_all = _ada(c, ada_w, ada_b)
    bias_tiles = _bias_tables(rel_bias, seq)

    xf = x.reshape(n, d)
    for i in range(depth):
        mod = mod_all[i]
        ng = norm_g[i]
        lam_init = 0.8 - 0.6 * math.exp(-0.3 * i)

        w_gate = w_in[i][:, n_main:]
        w_bf = w_in[i][:, :n_main].astype(BF16)
        v0, v1 = 2 * HEAD_GROUP_W, 3 * HEAD_GROUP_W
        slabs = _inproj(
            xf, mod, ng[0:1], jnp.concatenate([w_bf[:, :v0], w_bf[:, v1:]], axis=1), w_bf[:, v0:v1].T,
            jnp.zeros((d, LANES), BF16).at[:, :2 * M_HEADS].set(w_gate.astype(BF16)),
            w_gate.T.astype(BF16),
            jnp.zeros((1, LANES), F32).at[0, :2 * M_HEADS].set(gate_b[i].reshape(-1)),
            gate_b[i].reshape(2 * M_HEADS, 1), batch, seq)
        aq, ak, mq, mk, mv, mo = (s.reshape(batch, seq, HEAD_GROUP_W) for s in slabs[:N_SLABS - 1])
        avt, grow, gcol = slabs[N_SLABS - 1:]

        scal = jnp.full((1,), lam_init, F32)
        a_out = _attention(aq, ak, avt, scal, bias_tiles, lam_vec[i], attn_norm_g[i].reshape(LANES, 1))

        h_m = _mlstm(mq, mk, mv, mo, grow, gcol,
                     conv_w[i].reshape(CONV_K, 2 * M_HEADS, LANES).transpose(1, 0, 2),
                     conv_b[i].reshape(2 * M_HEADS, 1, LANES),
                     mlstm_norm_g[i].reshape(M_HEADS, 1, LANES))

        w_o = w_out[i].astype(BF16)
        j = i // 2
        if i % 2 == 0:
            x1, h2 = _outproj(a_out.reshape(n, -1), h_m.reshape(n, -1), xf, mod, ng,
                              w_o[:HEAD_GROUP_W], w_o[HEAD_GROUP_W:], None, batch, seq)
            xf = _ffn(h2, x1, mod, ng, None, ffn_w1[j][None].astype(BF16), ffn_w3[j][None].astype(BF16),
                      ffn_w2[j][None].astype(BF16), batch, seq)
        else:
            rw = jnp.zeros((d, LANES), F32).at[:, :N_EXPERTS].set(router_w[j])
            x1, h2, gates = _outproj(a_out.reshape(n, -1), h_m.reshape(n, -1), xf, mod, ng,
                                     w_o[:HEAD_GROUP_W], w_o[HEAD_GROUP_W:], rw, batch, seq)
            xf = _ffn(h2, x1, mod, ng, gates, moe_w1[j].astype(BF16), moe_w3[j].astype(BF16),
                      moe_w2[j].astype(BF16), batch, seq)
    return xf.reshape(batch, seq, d)


def kernel(x, c, rel_bias, ada_w, ada_b, norm_g, w_in, gate_b, conv_w, conv_b, lam_vec, attn_norm_g,
           mlstm_norm_g, w_out, ffn_w1, ffn_w3, ffn_w2, router_w, moe_w1, moe_w3, moe_w2):
    return _forward(x, c, rel_bias, ada_w, ada_b, norm_g, w_in, gate_b, conv_w, conv_b, lam_vec,
                    attn_norm_g, mlstm_norm_g, w_out, ffn_w1, ffn_w3, ffn_w2, router_w, moe_w1, moe_w3, moe_w2)
```

```python
import functools
import math

import numpy as np
import jax
import jax.numpy as jnp
from jax import lax
from jax.experimental import pallas as pl
from jax.experimental.pallas import tpu as pltpu

F32 = jnp.float32
BF16 = jnp.bfloat16
HIGHEST = lax.Precision.HIGHEST

A_HEADS = 4
A_DH = 64
M_HEADS = 4
M_DH = 128
CONV_K = 4
N_BUCKETS = 32
MAX_DIST = 128
N_EXPERTS = 8
EPS = 1e-6

LANES = 128
HEAD_GROUP_W = 512
N_SLABS = 7
NEG = -1e30
VMEM_LIMIT_BYTES = 56 * 1024 * 1024

ATTN_BLOCK = 512
MLSTM_CHUNK = 256
ROW_TILE = 512
FFN_ROW_TILE = 1024
FFN_COL_TILE = 512
ADA_COL_TILE = 1536
MOE_BLOCK = 256


def _params(*sem):
    return pltpu.CompilerParams(dimension_semantics=sem, vmem_limit_bytes=VMEM_LIMIT_BYTES)


def _rms(x, g):
    return x * lax.rsqrt(jnp.mean(x * x, axis=-1, keepdims=True) + EPS) * g


def _sigmoid(x):
    return 1.0 / (1.0 + jnp.exp(-x))


def _log_sigmoid(x):
    return jnp.minimum(x, 0.0) - jnp.log(1.0 + jnp.exp(-jnp.abs(x)))


def _pick_lane(x, idx):
    lane = lax.broadcasted_iota(jnp.int32, x.shape, 1)
    return jnp.sum(jnp.where(lane == idx, x, 0.0), axis=1, keepdims=True)


def _pick_row(x, idx):
    row = lax.broadcasted_iota(jnp.int32, x.shape, 0)
    return jnp.sum(jnp.where(row == idx, x, 0.0), axis=0, keepdims=True)


def _ada_kernel(c_ref, w_ref, b_ref, o_ref):
    c = c_ref[...]
    ca = c * _sigmoid(c)
    o_ref[0] = jnp.dot(ca, w_ref[0], precision=HIGHEST, preferred_element_type=F32) + b_ref[0]


def _ada(c, ada_w, ada_b):
    depth, d, d6 = ada_w.shape
    b = c.shape[0]
    rows = 8
    cp = jnp.zeros((rows, d), F32).at[:b].set(c)
    out = pl.pallas_call(
        _ada_kernel,
        out_shape=jax.ShapeDtypeStruct((depth, rows, d6), F32),
        grid=(depth, d6 // ADA_COL_TILE),
        in_specs=[
            pl.BlockSpec((rows, d), lambda l, j: (0, 0)),
            pl.BlockSpec((1, d, ADA_COL_TILE), lambda l, j: (l, 0, j)),
            pl.BlockSpec((1, 1, ADA_COL_TILE), lambda l, j: (l, 0, j)),
        ],
        out_specs=pl.BlockSpec((1, rows, ADA_COL_TILE), lambda l, j: (l, 0, j)),
        compiler_params=_params("parallel", "parallel"),
        name="ada_mod",
    )(cp, ada_w, ada_b.reshape(depth, 1, d6))
    return out[:, :b].reshape(depth, b, 6, d)


def _inproj_kernel(x_ref, mod_ref, g_ref, w_ref, wvt_ref, wg_ref, wgt_ref, gb_ref, gbt_ref,
                   aq, ak, mq, mk, mv, mo, avt, grow, gcol):
    mod = mod_ref[0]
    h = _rms(x_ref[...], g_ref[...]) * (1.0 + mod[1:2]) + mod[0:1]
    hb = h.astype(BF16)
    for s, o in enumerate((aq, ak, mq, mk, mv, mo)):
        w = w_ref[:, s * HEAD_GROUP_W:(s + 1) * HEAD_GROUP_W]
        o[...] = jnp.dot(hb, w, preferred_element_type=F32).astype(BF16)
    avt[0, 0] = lax.dot_general(wvt_ref[...], hb, (((1,), (1,)), ((), ())),
                                preferred_element_type=F32).astype(BF16)
    grow[...] = jnp.dot(hb, wg_ref[...], preferred_element_type=F32) + gb_ref[...]
    gcol[0] = lax.dot_general(wgt_ref[...], hb, (((1,), (1,)), ((), ())),
                              preferred_element_type=F32) + gbt_ref[...]


def _inproj(xf, mod, g, w_main, w_vt, wg, wgt, gb, gbt, batch, seq):
    n, d = xf.shape
    tm = ROW_TILE
    tpb = seq // tm
    n_row_slabs = N_SLABS - 1
    slab = jax.ShapeDtypeStruct((n, HEAD_GROUP_W), BF16)
    row_spec = pl.BlockSpec((tm, HEAD_GROUP_W), lambda i: (i, 0))
    return pl.pallas_call(
        _inproj_kernel,
        out_shape=[slab] * n_row_slabs + [jax.ShapeDtypeStruct((batch, tpb, HEAD_GROUP_W, tm), BF16),
                                          jax.ShapeDtypeStruct((n, LANES), F32),
                                          jax.ShapeDtypeStruct((batch, 8, seq), F32)],
        grid=(n // tm,),
        in_specs=[
            pl.BlockSpec((tm, d), lambda i: (i, 0)),
            pl.BlockSpec((1, 6, d), lambda i: (i // tpb, 0, 0)),
            pl.BlockSpec((1, d), lambda i: (0, 0)),
            pl.BlockSpec(w_main.shape, lambda i: (0, 0)),
            pl.BlockSpec(w_vt.shape, lambda i: (0, 0)),
            pl.BlockSpec(wg.shape, lambda i: (0, 0)),
            pl.BlockSpec(wgt.shape, lambda i: (0, 0)),
            pl.BlockSpec(gb.shape, lambda i: (0, 0)),
            pl.BlockSpec(gbt.shape, lambda i: (0, 0)),
        ],
        out_specs=[row_spec] * n_row_slabs + [
            pl.BlockSpec((1, 1, HEAD_GROUP_W, tm), lambda i: (i // tpb, i % tpb, 0, 0)),
            pl.BlockSpec((tm, LANES), lambda i: (i, 0)),
            pl.BlockSpec((1, 8, tm), lambda i: (i // tpb, 0, i % tpb))],
        compiler_params=_params("parallel"),
        name="in_proj",
    )(xf, mod, g, w_main, w_vt, wg, wgt, gb, gbt)


ACC_ROWS = LANES + 16


def _attn_kernel(scal_ref, q_ref, k_ref, vt_ref, bias_ref, lv_ref, g_ref, o_ref, m_sc, acc_sc,
                 sa_sc, sb_sc, mxa_sc, mxb_sc, *, blk):
    qi = pl.program_id(2)
    lam_init = scal_ref[0]

    lane = lax.broadcasted_iota(jnp.int32, (blk, LANES), 1)
    qs = q_ref[0] * jnp.asarray(A_DH ** -0.5, BF16)
    zero = jnp.zeros_like(qs)
    qz = (jnp.where(lane < A_DH, qs, zero), jnp.where(lane >= A_DH, qs, zero))
    ones_rows = jnp.ones((ACC_ROWS - LANES, blk), BF16)

    m_sc[...] = jnp.full(m_sc.shape, NEG, F32)
    acc_sc[...] = jnp.zeros(acc_sc.shape, F32)

    def scores(kj, s_sc, mx_sc):
        start = pl.multiple_of(kj * blk, blk)
        kt = k_ref[0, pl.ds(start, blk), :]
        bias = bias_ref[0, jnp.minimum(qi - kj, 2)]
        for c in range(2):
            s = lax.dot_general(kt, qz[c], (((1,), (1,)), ((), ())), preferred_element_type=F32) + bias
            s_sc[c] = s
            mx_sc[c] = jnp.max(s, axis=0, keepdims=True)

    def accumulate(kj, s_sc, mx_sc):
        vext = jnp.concatenate([vt_ref[0, kj], ones_rows], axis=0)
        for c in range(2):
            m_old = m_sc[c]
            m_new = jnp.maximum(m_old, mx_sc[c])
            alpha = jnp.exp(m_old - m_new)
            p = jnp.exp(s_sc[c] - m_new).astype(BF16)
            pv = jnp.dot(vext, p, preferred_element_type=F32)
            acc_sc[c] = alpha * acc_sc[c] + pv
            m_sc[c] = m_new

    scores(0, sa_sc, mxa_sc)

    def pair_body(u, carry):
        scores(2 * u + 1, sb_sc, mxb_sc)
        accumulate(2 * u, sa_sc, mxa_sc)
        scores(2 * u + 2, sa_sc, mxa_sc)
        accumulate(2 * u + 1, sb_sc, mxb_sc)
        return carry

    lax.fori_loop(0, qi // 2, pair_body, 0)

    @pl.when(qi % 2 == 1)
    def _():
        scores(qi, sb_sc, mxb_sc)
        accumulate(qi - 1, sa_sc, mxa_sc)
        accumulate(qi, sb_sc, mxb_sc)

    @pl.when(qi % 2 == 0)
    def _():
        accumulate(qi, sa_sc, mxa_sc)

    acc0 = acc_sc[0]
    acc1 = acc_sc[1]
    o0 = acc0[:LANES] * (1.0 / acc0[LANES:LANES + 1])
    o1 = acc1[:LANES] * (1.0 / acc1[LANES:LANES + 1])
    lv = lv_ref[...]
    lam = (jnp.exp(jnp.sum(lv[0:1] * lv[1:2], axis=1, keepdims=True))
           - jnp.exp(jnp.sum(lv[2:3] * lv[3:4], axis=1, keepdims=True)) + lam_init)
    a = o0 - lam * o1
    y = a * lax.rsqrt(jnp.mean(a * a, axis=0, keepdims=True) + EPS) * g_ref[...] * (1.0 - lam_init)
    o_ref[0] = y.T.astype(BF16)


def _attention(aq, ak, avt, scal, bias_tiles, lam_vec, norm_g):
    batch, seq, _ = aq.shape
    blk = ATTN_BLOCK
    nk = seq // blk
    kernel = functools.partial(_attn_kernel, blk=blk)
    grid_spec = pltpu.PrefetchScalarGridSpec(
        num_scalar_prefetch=1,
        grid=(batch, A_HEADS, nk),
        in_specs=[
            pl.BlockSpec((1, blk, LANES), lambda b, h, i, s: (b, i, h)),
            pl.BlockSpec((1, seq, LANES), lambda b, h, i, s: (b, 0, h)),
            pl.BlockSpec((1, nk, LANES, blk), lambda b, h, i, s: (b, 0, h, 0)),
            pl.BlockSpec((1, 3, blk, blk), lambda b, h, i, s: (h, 0, 0, 0)),
            pl.BlockSpec(lam_vec.shape, lambda b, h, i, s: (0, 0)),
            pl.BlockSpec(norm_g.shape, lambda b, h, i, s: (0, 0)),
        ],
        out_specs=pl.BlockSpec((1, blk, LANES), lambda b, h, i, s: (b, i, h)),
        scratch_shapes=[pltpu.VMEM((2, 1, blk), F32), pltpu.VMEM((2, ACC_ROWS, blk), F32),
                        pltpu.VMEM((2, blk, blk), F32), pltpu.VMEM((2, blk, blk), F32),
                        pltpu.VMEM((2, 1, blk), F32), pltpu.VMEM((2, 1, blk), F32)],
    )
    return pl.pallas_call(
        kernel,
        out_shape=jax.ShapeDtypeStruct(aq.shape, BF16),
        grid_spec=grid_spec,
        compiler_params=_params("parallel", "parallel", "arbitrary"),
        name="diff_attn",
    )(scal, aq, ak, avt, bias_tiles, lam_vec, norm_g)


def _t5_bucket(n):
    max_exact = N_BUCKETS // 2
    nf = jnp.maximum(n, 1).astype(F32)
    large = max_exact + (jnp.log(nf / max_exact) / math.log(MAX_DIST / max_exact)
                         * (N_BUCKETS - max_exact)).astype(jnp.int32)
    large = jnp.minimum(large, N_BUCKETS - 1)
    return jnp.where(n < max_exact, n, large)


def _bias_tables(rel_bias, seq):
    blk = ATTN_BLOCK
    far = np.arange(blk + 1, max(seq, blk + 2), dtype=np.float64)
    sat = (N_BUCKETS // 2) + np.log(far / (N_BUCKETS // 2)) / math.log(MAX_DIST / (N_BUCKETS // 2)) * (N_BUCKETS // 2)
    assert np.all(sat >= N_BUCKETS - 0.5), "far tiles need a saturated distance bucket"
    dist_bias = rel_bias[_t5_bucket(jnp.arange(seq, dtype=jnp.int32))].T.astype(F32)
    heads = dist_bias.shape[0]
    last = dist_bias[:, seq - 1:seq]
    padded = jnp.concatenate([jnp.full((heads, blk), NEG, F32), dist_bias,
                              jnp.broadcast_to(last, (heads, blk))], axis=1)
    tiles = []
    for d in (0, 1):
        base = blk + d * blk
        w = jnp.concatenate([padded[:, base:base + blk + 1], padded[:, base - blk + 1:base]], axis=1)
        t = jnp.tile(w, (1, blk))[:, :blk * (2 * blk - 1)].reshape(heads, blk, 2 * blk - 1)[:, :, :blk]
        tiles.append(t)
    tiles.append(jnp.broadcast_to(last[:, :, None], (heads, blk, blk)))
    return jnp.stack(tiles, axis=1)


def _mlstm_kernel(mq_ref, mk_ref, mv_ref, mo_ref, grow_ref, gcol_ref, cwq_ref, cwk_ref, cbq_ref, cbk_ref,
                  ng_ref, o_ref, xq_sc, xk_sc, c_sc, n_sc, m_sc, *, chunk):
    L = chunk
    h = pl.program_id(1)
    ci = pl.program_id(2)

    @pl.when(ci == 0)
    def _():
        xq_sc[0:8, :] = jnp.zeros((8, LANES), F32)
        xk_sc[0:8, :] = jnp.zeros((8, LANES), F32)
        c_sc[...] = jnp.zeros(c_sc.shape, F32)
        n_sc[...] = jnp.zeros(n_sc.shape, F32)
        m_sc[...] = jnp.zeros(m_sc.shape, F32)

    def conv_silu(x_ref, sc, w_ref, b_ref):
        sc[8:8 + L, :] = x_ref[0].astype(F32)
        w = w_ref[0]
        y = b_ref[0]
        for j in range(CONV_K):
            y = y + sc[5 + j:5 + j + L, :] * w[j:j + 1]
        sc[0:8, :] = sc[L:L + 8, :]
        return y * _sigmoid(y)

    q = conv_silu(mq_ref, xq_sc, cwq_ref, cbq_ref)
    k = conv_silu(mk_ref, xk_sc, cwk_ref, cbk_ref) * (M_DH ** -0.5)
    v = mv_ref[0]

    r = lax.broadcasted_iota(jnp.int32, (L, L), 0)
    cidx = lax.broadcasted_iota(jnp.int32, (L, L), 1)
    causal = r >= cidx
    lower = causal.astype(F32)
    upper = (r <= cidx).astype(F32)

    g_rows = grow_ref[...]
    cum_rows = jnp.dot(lower, _log_sigmoid(g_rows), precision=HIGHEST, preferred_element_type=F32)
    ig_col = _pick_lane(g_rows, h)
    b_col = _pick_lane(cum_rows, M_HEADS + h)
    g_cols = gcol_ref[0]
    cum_cols = jnp.dot(_log_sigmoid(g_cols), upper, precision=HIGHEST, preferred_element_type=F32)
    ig_row = _pick_row(g_cols, h)
    b_row = _pick_row(cum_cols, M_HEADS + h)
    b_last = b_col[L - 1:L, :]
    m_prev = m_sc[...]

    dlog = jnp.where(causal, b_col - b_row + ig_row, NEG)
    inter = b_col + m_prev
    m_t = jnp.maximum(inter, jnp.max(dlog, axis=1, keepdims=True))
    qb = q.astype(BF16)
    kb = k.astype(BF16)
    sw = lax.dot_general(qb, kb, (((1,), (1,)), ((), ())), preferred_element_type=F32) * jnp.exp(dlog - m_t)
    dec = jnp.exp(inter - m_t)
    num = (dec * jnp.dot(qb, c_sc[...].astype(BF16), preferred_element_type=F32)
           + jnp.dot(sw.astype(BF16), v, preferred_element_type=F32))
    den = dec * jnp.sum(q * n_sc[...], axis=1, keepdims=True) + jnp.sum(sw, axis=1, keepdims=True)
    h_t = num * (1.0 / jnp.maximum(jnp.abs(den), jnp.exp(-m_t)))

    a_col = b_last - b_col + ig_col
    m_new = jnp.maximum(b_last + m_prev, jnp.max(a_col, axis=0, keepdims=True))
    decay = jnp.exp(b_last + m_prev - m_new)
    kw = k * jnp.exp(a_col - m_new)
    c_sc[...] = decay * c_sc[...] + lax.dot_general(kw.astype(BF16), v, (((0,), (0,)), ((), ())),
                                                    preferred_element_type=F32)
    n_sc[...] = decay * n_sc[...] + jnp.sum(kw, axis=0, keepdims=True)
    m_sc[...] = m_new

    hm = _sigmoid(mo_ref[0].astype(F32)) * h_t
    o_ref[0] = _rms(hm, ng_ref[0]).astype(BF16)


def _mlstm(mq, mk, mv, mo, grow, gcol, conv_w, conv_b, norm_g):
    batch, seq, _ = mq.shape
    L = MLSTM_CHUNK
    kernel = functools.partial(_mlstm_kernel, chunk=L)
    slab = pl.BlockSpec((1, L, LANES), lambda b, h, c: (b, c, h))
    nchunk = seq // L
    return pl.pallas_call(
        kernel,
        out_shape=jax.ShapeDtypeStruct(mq.shape, BF16),
        grid=(batch, M_HEADS, nchunk),
        in_specs=[
            slab, slab, slab, slab,
            pl.BlockSpec((L, LANES), lambda b, h, c: (b * nchunk + c, 0)),
            pl.BlockSpec((1, 8, L), lambda b, h, c: (b, 0, c)),
            pl.BlockSpec((1, CONV_K, LANES), lambda b, h, c: (h, 0, 0)),
            pl.BlockSpec((1, CONV_K, LANES), lambda b, h, c: (M_HEADS + h, 0, 0)),
            pl.BlockSpec((1, 1, LANES), lambda b, h, c: (h, 0, 0)),
            pl.BlockSpec((1, 1, LANES), lambda b, h, c: (M_HEADS + h, 0, 0)),
            pl.BlockSpec((1, 1, LANES), lambda b, h, c: (h, 0, 0)),
        ],
        out_specs=slab,
        scratch_shapes=[pltpu.VMEM((L + 8, LANES), F32), pltpu.VMEM((L + 8, LANES), F32),
                        pltpu.VMEM((M_DH, M_DH), F32), pltpu.VMEM((1, M_DH), F32), pltpu.VMEM((1, 1), F32)],
        compiler_params=_params("parallel", "parallel", "arbitrary"),
        name="mlstm",
    )(mq, mk, mv, mo, grow, gcol, conv_w, conv_w, conv_b, conv_b, norm_g)


def _outproj_kernel(*refs, moe):
    if moe:
        (a_ref, hm_ref, x_ref, mod_ref, ng_ref, wt_ref, wb_ref, rw_ref,
         x1_ref, h2_ref, gates_ref, pos_ref, post_ref, cnt_ref) = refs
    else:
        a_ref, hm_ref, x_ref, mod_ref, ng_ref, wt_ref, wb_ref, x1_ref, h2_ref = refs
    y = (jnp.dot(a_ref[...], wt_ref[...], preferred_element_type=F32)
         + jnp.dot(hm_ref[...], wb_ref[...], preferred_element_type=F32))
    mod = mod_ref[0]
    ng = ng_ref[...]
    x1 = x_ref[...] + mod[2:3] * _rms(y, ng[1:2])
    x1_ref[...] = x1
    h2 = _rms(x1, ng[2:3]) * (1.0 + mod[4:5]) + mod[3:4]
    h2_ref[...] = h2.astype(BF16)
    if moe:
        logits = jnp.dot(h2, rw_ref[...], precision=HIGHEST, preferred_element_type=F32)
        lane = lax.broadcasted_iota(jnp.int32, logits.shape, 1)
        lg = jnp.where(lane < N_EXPERTS, logits, NEG)
        v1 = jnp.max(lg, axis=1, keepdims=True)
        i1 = jnp.min(jnp.where(lg == v1, lane, LANES), axis=1, keepdims=True)
        lg2 = jnp.where(lane == i1, NEG, lg)
        v2 = jnp.max(lg2, axis=1, keepdims=True)
        i2 = jnp.min(jnp.where(lg2 == v2, lane, LANES), axis=1, keepdims=True)
        e2 = jnp.exp(v2 - v1)
        w1 = 1.0 / (1.0 + e2)
        gates_ref[...] = jnp.where(lane == i1, w1, 0.0) + jnp.where(lane == i2, e2 * w1, 0.0)
        sel = jnp.logical_or(lane == i1, lane == i2)
        t = MOE_BLOCK
        r = lax.broadcasted_iota(jnp.int32, (t, t), 0)
        cidx = lax.broadcasted_iota(jnp.int32, (t, t), 1)
        lower = (r >= cidx).astype(BF16)
        for blk in range(logits.shape[0] // t):
            sel_b = sel[blk * t:(blk + 1) * t]
            rank = jnp.dot(lower, sel_b.astype(BF16), preferred_element_type=F32)
            pos = jnp.where(sel_b, rank, 0.0)
            pos_ref[blk * t:(blk + 1) * t, :] = pos
            post_ref[blk] = pos.T[:N_EXPERTS]
            cnt_ref[blk] = rank[t - 1:t, :]


def _outproj(a_out, h_m, xf, mod, ng, w_top, w_bot, router_w, batch, seq):
    n, d = xf.shape
    tm = ROW_TILE
    tpb = seq // tm
    moe = router_w is not None
    half = pl.BlockSpec((tm, HEAD_GROUP_W), lambda i: (i, 0))
    full = pl.BlockSpec((tm, d), lambda i: (i, 0))
    in_specs = [half, half, full,
                pl.BlockSpec((1, 6, d), lambda i: (i // tpb, 0, 0)),
                pl.BlockSpec(ng.shape, lambda i: (0, 0)),
                pl.BlockSpec(w_top.shape, lambda i: (0, 0)),
                pl.BlockSpec(w_bot.shape, lambda i: (0, 0))]
    args = [a_out, h_m, xf, mod, ng, w_top, w_bot]
    out_shape = [jax.ShapeDtypeStruct((n, d), F32), jax.ShapeDtypeStruct((n, d), BF16)]
    out_specs = [full, full]
    if moe:
        in_specs.append(pl.BlockSpec(router_w.shape, lambda i: (0, 0)))
        args.append(router_w)
        bpt = tm // MOE_BLOCK
        out_shape += [jax.ShapeDtypeStruct((n, LANES), F32), jax.ShapeDtypeStruct((n, LANES), F32),
                      jax.ShapeDtypeStruct((n // MOE_BLOCK, N_EXPERTS, MOE_BLOCK), F32),
                      jax.ShapeDtypeStruct((n // MOE_BLOCK, 1, LANES), F32)]
        out_specs += [pl.BlockSpec((tm, LANES), lambda i: (i, 0)), pl.BlockSpec((tm, LANES), lambda i: (i, 0)),
                      pl.BlockSpec((bpt, N_EXPERTS, MOE_BLOCK), lambda i: (i, 0, 0)),
                      pl.BlockSpec((bpt, 1, LANES), lambda i: (i, 0, 0))]
    return pl.pallas_call(
        functools.partial(_outproj_kernel, moe=moe),
        out_shape=out_shape,
        grid=(n // tm,),
        in_specs=in_specs,
        out_specs=out_specs,
        compiler_params=_params("parallel"),
        name="out_proj_moe" if moe else "out_proj",
    )(*args)


def _ffn_kernel(h_ref, x_ref, mod_ref, ng_ref, w1_ref, w3_ref, w2_ref, o_ref, acc_sc, *, n_col):
    f = pl.program_id(1)

    @pl.when(f == 0)
    def _():
        acc_sc[...] = jnp.zeros(acc_sc.shape, F32)

    hb = h_ref[...]
    a = jnp.dot(hb, w1_ref[...], preferred_element_type=F32)
    b = jnp.dot(hb, w3_ref[...], preferred_element_type=F32)
    hh = a * _sigmoid(a) * b
    acc_sc[...] += jnp.dot(hh.astype(BF16), w2_ref[...], preferred_element_type=F32)

    @pl.when(f == n_col - 1)
    def _():
        mod = mod_ref[0]
        o_ref[...] = x_ref[...] + mod[5:6] * _rms(acc_sc[...], ng_ref[3:4, :])


def _ffn(h2, x1, mod, ng, w1, w3, w2, batch, seq):
    n, d = x1.shape
    dff = w1.shape[1]
    tm, tf = FFN_ROW_TILE, FFN_COL_TILE
    tpb = seq // tm
    n_col = dff // tf
    row = lambda i, f: (i, 0)
    return pl.pallas_call(
        functools.partial(_ffn_kernel, n_col=n_col),
        out_shape=jax.ShapeDtypeStruct((n, d), F32),
        grid=(n // tm, n_col),
        in_specs=[pl.BlockSpec((tm, d), row), pl.BlockSpec((tm, d), row),
                  pl.BlockSpec((1, 6, d), lambda i, f: (i // tpb, 0, 0)),
                  pl.BlockSpec(ng.shape, lambda i, f: (0, 0)),
                  pl.BlockSpec((d, tf), lambda i, f: (0, f)), pl.BlockSpec((d, tf), lambda i, f: (0, f)),
                  pl.BlockSpec((tf, d), lambda i, f: (f, 0))],
        out_specs=pl.BlockSpec((tm, d), row),
        scratch_shapes=[pltpu.VMEM((tm, d), F32)],
        compiler_params=_params("parallel", "arbitrary"),
        name="dense_ffn",
    )(h2, x1, mod, ng, w1, w3, w2)


def _moe_tables(cnt, batch, seq):
    t = MOE_BLOCK
    e_n = N_EXPERTS
    nbb = seq // t
    ntb = 2 * nbb + e_n
    c = cnt.reshape(batch, nbb, e_n)
    cum = jnp.cumsum(c, axis=1) - c
    tot = jnp.sum(c, axis=1)
    ntile = (tot + t - 1) // t
    tile_end = jnp.cumsum(ntile, axis=1)
    tile_base = tile_end - ntile
    tl = jnp.arange(ntb, dtype=jnp.int32)
    e_t = jnp.sum((tl[None, :, None] >= tile_end[:, None, :]).astype(jnp.int32), axis=2)
    valid = e_t < e_n
    e_c = jnp.minimum(e_t, e_n - 1)
    s0 = (tl[None, :] - jnp.take_along_axis(tile_base, e_c, axis=1)) * t
    bidx = jnp.arange(batch)[:, None]
    cum_sel = cum.transpose(0, 2, 1)[bidx, e_c]
    end_sel = cum_sel + c.transpose(0, 2, 1)[bidx, e_c]
    lo = jnp.sum((end_sel <= s0[..., None]).astype(jnp.int32), axis=2)
    hi = jnp.sum((cum_sel < (s0 + t)[..., None]).astype(jnp.int32), axis=2)

    def flat(a, tail):
        return jnp.concatenate([a.reshape(-1).astype(jnp.int32), jnp.full((1,), tail, jnp.int32)])

    tile_tables = (flat(jnp.broadcast_to(bidx, (batch, ntb)), batch - 1), flat(e_c, e_n - 1), flat(valid, 0),
                   flat(lo, 0), flat(hi, 0), flat(s0, 0), cum.reshape(-1).astype(jnp.int32))
    row0 = (bidx[:, :, None] * ntb + tile_base[:, None, :]) * t + cum
    seg_tables = ((row0 // t).reshape(-1).astype(jnp.int32), (row0 % t).reshape(-1).astype(jnp.int32))
    return tile_tables, seg_tables


def _moe_expert_kernel(tb_ref, te_ref, tv_ref, lo_ref, hi_ref, s0_ref, cum_ref,
                       h_ref, post_ref, w1_ref, w3_ref, w2_ref, o_ref, xg_sc, *, n_col, tf, nbb):
    ti = pl.program_id(0)
    t = MOE_BLOCK

    @pl.when(tv_ref[ti] == 0)
    def _():
        o_ref[...] = jnp.zeros(o_ref.shape, o_ref.dtype)

    @pl.when(tv_ref[ti] == 1)
    def _():
        b = tb_ref[ti]
        e = te_ref[ti]
        s0 = s0_ref[ti]
        xg_sc[...] = jnp.zeros(xg_sc.shape, F32)
        slot = lax.broadcasted_iota(jnp.int32, (t, t), 0).astype(F32)

        def gather(j, carry):
            blk = b * nbb + j
            rank = post_ref[blk, pl.ds(e, 1), :]
            shift = (cum_ref[blk * N_EXPERTS + e] - s0 - 1).astype(F32)
            onehot = jnp.logical_and(rank > 0.0, rank + shift == slot).astype(BF16)
            rows = h_ref[0, pl.ds(pl.multiple_of(j * t, t), t), :]
            xg_sc[...] += jnp.dot(onehot, rows, preferred_element_type=F32)
            return carry

        lax.fori_loop(lo_ref[ti], hi_ref[ti], gather, 0)
        x = xg_sc[...].astype(BF16)
        acc = jnp.zeros(xg_sc.shape, F32)
        for f in range(n_col):
            a = jnp.dot(x, w1_ref[0, :, f * tf:(f + 1) * tf], preferred_element_type=F32)
            g = jnp.dot(x, w3_ref[0, :, f * tf:(f + 1) * tf], preferred_element_type=F32)
            hh = (a * _sigmoid(a) * g).astype(BF16)
            acc = acc + jnp.dot(hh, w2_ref[0, f * tf:(f + 1) * tf, :], preferred_element_type=F32)
        o_ref[...] = acc.astype(BF16)


def _moe_experts(h2, post, tables, w1, w3, w2, batch, seq):
    n, d = h2.shape
    t = MOE_BLOCK
    nbb = seq // t
    n_tiles = tables[0].shape[0]
    dff = w1.shape[2]
    tf = FFN_COL_TILE
    once = pl.Buffered(1)
    grid_spec = pltpu.PrefetchScalarGridSpec(
        num_scalar_prefetch=len(tables),
        grid=(n_tiles,),
        in_specs=[
            pl.BlockSpec((1, seq, d), lambda i, tb, te, *_: (tb[i], 0, 0), pipeline_mode=once),
            pl.BlockSpec(post.shape, lambda i, *_: (0, 0, 0)),
            pl.BlockSpec((1, d, dff), lambda i, tb, te, *_: (te[i], 0, 0), pipeline_mode=once),
            pl.BlockSpec((1, d, dff), lambda i, tb, te, *_: (te[i], 0, 0), pipeline_mode=once),
            pl.BlockSpec((1, dff, d), lambda i, tb, te, *_: (te[i], 0, 0), pipeline_mode=once),
        ],
        out_specs=pl.BlockSpec((t, d), lambda i, *_: (i, 0)),
        scratch_shapes=[pltpu.VMEM((t, d), F32)],
    )
    return pl.pallas_call(
        functools.partial(_moe_expert_kernel, n_col=dff // tf, tf=tf, nbb=nbb),
        out_shape=jax.ShapeDtypeStruct((n_tiles * t, d), BF16),
        grid_spec=grid_spec,
        compiler_params=_params("arbitrary"),
        name="moe_experts",
    )(*tables, h2.reshape(batch, seq, d), post, w1, w3, w2)


def _moe_combine_kernel(st_ref, so_ref, ya_ref, yb_ref, pos_ref, gates_ref, x_ref, mod_ref, ng_ref,
                        o_ref, acc_sc):
    i = pl.program_id(0)
    e = pl.program_id(1)
    t = MOE_BLOCK

    @pl.when(e == 0)
    def _():
        acc_sc[...] = jnp.zeros(acc_sc.shape, F32)

    rank = _pick_lane(pos_ref[...], e)
    off = so_ref[i * N_EXPERTS + e].astype(F32)
    slot = lax.broadcasted_iota(jnp.int32, (t, 2 * t), 1).astype(F32)
    onehot = jnp.logical_and(rank > 0.0, rank - 1.0 + off == slot).astype(BF16)
    ys = jnp.concatenate([ya_ref[...], yb_ref[...]], axis=0)
    acc_sc[...] += _pick_lane(gates_ref[...], e) * jnp.dot(onehot, ys, preferred_element_type=F32)

    @pl.when(e == N_EXPERTS - 1)
    def _():
        mod = mod_ref[0]
        o_ref[...] = x_ref[...] + mod[5:6] * _rms(acc_sc[...], ng_ref[3:4, :])


def _moe_combine(ys, pos, gates, x1, mod, ng, seg_tables, batch, seq):
    n, d = x1.shape
    t = MOE_BLOCK
    nbb = seq // t
    grid_spec = pltpu.PrefetchScalarGridSpec(
        num_scalar_prefetch=2,
        grid=(n // t, N_EXPERTS),
        in_specs=[
            pl.BlockSpec((t, d), lambda i, e, st, so: (st[i * N_EXPERTS + e], 0)),
            pl.BlockSpec((t, d), lambda i, e, st, so: (st[i * N_EXPERTS + e] + 1, 0)),
            pl.BlockSpec((t, LANES), lambda i, e, st, so: (i, 0)),
            pl.BlockSpec((t, LANES), lambda i, e, st, so: (i, 0)),
            pl.BlockSpec((t, d), lambda i, e, st, so: (i, 0)),
            pl.BlockSpec((1, 6, d), lambda i, e, st, so: (i // nbb, 0, 0)),
            pl.BlockSpec(ng.shape, lambda i, e, st, so: (0, 0)),
        ],
        out_specs=pl.BlockSpec((t, d), lambda i, e, st, so: (i, 0)),
        scratch_shapes=[pltpu.VMEM((t, d), F32)],
    )
    return pl.pallas_call(
        _moe_combine_kernel,
        out_shape=jax.ShapeDtypeStruct((n, d), F32),
        grid_spec=grid_spec,
        compiler_params=_params("parallel", "arbitrary"),
        name="moe_combine",
    )(*seg_tables, ys, ys, pos, gates, x1, mod, ng)


@jax.jit
def _forward(x, c, rel_bias, ada_w, ada_b, norm_g, w_in, gate_b, conv_w, conv_b, lam_vec,
             attn_norm_g, mlstm_norm_g, w_out, ffn_w1, ffn_w3, ffn_w2, router_w, moe_w1, moe_w3, moe_w2):
    batch, seq, d = x.shape
    depth = ada_w.shape[0]
    n = batch * seq
    n_main = N_SLABS * HEAD_GROUP_W
    assert seq % ATTN_BLOCK == 0 and seq % FFN_ROW_TILE == 0 and seq % MLSTM_CHUNK == 0
    assert ROW_TILE % MOE_BLOCK == 0
    assert seq >= 2 * ATTN_BLOCK and ROW_TILE == ATTN_BLOCK
    assert w_in.shape[2] == n_main + 2 * M_HEADS

    mod_all = _ada(c, ada_w, ada_b)
    bias_tiles = _bias_tables(rel_bias, seq)

    xf = x.reshape(n, d)
    for i in range(depth):
        mod = mod_all[i]
        ng = norm_g[i]
        lam_init = 0.8 - 0.6 * math.exp(-0.3 * i)

        w_gate = w_in[i][:, n_main:]
        w_bf = w_in[i][:, :n_main].astype(BF16)
        v0, v1 = 2 * HEAD_GROUP_W, 3 * HEAD_GROUP_W
        slabs = _inproj(
            xf, mod, ng[0:1], jnp.concatenate([w_bf[:, :v0], w_bf[:, v1:]], axis=1), w_bf[:, v0:v1].T,
            jnp.zeros((d, LANES), BF16).at[:, :2 * M_HEADS].set(w_gate.astype(BF16)),
            w_gate.T.astype(BF16),
            jnp.zeros((1, LANES), F32).at[0, :2 * M_HEADS].set(gate_b[i].reshape(-1)),
            gate_b[i].reshape(2 * M_HEADS, 1), batch, seq)
        aq, ak, mq, mk, mv, mo = (s.reshape(batch, seq, HEAD_GROUP_W) for s in slabs[:N_SLABS - 1])
        avt, grow, gcol = slabs[N_SLABS - 1:]

        scal = jnp.full((1,), lam_init, F32)
        a_out = _attention(aq, ak, avt, scal, bias_tiles, lam_vec[i], attn_norm_g[i].reshape(LANES, 1))

        h_m = _mlstm(mq, mk, mv, mo, grow, gcol,
                     conv_w[i].reshape(CONV_K, 2 * M_HEADS, LANES).transpose(1, 0, 2),
                     conv_b[i].reshape(2 * M_HEADS, 1, LANES),
                     mlstm_norm_g[i].reshape(M_HEADS, 1, LANES))

        w_o = w_out[i].astype(BF16)
        j = i // 2
        if i % 2 == 0:
            x1, h2 = _outproj(a_out.reshape(n, -1), h_m.reshape(n, -1), xf, mod, ng,
                              w_o[:HEAD_GROUP_W], w_o[HEAD_GROUP_W:], None, batch, seq)
            xf = _ffn(h2, x1, mod, ng, ffn_w1[j].astype(BF16), ffn_w3[j].astype(BF16),
                      ffn_w2[j].astype(BF16), batch, seq)
        else:
            rw = jnp.zeros((d, LANES), F32).at[:, :N_EXPERTS].set(router_w[j])
            x1, h2, gates, pos, post, cnt = _outproj(a_out.reshape(n, -1), h_m.reshape(n, -1), xf, mod, ng,
                                                     w_o[:HEAD_GROUP_W], w_o[HEAD_GROUP_W:], rw, batch, seq)
            tile_tables, seg_tables = _moe_tables(cnt[:, 0, :N_EXPERTS].astype(jnp.int32), batch, seq)
            ys = _moe_experts(h2, post, tile_tables, moe_w1[j].astype(BF16), moe_w3[j].astype(BF16),
                              moe_w2[j].astype(BF16), batch, seq)
            xf = _moe_combine(ys, pos, gates, x1, mod, ng, seg_tables, batch, seq)
    return xf.reshape(batch, seq, d)


def kernel(x, c, rel_bias, ada_w, ada_b, norm_g, w_in, gate_b, conv_w, conv_b, lam_vec, attn_norm_g,
           mlstm_norm_g, w_out, ffn_w1, ffn_w3, ffn_w2, router_w, moe_w1, moe_w3, moe_w2):
    return _forward(x, c, rel_bias, ada_w, ada_b, norm_g, w_in, gate_b, conv_w, conv_b, lam_vec,
                    attn_norm_g, mlstm_norm_g, w_out, ffn_w1, ffn_w3, ffn_w2, router_w, moe_w1, moe_w3, moe_w2)
```

```python
import functools
import math

import numpy as np
import jax
import jax.numpy as jnp
from jax import lax
from jax.experimental import pallas as pl
from jax.experimental.pallas import tpu as pltpu

F32 = jnp.float32
BF16 = jnp.bfloat16
HIGHEST = lax.Precision.HIGHEST

A_HEADS = 4
A_DH = 64
M_HEADS = 4
M_DH = 128
CONV_K = 4
N_BUCKETS = 32
MAX_DIST = 128
N_EXPERTS = 8
EPS = 1e-6

LANES = 128
HEAD_GROUP_W = 512
N_SLABS = 7
NEG = -1e30
LOG2E = math.log2(math.e)
Q_SCALE = A_DH ** -0.5 * LOG2E
VMEM_LIMIT_BYTES = 56 * 1024 * 1024

ATTN_BLOCK = 512
MLSTM_CHUNK = 256
ROW_TILE = 512
FFN_ROW_TILE = 1024
FFN_COL_TILE = 512
ADA_COL_TILE = 1536
MOE_BLOCK = 256


def _params(*sem):
    return pltpu.CompilerParams(dimension_semantics=sem, vmem_limit_bytes=VMEM_LIMIT_BYTES)


def _rms(x, g):
    return x * lax.rsqrt(jnp.mean(x * x, axis=-1, keepdims=True) + EPS) * g


def _sigmoid(x):
    return 1.0 / (1.0 + jnp.exp(-x))


def _log_sigmoid(x):
    return jnp.minimum(x, 0.0) - jnp.log(1.0 + jnp.exp(-jnp.abs(x)))


def _pick_lane(x, idx):
    lane = lax.broadcasted_iota(jnp.int32, x.shape, 1)
    return jnp.sum(jnp.where(lane == idx, x, 0.0), axis=1, keepdims=True)


def _pick_row(x, idx):
    row = lax.broadcasted_iota(jnp.int32, x.shape, 0)
    return jnp.sum(jnp.where(row == idx, x, 0.0), axis=0, keepdims=True)


def _ada_kernel(c_ref, w_ref, b_ref, o_ref):
    c = c_ref[...]
    ca = c * _sigmoid(c)
    o_ref[0] = jnp.dot(ca, w_ref[0], precision=HIGHEST, preferred_element_type=F32) + b_ref[0]


def _ada(c, ada_w, ada_b):
    depth, d, d6 = ada_w.shape
    b = c.shape[0]
    rows = 8
    cp = jnp.zeros((rows, d), F32).at[:b].set(c)
    out = pl.pallas_call(
        _ada_kernel,
        out_shape=jax.ShapeDtypeStruct((depth, rows, d6), F32),
        grid=(depth, d6 // ADA_COL_TILE),
        in_specs=[
            pl.BlockSpec((rows, d), lambda l, j: (0, 0)),
            pl.BlockSpec((1, d, ADA_COL_TILE), lambda l, j: (l, 0, j)),
            pl.BlockSpec((1, 1, ADA_COL_TILE), lambda l, j: (l, 0, j)),
        ],
        out_specs=pl.BlockSpec((1, rows, ADA_COL_TILE), lambda l, j: (l, 0, j)),
        compiler_params=_params("parallel", "parallel"),
        name="ada_mod",
    )(cp, ada_w, ada_b.reshape(depth, 1, d6))
    return out[:, :b].reshape(depth, b, 6, d)


def _inproj_kernel(x_ref, mod_ref, g_ref, w_ref, wqt_ref, wvt_ref, wg_ref, wgt_ref, gb_ref, gbt_ref,
                   ak, mq, mk, mv, mo, aqt, avt, grow, gcol):
    mod = mod_ref[0]
    h = _rms(x_ref[...], g_ref[...]) * (1.0 + mod[1:2]) + mod[0:1]
    hb = h.astype(BF16)
    for s, o in enumerate((ak, mq, mk, mv, mo)):
        w = w_ref[:, s * HEAD_GROUP_W:(s + 1) * HEAD_GROUP_W]
        o[...] = jnp.dot(hb, w, preferred_element_type=F32).astype(BF16)
    nt = (((1,), (1,)), ((), ()))
    aqt[0, 0] = (lax.dot_general(wqt_ref[...], hb, nt, preferred_element_type=F32) * Q_SCALE).astype(BF16)
    avt[0, 0] = lax.dot_general(wvt_ref[...], hb, nt, preferred_element_type=F32).astype(BF16)
    grow[...] = jnp.dot(hb, wg_ref[...], preferred_element_type=F32) + gb_ref[...]
    gcol[0] = lax.dot_general(wgt_ref[...], hb, (((1,), (1,)), ((), ())),
                              preferred_element_type=F32) + gbt_ref[...]


def _inproj(xf, mod, g, w_main, w_qt, w_vt, wg, wgt, gb, gbt, batch, seq):
    n, d = xf.shape
    tm = ROW_TILE
    tpb = seq // tm
    n_row_slabs = N_SLABS - 2
    slab = jax.ShapeDtypeStruct((n, HEAD_GROUP_W), BF16)
    slab_t = jax.ShapeDtypeStruct((batch, tpb, HEAD_GROUP_W, tm), BF16)
    row_spec = pl.BlockSpec((tm, HEAD_GROUP_W), lambda i: (i, 0))
    t_spec = pl.BlockSpec((1, 1, HEAD_GROUP_W, tm), lambda i: (i // tpb, i % tpb, 0, 0))
    const = lambda i: (0, 0)
    return pl.pallas_call(
        _inproj_kernel,
        out_shape=[slab] * n_row_slabs + [slab_t, slab_t, jax.ShapeDtypeStruct((n, LANES), F32),
                                          jax.ShapeDtypeStruct((batch, 8, seq), F32)],
        grid=(n // tm,),
        in_specs=[
            pl.BlockSpec((tm, d), lambda i: (i, 0)),
            pl.BlockSpec((1, 6, d), lambda i: (i // tpb, 0, 0)),
            pl.BlockSpec((1, d), const),
            pl.BlockSpec(w_main.shape, const),
            pl.BlockSpec(w_qt.shape, const),
            pl.BlockSpec(w_vt.shape, const),
            pl.BlockSpec(wg.shape, const),
            pl.BlockSpec(wgt.shape, const),
            pl.BlockSpec(gb.shape, const),
            pl.BlockSpec(gbt.shape, const),
        ],
        out_specs=[row_spec] * n_row_slabs + [t_spec, t_spec,
                                              pl.BlockSpec((tm, LANES), lambda i: (i, 0)),
                                              pl.BlockSpec((1, 8, tm), lambda i: (i // tpb, 0, i % tpb))],
        compiler_params=_params("parallel"),
        name="in_proj",
    )(xf, mod, g, w_main, w_qt, w_vt, wg, wgt, gb, gbt)


ACC_ROWS = LANES + 16


def _attn_kernel(scal_ref, qt_ref, k_ref, vt_ref, bias_ref, lv_ref, g_ref, o_ref, m_sc, acc_sc,
                 sa_sc, sb_sc, mxa_sc, mxb_sc, *, blk):
    qi = pl.program_id(2)
    lam_init = scal_ref[0]

    feat = lax.broadcasted_iota(jnp.int32, (LANES, blk), 0)
    qt = qt_ref[0, 0]
    zero = jnp.zeros_like(qt)
    qz = (jnp.where(feat < A_DH, qt, zero), jnp.where(feat >= A_DH, qt, zero))
    ones_rows = jnp.ones((ACC_ROWS - LANES, blk), BF16)

    m_sc[...] = jnp.full(m_sc.shape, NEG, F32)
    acc_sc[...] = jnp.zeros(acc_sc.shape, F32)

    def scores(kj, s_sc, mx_sc):
        start = pl.multiple_of(kj * blk, blk)
        kt = k_ref[0, pl.ds(start, blk), :]
        bias = bias_ref[0, jnp.minimum(qi - kj, 2)]
        for c in range(2):
            s = jnp.dot(kt, qz[c], preferred_element_type=F32) + bias
            s_sc[c] = s
            mx_sc[c] = jnp.max(s, axis=0, keepdims=True)

    def accumulate(kj, s_sc, mx_sc):
        vext = jnp.concatenate([vt_ref[0, kj], ones_rows], axis=0)
        for c in range(2):
            m_old = m_sc[c]
            m_new = jnp.maximum(m_old, mx_sc[c])
            alpha = jnp.exp2(m_old - m_new)
            p = jnp.exp2(s_sc[c] - m_new).astype(BF16)
            pv = jnp.dot(vext, p, preferred_element_type=F32)
            acc_sc[c] = alpha * acc_sc[c] + pv
            m_sc[c] = m_new

    scores(0, sa_sc, mxa_sc)

    def pair_body(u, carry):
        scores(2 * u + 1, sb_sc, mxb_sc)
        accumulate(2 * u, sa_sc, mxa_sc)
        scores(2 * u + 2, sa_sc, mxa_sc)
        accumulate(2 * u + 1, sb_sc, mxb_sc)
        return carry

    lax.fori_loop(0, qi // 2, pair_body, 0)

    @pl.when(qi % 2 == 1)
    def _():
        scores(qi, sb_sc, mxb_sc)
        accumulate(qi - 1, sa_sc, mxa_sc)
        accumulate(qi, sb_sc, mxb_sc)

    @pl.when(qi % 2 == 0)
    def _():
        accumulate(qi, sa_sc, mxa_sc)

    acc0 = acc_sc[0]
    acc1 = acc_sc[1]
    o0 = acc0[:LANES] * (1.0 / acc0[LANES:LANES + 1])
    o1 = acc1[:LANES] * (1.0 / acc1[LANES:LANES + 1])
    lv = lv_ref[...]
    lam = (jnp.exp(jnp.sum(lv[0:1] * lv[1:2], axis=1, keepdims=True))
           - jnp.exp(jnp.sum(lv[2:3] * lv[3:4], axis=1, keepdims=True)) + lam_init)
    a = o0 - lam * o1
    y = a * lax.rsqrt(jnp.mean(a * a, axis=0, keepdims=True) + EPS) * g_ref[...] * (1.0 - lam_init)
    o_ref[0] = y.T.astype(BF16)


def _attention(aqt, ak, avt, scal, bias_tiles, lam_vec, norm_g):
    batch, seq, _ = ak.shape
    blk = ATTN_BLOCK
    nk = seq // blk
    kernel = functools.partial(_attn_kernel, blk=blk)
    grid_spec = pltpu.PrefetchScalarGridSpec(
        num_scalar_prefetch=1,
        grid=(batch, A_HEADS, nk),
        in_specs=[
            pl.BlockSpec((1, 1, LANES, blk), lambda b, h, i, s: (b, i, h, 0)),
            pl.BlockSpec((1, seq, LANES), lambda b, h, i, s: (b, 0, h)),
            pl.BlockSpec((1, nk, LANES, blk), lambda b, h, i, s: (b, 0, h, 0)),
            pl.BlockSpec((1, 3, blk, blk), lambda b, h, i, s: (h, 0, 0, 0)),
            pl.BlockSpec(lam_vec.shape, lambda b, h, i, s: (0, 0)),
            pl.BlockSpec(norm_g.shape, lambda b, h, i, s: (0, 0)),
        ],
        out_specs=pl.BlockSpec((1, blk, LANES), lambda b, h, i, s: (b, i, h)),
        scratch_shapes=[pltpu.VMEM((2, 1, blk), F32), pltpu.VMEM((2, ACC_ROWS, blk), F32),
                        pltpu.VMEM((2, blk, blk), F32), pltpu.VMEM((2, blk, blk), F32),
                        pltpu.VMEM((2, 1, blk), F32), pltpu.VMEM((2, 1, blk), F32)],
    )
    return pl.pallas_call(
        kernel,
        out_shape=jax.ShapeDtypeStruct(ak.shape, BF16),
        grid_spec=grid_spec,
        compiler_params=_params("parallel", "parallel", "arbitrary"),
        name="diff_attn",
    )(scal, aqt, ak, avt, bias_tiles, lam_vec, norm_g)


def _t5_bucket(n):
    max_exact = N_BUCKETS // 2
    nf = jnp.maximum(n, 1).astype(F32)
    large = max_exact + (jnp.log(nf / max_exact) / math.log(MAX_DIST / max_exact)
                         * (N_BUCKETS - max_exact)).astype(jnp.int32)
    large = jnp.minimum(large, N_BUCKETS - 1)
    return jnp.where(n < max_exact, n, large)


def _bias_tables(rel_bias, seq):
    blk = ATTN_BLOCK
    far = np.arange(blk + 1, max(seq, blk + 2), dtype=np.float64)
    sat = (N_BUCKETS // 2) + np.log(far / (N_BUCKETS // 2)) / math.log(MAX_DIST / (N_BUCKETS // 2)) * (N_BUCKETS // 2)
    assert np.all(sat >= N_BUCKETS - 0.5), "far tiles need a saturated distance bucket"
    dist_bias = rel_bias[_t5_bucket(jnp.arange(seq, dtype=jnp.int32))].T.astype(F32)
    heads = dist_bias.shape[0]
    last = dist_bias[:, seq - 1:seq]
    padded = jnp.concatenate([jnp.full((heads, blk), NEG, F32), dist_bias,
                              jnp.broadcast_to(last, (heads, blk))], axis=1)
    tiles = []
    for d in (0, 1):
        base = blk + d * blk
        w = jnp.concatenate([padded[:, base:base + blk + 1], padded[:, base - blk + 1:base]], axis=1)
        t = jnp.tile(w, (1, blk))[:, :blk * (2 * blk - 1)].reshape(heads, blk, 2 * blk - 1)[:, :, :blk]
        tiles.append(t)
    tiles.append(jnp.broadcast_to(last[:, :, None], (heads, blk, blk)))
    return jnp.stack(tiles, axis=1) * LOG2E


def _mlstm_kernel(mq_ref, mk_ref, mv_ref, mo_ref, grow_ref, gcol_ref, cwq_ref, cwk_ref, cbq_ref, cbk_ref,
                  ng_ref, o_ref, xq_sc, xk_sc, c_sc, n_sc, m_sc, *, chunk):
    L = chunk
    h = pl.program_id(1)
    ci = pl.program_id(2)

    @pl.when(ci == 0)
    def _():
        xq_sc[0:8, :] = jnp.zeros((8, LANES), F32)
        xk_sc[0:8, :] = jnp.zeros((8, LANES), F32)
        c_sc[...] = jnp.zeros(c_sc.shape, F32)
        n_sc[...] = jnp.zeros(n_sc.shape, F32)
        m_sc[...] = jnp.zeros(m_sc.shape, F32)

    def conv_silu(x_ref, sc, w_ref, b_ref):
        sc[8:8 + L, :] = x_ref[0].astype(F32)
        w = w_ref[0]
        y = b_ref[0]
        for j in range(CONV_K):
            y = y + sc[5 + j:5 + j + L, :] * w[j:j + 1]
        sc[0:8, :] = sc[L:L + 8, :]
        return y * _sigmoid(y)

    q = conv_silu(mq_ref, xq_sc, cwq_ref, cbq_ref)
    k = conv_silu(mk_ref, xk_sc, cwk_ref, cbk_ref) * (M_DH ** -0.5)
    v = mv_ref[0]

    r = lax.broadcasted_iota(jnp.int32, (L, L), 0)
    cidx = lax.broadcasted_iota(jnp.int32, (L, L), 1)
    causal = r >= cidx
    lower = causal.astype(F32)
    upper = (r <= cidx).astype(F32)

    g_rows = grow_ref[...]
    cum_rows = jnp.dot(lower, _log_sigmoid(g_rows), precision=HIGHEST, preferred_element_type=F32)
    ig_col = _pick_lane(g_rows, h)
    b_col = _pick_lane(cum_rows, M_HEADS + h)
    g_cols = gcol_ref[0]
    cum_cols = jnp.dot(_log_sigmoid(g_cols), upper, precision=HIGHEST, preferred_element_type=F32)
    ig_row = _pick_row(g_cols, h)
    b_row = _pick_row(cum_cols, M_HEADS + h)
    b_last = b_col[L - 1:L, :]
    m_prev = m_sc[...]

    dlog = jnp.where(causal, b_col - b_row + ig_row, NEG)
    inter = b_col + m_prev
    m_t = jnp.maximum(inter, jnp.max(dlog, axis=1, keepdims=True))
    qb = q.astype(BF16)
    kb = k.astype(BF16)
    sw = lax.dot_general(qb, kb, (((1,), (1,)), ((), ())), preferred_element_type=F32) * jnp.exp(dlog - m_t)
    dec = jnp.exp(inter - m_t)
    num = (dec * jnp.dot(qb, c_sc[...].astype(BF16), preferred_element_type=F32)
           + jnp.dot(sw.astype(BF16), v, preferred_element_type=F32))
    den = dec * jnp.sum(q * n_sc[...], axis=1, keepdims=True) + jnp.sum(sw, axis=1, keepdims=True)
    h_t = num * (1.0 / jnp.maximum(jnp.abs(den), jnp.exp(-m_t)))

    a_col = b_last - b_col + ig_col
    m_new = jnp.maximum(b_last + m_prev, jnp.max(a_col, axis=0, keepdims=True))
    decay = jnp.exp(b_last + m_prev - m_new)
    kw = k * jnp.exp(a_col - m_new)
    c_sc[...] = decay * c_sc[...] + lax.dot_general(kw.astype(BF16), v, (((0,), (0,)), ((), ())),
                                                    preferred_element_type=F32)
    n_sc[...] = decay * n_sc[...] + jnp.sum(kw, axis=0, keepdims=True)
    m_sc[...] = m_new

    hm = _sigmoid(mo_ref[0].astype(F32)) * h_t
    o_ref[0] = _rms(hm, ng_ref[0]).astype(BF16)


def _mlstm(mq, mk, mv, mo, grow, gcol, conv_w, conv_b, norm_g):
    batch, seq, _ = mq.shape
    L = MLSTM_CHUNK
    kernel = functools.partial(_mlstm_kernel, chunk=L)
    slab = pl.BlockSpec((1, L, LANES), lambda b, h, c: (b, c, h))
    nchunk = seq // L
    return pl.pallas_call(
        kernel,
        out_shape=jax.ShapeDtypeStruct(mq.shape, BF16),
        grid=(batch, M_HEADS, nchunk),
        in_specs=[
            slab, slab, slab, slab,
            pl.BlockSpec((L, LANES), lambda b, h, c: (b * nchunk + c, 0)),
            pl.BlockSpec((1, 8, L), lambda b, h, c: (b, 0, c)),
            pl.BlockSpec((1, CONV_K, LANES), lambda b, h, c: (h, 0, 0)),
            pl.BlockSpec((1, CONV_K, LANES), lambda b, h, c: (M_HEADS + h, 0, 0)),
            pl.BlockSpec((1, 1, LANES), lambda b, h, c: (h, 0, 0)),
            pl.BlockSpec((1, 1, LANES), lambda b, h, c: (M_HEADS + h, 0, 0)),
            pl.BlockSpec((1, 1, LANES), lambda b, h, c: (h, 0, 0)),
        ],
        out_specs=slab,
        scratch_shapes=[pltpu.VMEM((L + 8, LANES), F32), pltpu.VMEM((L + 8, LANES), F32),
                        pltpu.VMEM((M_DH, M_DH), F32), pltpu.VMEM((1, M_DH), F32), pltpu.VMEM((1, 1), F32)],
        compiler_params=_params("parallel", "parallel", "arbitrary"),
        name="mlstm",
    )(mq, mk, mv, mo, grow, gcol, conv_w, conv_w, conv_b, conv_b, norm_g)


def _outproj_kernel(*refs, moe):
    if moe:
        (a_ref, hm_ref, x_ref, mod_ref, ng_ref, wt_ref, wb_ref, rw_ref,
         x1_ref, h2_ref, gates_ref, pos_ref, post_ref, cnt_ref) = refs
    else:
        a_ref, hm_ref, x_ref, mod_ref, ng_ref, wt_ref, wb_ref, x1_ref, h2_ref = refs
    y = (jnp.dot(a_ref[...], wt_ref[...], preferred_element_type=F32)
         + jnp.dot(hm_ref[...], wb_ref[...], preferred_element_type=F32))
    mod = mod_ref[0]
    ng = ng_ref[...]
    x1 = x_ref[...] + mod[2:3] * _rms(y, ng[1:2])
    x1_ref[...] = x1
    h2 = _rms(x1, ng[2:3]) * (1.0 + mod[4:5]) + mod[3:4]
    h2_ref[...] = h2.astype(BF16)
    if moe:
        logits = jnp.dot(h2, rw_ref[...], precision=HIGHEST, preferred_element_type=F32)
        lane = lax.broadcasted_iota(jnp.int32, logits.shape, 1)
        lg = jnp.where(lane < N_EXPERTS, logits, NEG)
        v1 = jnp.max(lg, axis=1, keepdims=True)
        i1 = jnp.min(jnp.where(lg == v1, lane, LANES), axis=1, keepdims=True)
        lg2 = jnp.where(lane == i1, NEG, lg)
        v2 = jnp.max(lg2, axis=1, keepdims=True)
        i2 = jnp.min(jnp.where(lg2 == v2, lane, LANES), axis=1, keepdims=True)
        e2 = jnp.exp(v2 - v1)
        w1 = 1.0 / (1.0 + e2)
        gates_ref[...] = jnp.where(lane == i1, w1, 0.0) + jnp.where(lane == i2, e2 * w1, 0.0)
        sel = jnp.logical_or(lane == i1, lane == i2)
        t = MOE_BLOCK
        r = lax.broadcasted_iota(jnp.int32, (t, t), 0)
        cidx = lax.broadcasted_iota(jnp.int32, (t, t), 1)
        lower = (r >= cidx).astype(BF16)
        for blk in range(logits.shape[0] // t):
            sel_b = sel[blk * t:(blk + 1) * t]
            rank = jnp.dot(lower, sel_b.astype(BF16), preferred_element_type=F32)
            pos = jnp.where(sel_b, rank, 0.0)
            pos_ref[blk * t:(blk + 1) * t, :] = pos
            post_ref[blk] = pos.T[:N_EXPERTS]
            cnt_ref[blk] = rank[t - 1:t, :]


def _outproj(a_out, h_m, xf, mod, ng, w_top, w_bot, router_w, batch, seq):
    n, d = xf.shape
    tm = ROW_TILE
    tpb = seq // tm
    moe = router_w is not None
    half = pl.BlockSpec((tm, HEAD_GROUP_W), lambda i: (i, 0))
    full = pl.BlockSpec((tm, d), lambda i: (i, 0))
    in_specs = [half, half, full,
                pl.BlockSpec((1, 6, d), lambda i: (i // tpb, 0, 0)),
                pl.BlockSpec(ng.shape, lambda i: (0, 0)),
                pl.BlockSpec(w_top.shape, lambda i: (0, 0)),
                pl.BlockSpec(w_bot.shape, lambda i: (0, 0))]
    args = [a_out, h_m, xf, mod, ng, w_top, w_bot]
    out_shape = [jax.ShapeDtypeStruct((n, d), F32), jax.ShapeDtypeStruct((n, d), BF16)]
    out_specs = [full, full]
    if moe:
        in_specs.append(pl.BlockSpec(router_w.shape, lambda i: (0, 0)))
        args.append(router_w)
        bpt = tm // MOE_BLOCK
        out_shape += [jax.ShapeDtypeStruct((n, LANES), F32), jax.ShapeDtypeStruct((n, LANES), F32),
                      jax.ShapeDtypeStruct((n // MOE_BLOCK, N_EXPERTS, MOE_BLOCK), F32),
                      jax.ShapeDtypeStruct((n // MOE_BLOCK, 1, LANES), F32)]
        out_specs += [pl.BlockSpec((tm, LANES), lambda i: (i, 0)), pl.BlockSpec((tm, LANES), lambda i: (i, 0)),
                      pl.BlockSpec((bpt, N_EXPERTS, MOE_BLOCK), lambda i: (i, 0, 0)),
                      pl.BlockSpec((bpt, 1, LANES), lambda i: (i, 0, 0))]
    return pl.pallas_call(
        functools.partial(_outproj_kernel, moe=moe),
        out_shape=out_shape,
        grid=(n // tm,),
        in_specs=in_specs,
        out_specs=out_specs,
        compiler_params=_params("parallel"),
        name="out_proj_moe" if moe else "out_proj",
    )(*args)


def _ffn_kernel(h_ref, x_ref, mod_ref, ng_ref, w1_ref, w3_ref, w2_ref, o_ref, acc_sc, *, n_col):
    f = pl.program_id(1)

    @pl.when(f == 0)
    def _():
        acc_sc[...] = jnp.zeros(acc_sc.shape, F32)

    hb = h_ref[...]
    a = jnp.dot(hb, w1_ref[...], preferred_element_type=F32)
    b = jnp.dot(hb, w3_ref[...], preferred_element_type=F32)
    hh = a * _sigmoid(a) * b
    acc_sc[...] += jnp.dot(hh.astype(BF16), w2_ref[...], preferred_element_type=F32)

    @pl.when(f == n_col - 1)
    def _():
        mod = mod_ref[0]
        o_ref[...] = x_ref[...] + mod[5:6] * _rms(acc_sc[...], ng_ref[3:4, :])


def _ffn(h2, x1, mod, ng, w1, w3, w2, batch, seq):
    n, d = x1.shape
    dff = w1.shape[1]
    tm, tf = FFN_ROW_TILE, FFN_COL_TILE
    tpb = seq // tm
    n_col = dff // tf
    row = lambda i, f: (i, 0)
    return pl.pallas_call(
        functools.partial(_ffn_kernel, n_col=n_col),
        out_shape=jax.ShapeDtypeStruct((n, d), F32),
        grid=(n // tm, n_col),
        in_specs=[pl.BlockSpec((tm, d), row), pl.BlockSpec((tm, d), row),
                  pl.BlockSpec((1, 6, d), lambda i, f: (i // tpb, 0, 0)),
                  pl.BlockSpec(ng.shape, lambda i, f: (0, 0)),
                  pl.BlockSpec((d, tf), lambda i, f: (0, f)), pl.BlockSpec((d, tf), lambda i, f: (0, f)),
                  pl.BlockSpec((tf, d), lambda i, f: (f, 0))],
        out_specs=pl.BlockSpec((tm, d), row),
        scratch_shapes=[pltpu.VMEM((tm, d), F32)],
        compiler_params=_params("parallel", "arbitrary"),
        name="dense_ffn",
    )(h2, x1, mod, ng, w1, w3, w2)


def _moe_tables(cnt, batch, seq):
    t = MOE_BLOCK
    e_n = N_EXPERTS
    nbb = seq // t
    ntb = 2 * nbb + e_n
    c = cnt.reshape(batch, nbb, e_n)
    cum = jnp.cumsum(c, axis=1) - c
    tot = jnp.sum(c, axis=1)
    ntile = (tot + t - 1) // t
    tile_end = jnp.cumsum(ntile, axis=1)
    tile_base = tile_end - ntile
    tl = jnp.arange(ntb, dtype=jnp.int32)
    e_t = jnp.sum((tl[None, :, None] >= tile_end[:, None, :]).astype(jnp.int32), axis=2)
    valid = e_t < e_n
    e_c = jnp.minimum(e_t, e_n - 1)
    s0 = (tl[None, :] - jnp.take_along_axis(tile_base, e_c, axis=1)) * t
    bidx = jnp.arange(batch)[:, None]
    cum_sel = cum.transpose(0, 2, 1)[bidx, e_c]
    end_sel = cum_sel + c.transpose(0, 2, 1)[bidx, e_c]
    lo = jnp.sum((end_sel <= s0[..., None]).astype(jnp.int32), axis=2)
    hi = jnp.sum((cum_sel < (s0 + t)[..., None]).astype(jnp.int32), axis=2)

    def flat(a, tail):
        return jnp.concatenate([a.reshape(-1).astype(jnp.int32), jnp.full((1,), tail, jnp.int32)])

    tile_tables = (flat(jnp.broadcast_to(bidx, (batch, ntb)), batch - 1), flat(e_c, e_n - 1), flat(valid, 0),
                   flat(lo, 0), flat(hi, 0), flat(s0, 0), cum.reshape(-1).astype(jnp.int32))
    row0 = (bidx[:, :, None] * ntb + tile_base[:, None, :]) * t + cum
    seg_tables = ((row0 // t).reshape(-1).astype(jnp.int32), (row0 % t).reshape(-1).astype(jnp.int32))
    return tile_tables, seg_tables


def _moe_expert_kernel(tb_ref, te_ref, tv_ref, lo_ref, hi_ref, s0_ref, cum_ref,
                       h_ref, post_ref, w1_ref, w3_ref, w2_ref, o_ref, xg_sc, *, n_col, tf, nbb):
    ti = pl.program_id(0)
    t = MOE_BLOCK

    @pl.when(tv_ref[ti] == 0)
    def _():
        o_ref[...] = jnp.zeros(o_ref.shape, o_ref.dtype)

    @pl.when(tv_ref[ti] == 1)
    def _():
        b = tb_ref[ti]
        e = te_ref[ti]
        s0 = s0_ref[ti]
        xg_sc[...] = jnp.zeros(xg_sc.shape, F32)
        slot = lax.broadcasted_iota(jnp.int32, (t, t), 0).astype(F32)

        def gather(j, carry):
            blk = b * nbb + j
            rank = post_ref[blk, pl.ds(e, 1), :]
            shift = (cum_ref[blk * N_EXPERTS + e] - s0 - 1).astype(F32)
            onehot = jnp.logical_and(rank > 0.0, rank + shift == slot).astype(BF16)
            rows = h_ref[0, pl.ds(pl.multiple_of(j * t, t), t), :]
            xg_sc[...] += jnp.dot(onehot, rows, preferred_element_type=F32)
            return carry

        lax.fori_loop(lo_ref[ti], hi_ref[ti], gather, 0)
        x = xg_sc[...].astype(BF16)
        acc = jnp.zeros(xg_sc.shape, F32)
        for f in range(n_col):
            a = jnp.dot(x, w1_ref[0, :, f * tf:(f + 1) * tf], preferred_element_type=F32)
            g = jnp.dot(x, w3_ref[0, :, f * tf:(f + 1) * tf], preferred_element_type=F32)
            hh = (a * _sigmoid(a) * g).astype(BF16)
            acc = acc + jnp.dot(hh, w2_ref[0, f * tf:(f + 1) * tf, :], preferred_element_type=F32)
        o_ref[...] = acc.astype(BF16)


def _moe_experts(h2, post, tables, w1, w3, w2, batch, seq):
    n, d = h2.shape
    t = MOE_BLOCK
    nbb = seq // t
    n_tiles = tables[0].shape[0]
    dff = w1.shape[2]
    tf = FFN_COL_TILE
    once = pl.Buffered(1)
    grid_spec = pltpu.PrefetchScalarGridSpec(
        num_scalar_prefetch=len(tables),
        grid=(n_tiles,),
        in_specs=[
            pl.BlockSpec((1, seq, d), lambda i, tb, te, *_: (tb[i], 0, 0), pipeline_mode=once),
            pl.BlockSpec(post.shape, lambda i, *_: (0, 0, 0)),
            pl.BlockSpec((1, d, dff), lambda i, tb, te, *_: (te[i], 0, 0), pipeline_mode=once),
            pl.BlockSpec((1, d, dff), lambda i, tb, te, *_: (te[i], 0, 0), pipeline_mode=once),
            pl.BlockSpec((1, dff, d), lambda i, tb, te, *_: (te[i], 0, 0), pipeline_mode=once),
        ],
        out_specs=pl.BlockSpec((t, d), lambda i, *_: (i, 0)),
        scratch_shapes=[pltpu.VMEM((t, d), F32)],
    )
    return pl.pallas_call(
        functools.partial(_moe_expert_kernel, n_col=dff // tf, tf=tf, nbb=nbb),
        out_shape=jax.ShapeDtypeStruct((n_tiles * t, d), BF16),
        grid_spec=grid_spec,
        compiler_params=_params("arbitrary"),
        name="moe_experts",
    )(*tables, h2.reshape(batch, seq, d), post, w1, w3, w2)


def _moe_combine_kernel(st_ref, so_ref, *refs):
    y_refs = refs[:2 * N_EXPERTS]
    pos_ref, gates_ref, x_ref, mod_ref, ng_ref, o_ref = refs[2 * N_EXPERTS:]
    i = pl.program_id(0)
    t = MOE_BLOCK
    pos = pos_ref[...]
    gates = gates_ref[...]
    slot = lax.broadcasted_iota(jnp.int32, (t, 2 * t), 1).astype(F32)
    acc = jnp.zeros(x_ref.shape, F32)
    for e in range(N_EXPERTS):
        rank = pos[:, e:e + 1]
        off = so_ref[i * N_EXPERTS + e].astype(F32)
        onehot = jnp.logical_and(rank > 0.0, rank - 1.0 + off == slot).astype(BF16)
        ys = jnp.concatenate([y_refs[2 * e][...], y_refs[2 * e + 1][...]], axis=0)
        acc = acc + gates[:, e:e + 1] * jnp.dot(onehot, ys, preferred_element_type=F32)
    mod = mod_ref[0]
    o_ref[...] = x_ref[...] + mod[5:6] * _rms(acc, ng_ref[3:4, :])


def _moe_combine(ys, pos, gates, x1, mod, ng, seg_tables, batch, seq):
    n, d = x1.shape
    t = MOE_BLOCK
    nbb = seq // t
    y_specs = []
    for e in range(N_EXPERTS):
        for nxt in range(2):
            y_specs.append(pl.BlockSpec((t, d), functools.partial(
                lambda i, st, so, e, nxt: (st[i * N_EXPERTS + e] + nxt, 0), e=e, nxt=nxt)))
    grid_spec = pltpu.PrefetchScalarGridSpec(
        num_scalar_prefetch=2,
        grid=(n // t,),
        in_specs=y_specs + [
            pl.BlockSpec((t, LANES), lambda i, st, so: (i, 0)),
            pl.BlockSpec((t, LANES), lambda i, st, so: (i, 0)),
            pl.BlockSpec((t, d), lambda i, st, so: (i, 0)),
            pl.BlockSpec((1, 6, d), lambda i, st, so: (i // nbb, 0, 0)),
            pl.BlockSpec(ng.shape, lambda i, st, so: (0, 0)),
        ],
        out_specs=pl.BlockSpec((t, d), lambda i, st, so: (i, 0)),
    )
    return pl.pallas_call(
        _moe_combine_kernel,
        out_shape=jax.ShapeDtypeStruct((n, d), F32),
        grid_spec=grid_spec,
        compiler_params=_params("parallel"),
        name="moe_combine",
    )(*seg_tables, *([ys] * (2 * N_EXPERTS)), pos, gates, x1, mod, ng)


@jax.jit
def _forward(x, c, rel_bias, ada_w, ada_b, norm_g, w_in, gate_b, conv_w, conv_b, lam_vec,
             attn_norm_g, mlstm_norm_g, w_out, ffn_w1, ffn_w3, ffn_w2, router_w, moe_w1, moe_w3, moe_w2):
    batch, seq, d = x.shape
    depth = ada_w.shape[0]
    n = batch * seq
    n_main = N_SLABS * HEAD_GROUP_W
    assert seq % ATTN_BLOCK == 0 and seq % FFN_ROW_TILE == 0 and seq % MLSTM_CHUNK == 0
    assert ROW_TILE % MOE_BLOCK == 0
    assert seq >= 2 * ATTN_BLOCK and ROW_TILE == ATTN_BLOCK
    assert w_in.shape[2] == n_main + 2 * M_HEADS

    mod_all = _ada(c, ada_w, ada_b)
    bias_tiles = _bias_tables(rel_bias, seq)

    xf = x.reshape(n, d)
    for i in range(depth):
        mod = mod_all[i]
        ng = norm_g[i]
        lam_init = 0.8 - 0.6 * math.exp(-0.3 * i)

        w_gate = w_in[i][:, n_main:]
        w_bf = w_in[i][:, :n_main].astype(BF16)
        gw = HEAD_GROUP_W
        slabs = _inproj(
            xf, mod, ng[0:1], jnp.concatenate([w_bf[:, gw:2 * gw], w_bf[:, 3 * gw:]], axis=1),
            w_bf[:, :gw].T, w_bf[:, 2 * gw:3 * gw].T,
            jnp.zeros((d, LANES), BF16).at[:, :2 * M_HEADS].set(w_gate.astype(BF16)),
            w_gate.T.astype(BF16),
            jnp.zeros((1, LANES), F32).at[0, :2 * M_HEADS].set(gate_b[i].reshape(-1)),
            gate_b[i].reshape(2 * M_HEADS, 1), batch, seq)
        ak, mq, mk, mv, mo = (s.reshape(batch, seq, HEAD_GROUP_W) for s in slabs[:N_SLABS - 2])
        aqt, avt, grow, gcol = slabs[N_SLABS - 2:]

        scal = jnp.full((1,), lam_init, F32)
        a_out = _attention(aqt, ak, avt, scal, bias_tiles, lam_vec[i], attn_norm_g[i].reshape(LANES, 1))

        h_m = _mlstm(mq, mk, mv, mo, grow, gcol,
                     conv_w[i].reshape(CONV_K, 2 * M_HEADS, LANES).transpose(1, 0, 2),
                     conv_b[i].reshape(2 * M_HEADS, 1, LANES),
                     mlstm_norm_g[i].reshape(M_HEADS, 1, LANES))

        w_o = w_out[i].astype(BF16)
        j = i // 2
        if i % 2 == 0:
            x1, h2 = _outproj(a_out.reshape(n, -1), h_m.reshape(n, -1), xf, mod, ng,
                              w_o[:HEAD_GROUP_W], w_o[HEAD_GROUP_W:], None, batch, seq)
            xf = _ffn(h2, x1, mod, ng, ffn_w1[j].astype(BF16), ffn_w3[j].astype(BF16),
                      ffn_w2[j].astype(BF16), batch, seq)
        else:
            rw = jnp.zeros((d, LANES), F32).at[:, :N_EXPERTS].set(router_w[j])
            x1, h2, gates, pos, post, cnt = _outproj(a_out.reshape(n, -1), h_m.reshape(n, -1), xf, mod, ng,
                                                     w_o[:HEAD_GROUP_W], w_o[HEAD_GROUP_W:], rw, batch, seq)
            tile_tables, seg_tables = _moe_tables(cnt[:, 0, :N_EXPERTS].astype(jnp.int32), batch, seq)
            ys = _moe_experts(h2, post, tile_tables, moe_w1[j].astype(BF16), moe_w3[j].astype(BF16),
                              moe_w2[j].astype(BF16), batch, seq)
            xf = _moe_combine(ys, pos, gates, x1, mod, ng, seg_tables, batch, seq)
    return xf.reshape(batch, seq, d)


def kernel(x, c, rel_bias, ada_w, ada_b, norm_g, w_in, gate_b, conv_w, conv_b, lam_vec, attn_norm_g,
           mlstm_norm_g, w_out, ffn_w1, ffn_w3, ffn_w2, router_w, moe_w1, moe_w3, moe_w2):
    return _forward(x, c, rel_bias, ada_w, ada_b, norm_g, w_in, gate_b, conv_w, conv_b, lam_vec,
                    attn_norm_g, mlstm_norm_g, w_out, ffn_w1, ffn_w3, ffn_w2, router_w, moe_w1, moe_w3, moe_w2)
```

```python
import functools
import math

import numpy as np
import jax
import jax.numpy as jnp
from jax import lax
from jax.experimental import pallas as pl
from jax.experimental.pallas import tpu as pltpu

F32 = jnp.float32
BF16 = jnp.bfloat16
HIGHEST = lax.Precision.HIGHEST

A_HEADS = 4
A_DH = 64
M_HEADS = 4
M_DH = 128
CONV_K = 4
N_BUCKETS = 32
MAX_DIST = 128
N_EXPERTS = 8
EPS = 1e-6

LANES = 128
HEAD_GROUP_W = 512
N_SLABS = 7
NEG = -1e30
LOG2E = math.log2(math.e)
Q_SCALE = A_DH ** -0.5 * LOG2E
VMEM_LIMIT_BYTES = 56 * 1024 * 1024

ATTN_BLOCK = 512
MLSTM_CHUNK = 256
MLSTM_HEADS_PER_STEP = 4
ROW_TILE = 512
FFN_ROW_TILE = 1024
FFN_COL_TILE = 512
ADA_COL_TILE = 1536
MOE_BLOCK = 256


def _params(*sem):
    return pltpu.CompilerParams(dimension_semantics=sem, vmem_limit_bytes=VMEM_LIMIT_BYTES)


def _rms(x, g):
    return x * lax.rsqrt(jnp.mean(x * x, axis=-1, keepdims=True) + EPS) * g


def _sigmoid(x):
    return 1.0 / (1.0 + jnp.exp(-x))


def _log_sigmoid(x):
    return jnp.minimum(x, 0.0) - jnp.log(1.0 + jnp.exp(-jnp.abs(x)))


def _split3(x):
    hi = x.astype(BF16)
    r1 = x - hi.astype(F32)
    mid = r1.astype(BF16)
    lo = (r1 - mid.astype(F32)).astype(BF16)
    return hi, mid, lo


def _pick_lane(x, idx):
    lane = lax.broadcasted_iota(jnp.int32, x.shape, 1)
    return jnp.sum(jnp.where(lane == idx, x, 0.0), axis=1, keepdims=True)


def _pick_row(x, idx):
    row = lax.broadcasted_iota(jnp.int32, x.shape, 0)
    return jnp.sum(jnp.where(row == idx, x, 0.0), axis=0, keepdims=True)


def _ada_kernel(c_ref, w_ref, b_ref, o_ref):
    c = c_ref[...]
    ca = c * _sigmoid(c)
    o_ref[0] = jnp.dot(ca, w_ref[0], precision=HIGHEST, preferred_element_type=F32) + b_ref[0]


def _ada(c, ada_w, ada_b):
    depth, d, d6 = ada_w.shape
    b = c.shape[0]
    rows = 8
    cp = jnp.zeros((rows, d), F32).at[:b].set(c)
    out = pl.pallas_call(
        _ada_kernel,
        out_shape=jax.ShapeDtypeStruct((depth, rows, d6), F32),
        grid=(depth, d6 // ADA_COL_TILE),
        in_specs=[
            pl.BlockSpec((rows, d), lambda l, j: (0, 0)),
            pl.BlockSpec((1, d, ADA_COL_TILE), lambda l, j: (l, 0, j)),
            pl.BlockSpec((1, 1, ADA_COL_TILE), lambda l, j: (l, 0, j)),
        ],
        out_specs=pl.BlockSpec((1, rows, ADA_COL_TILE), lambda l, j: (l, 0, j)),
        compiler_params=_params("parallel", "parallel"),
        name="ada_mod",
    )(cp, ada_w, ada_b.reshape(depth, 1, d6))
    return out[:, :b].reshape(depth, b, 6, d)


def _inproj_kernel(x_ref, mod_ref, g_ref, w_ref, wqt_ref, wvt_ref, wg_ref, wgt_ref, gb_ref, gbt_ref,
                   ak, mq, mk, mv, mo, aqt, avt, grow, gcol):
    mod = mod_ref[0]
    h = _rms(x_ref[...], g_ref[...]) * (1.0 + mod[1:2]) + mod[0:1]
    hb = h.astype(BF16)
    for s, o in enumerate((ak, mq, mk, mv, mo)):
        w = w_ref[:, s * HEAD_GROUP_W:(s + 1) * HEAD_GROUP_W]
        o[...] = jnp.dot(hb, w, preferred_element_type=F32).astype(BF16)
    nt = (((1,), (1,)), ((), ()))
    aqt[0, 0] = (lax.dot_general(wqt_ref[...], hb, nt, preferred_element_type=F32) * Q_SCALE).astype(BF16)
    avt[0, 0] = lax.dot_general(wvt_ref[...], hb, nt, preferred_element_type=F32).astype(BF16)
    grow[...] = jnp.dot(hb, wg_ref[...], preferred_element_type=F32) + gb_ref[...]
    gcol[0] = lax.dot_general(wgt_ref[...], hb, (((1,), (1,)), ((), ())),
                              preferred_element_type=F32) + gbt_ref[...]


def _inproj(xf, mod, g, w_main, w_qt, w_vt, wg, wgt, gb, gbt, batch, seq):
    n, d = xf.shape
    tm = ROW_TILE
    tpb = seq // tm
    n_row_slabs = N_SLABS - 2
    slab = jax.ShapeDtypeStruct((n, HEAD_GROUP_W), BF16)
    slab_t = jax.ShapeDtypeStruct((batch, tpb, HEAD_GROUP_W, tm), BF16)
    row_spec = pl.BlockSpec((tm, HEAD_GROUP_W), lambda i: (i, 0))
    t_spec = pl.BlockSpec((1, 1, HEAD_GROUP_W, tm), lambda i: (i // tpb, i % tpb, 0, 0))
    const = lambda i: (0, 0)
    return pl.pallas_call(
        _inproj_kernel,
        out_shape=[slab] * n_row_slabs + [slab_t, slab_t, jax.ShapeDtypeStruct((n, LANES), F32),
                                          jax.ShapeDtypeStruct((batch, 8, seq), F32)],
        grid=(n // tm,),
        in_specs=[
            pl.BlockSpec((tm, d), lambda i: (i, 0)),
            pl.BlockSpec((1, 6, d), lambda i: (i // tpb, 0, 0)),
            pl.BlockSpec((1, d), const),
            pl.BlockSpec(w_main.shape, const),
            pl.BlockSpec(w_qt.shape, const),
            pl.BlockSpec(w_vt.shape, const),
            pl.BlockSpec(wg.shape, const),
            pl.BlockSpec(wgt.shape, const),
            pl.BlockSpec(gb.shape, const),
            pl.BlockSpec(gbt.shape, const),
        ],
        out_specs=[row_spec] * n_row_slabs + [t_spec, t_spec,
                                              pl.BlockSpec((tm, LANES), lambda i: (i, 0)),
                                              pl.BlockSpec((1, 8, tm), lambda i: (i // tpb, 0, i % tpb))],
        compiler_params=_params("parallel"),
        name="in_proj",
    )(xf, mod, g, w_main, w_qt, w_vt, wg, wgt, gb, gbt)


ACC_ROWS = LANES + 16


def _attn_kernel(scal_ref, qt_ref, k_ref, vt_ref, bias_ref, lv_ref, g_ref, o_ref, m_sc, acc_sc,
                 sa_sc, sb_sc, mxa_sc, mxb_sc, *, blk):
    qi = pl.program_id(2)
    lam_init = scal_ref[0]

    feat = lax.broadcasted_iota(jnp.int32, (LANES, blk), 0)
    qt = qt_ref[0, 0]
    zero = jnp.zeros_like(qt)
    qz = (jnp.where(feat < A_DH, qt, zero), jnp.where(feat >= A_DH, qt, zero))
    ones_rows = jnp.ones((ACC_ROWS - LANES, blk), BF16)

    m_sc[...] = jnp.full(m_sc.shape, NEG, F32)
    acc_sc[...] = jnp.zeros(acc_sc.shape, F32)

    def scores(kj, s_sc, mx_sc):
        start = pl.multiple_of(kj * blk, blk)
        kt = k_ref[0, pl.ds(start, blk), :]
        bias = bias_ref[0, jnp.minimum(qi - kj, 2)]
        for c in range(2):
            s = jnp.dot(kt, qz[c], preferred_element_type=F32) + bias
            s_sc[c] = s
            mx_sc[c] = jnp.max(s, axis=0, keepdims=True)

    def accumulate(kj, s_sc, mx_sc):
        vext = jnp.concatenate([vt_ref[0, kj], ones_rows], axis=0)
        for c in range(2):
            m_old = m_sc[c]
            m_new = jnp.maximum(m_old, mx_sc[c])
            alpha = jnp.exp2(m_old - m_new)
            p = jnp.exp2(s_sc[c] - m_new).astype(BF16)
            pv = jnp.dot(vext, p, preferred_element_type=F32)
            acc_sc[c] = alpha * acc_sc[c] + pv
            m_sc[c] = m_new

    scores(0, sa_sc, mxa_sc)

    def pair_body(u, carry):
        scores(2 * u + 1, sb_sc, mxb_sc)
        accumulate(2 * u, sa_sc, mxa_sc)
        scores(2 * u + 2, sa_sc, mxa_sc)
        accumulate(2 * u + 1, sb_sc, mxb_sc)
        return carry

    lax.fori_loop(0, qi // 2, pair_body, 0)

    @pl.when(qi % 2 == 1)
    def _():
        scores(qi, sb_sc, mxb_sc)
        accumulate(qi - 1, sa_sc, mxa_sc)
        accumulate(qi, sb_sc, mxb_sc)

    @pl.when(qi % 2 == 0)
    def _():
        accumulate(qi, sa_sc, mxa_sc)

    acc0 = acc_sc[0]
    acc1 = acc_sc[1]
    o0 = acc0[:LANES] * (1.0 / acc0[LANES:LANES + 1])
    o1 = acc1[:LANES] * (1.0 / acc1[LANES:LANES + 1])
    lv = lv_ref[...]
    lam = (jnp.exp(jnp.sum(lv[0:1] * lv[1:2], axis=1, keepdims=True))
           - jnp.exp(jnp.sum(lv[2:3] * lv[3:4], axis=1, keepdims=True)) + lam_init)
    a = o0 - lam * o1
    y = a * lax.rsqrt(jnp.mean(a * a, axis=0, keepdims=True) + EPS) * g_ref[...] * (1.0 - lam_init)
    o_ref[0] = y.T.astype(BF16)


def _attention(aqt, ak, avt, scal, bias_tiles, lam_vec, norm_g):
    batch, seq, _ = ak.shape
    blk = ATTN_BLOCK
    nk = seq // blk
    kernel = functools.partial(_attn_kernel, blk=blk)
    grid_spec = pltpu.PrefetchScalarGridSpec(
        num_scalar_prefetch=1,
        grid=(batch, A_HEADS, nk),
        in_specs=[
            pl.BlockSpec((1, 1, LANES, blk), lambda b, h, i, s: (b, i, h, 0)),
            pl.BlockSpec((1, seq, LANES), lambda b, h, i, s: (b, 0, h)),
            pl.BlockSpec((1, nk, LANES, blk), lambda b, h, i, s: (b, 0, h, 0)),
            pl.BlockSpec((1, 3, blk, blk), lambda b, h, i, s: (h, 0, 0, 0)),
            pl.BlockSpec(lam_vec.shape, lambda b, h, i, s: (0, 0)),
            pl.BlockSpec(norm_g.shape, lambda b, h, i, s: (0, 0)),
        ],
        out_specs=pl.BlockSpec((1, blk, LANES), lambda b, h, i, s: (b, i, h)),
        scratch_shapes=[pltpu.VMEM((2, 1, blk), F32), pltpu.VMEM((2, ACC_ROWS, blk), F32),
                        pltpu.VMEM((2, blk, blk), F32), pltpu.VMEM((2, blk, blk), F32),
                        pltpu.VMEM((2, 1, blk), F32), pltpu.VMEM((2, 1, blk), F32)],
    )
    return pl.pallas_call(
        kernel,
        out_shape=jax.ShapeDtypeStruct(ak.shape, BF16),
        grid_spec=grid_spec,
        compiler_params=_params("parallel", "parallel", "arbitrary"),
        name="diff_attn",
    )(scal, aqt, ak, avt, bias_tiles, lam_vec, norm_g)


def _t5_bucket(n):
    max_exact = N_BUCKETS // 2
    nf = jnp.maximum(n, 1).astype(F32)
    large = max_exact + (jnp.log(nf / max_exact) / math.log(MAX_DIST / max_exact)
                         * (N_BUCKETS - max_exact)).astype(jnp.int32)
    large = jnp.minimum(large, N_BUCKETS - 1)
    return jnp.where(n < max_exact, n, large)


def _bias_tables(rel_bias, seq):
    blk = ATTN_BLOCK
    far = np.arange(blk + 1, max(seq, blk + 2), dtype=np.float64)
    sat = (N_BUCKETS // 2) + np.log(far / (N_BUCKETS // 2)) / math.log(MAX_DIST / (N_BUCKETS // 2)) * (N_BUCKETS // 2)
    assert np.all(sat >= N_BUCKETS - 0.5), "far tiles need a saturated distance bucket"
    dist_bias = rel_bias[_t5_bucket(jnp.arange(seq, dtype=jnp.int32))].T.astype(F32)
    heads = dist_bias.shape[0]
    last = dist_bias[:, seq - 1:seq]
    padded = jnp.concatenate([jnp.full((heads, blk), NEG, F32), dist_bias,
                              jnp.broadcast_to(last, (heads, blk))], axis=1)
    tiles = []
    for d in (0, 1):
        base = blk + d * blk
        w = jnp.concatenate([padded[:, base:base + blk + 1], padded[:, base - blk + 1:base]], axis=1)
        t = jnp.tile(w, (1, blk))[:, :blk * (2 * blk - 1)].reshape(heads, blk, 2 * blk - 1)[:, :, :blk]
        tiles.append(t)
    tiles.append(jnp.broadcast_to(last[:, :, None], (heads, blk, blk)))
    return jnp.stack(tiles, axis=1) * LOG2E


def _mlstm_kernel(mq_ref, mk_ref, mv_ref, mo_ref, grow_ref, gcol_ref, cwq_ref, cwk_ref, cbq_ref, cbk_ref,
                  ng_ref, o_ref, xq_sc, xk_sc, c_sc, n_sc, m_sc, *, chunk, heads):
    L = chunk
    hp = pl.program_id(1)
    ci = pl.program_id(2)

    @pl.when(ci == 0)
    def _():
        xq_sc[:, 0:8, :] = jnp.zeros((heads, 8, LANES), F32)
        xk_sc[:, 0:8, :] = jnp.zeros((heads, 8, LANES), F32)
        c_sc[...] = jnp.zeros(c_sc.shape, F32)
        n_sc[...] = jnp.zeros(n_sc.shape, F32)
        m_sc[...] = jnp.zeros(m_sc.shape, F32)

    def conv_silu(x, sc, w, b):
        sc[8:8 + L, :] = x.astype(F32)
        y = b
        for j in range(CONV_K):
            y = y + sc[5 + j:5 + j + L, :] * w[j:j + 1]
        sc[0:8, :] = sc[L:L + 8, :]
        return y * _sigmoid(y)

    r = lax.broadcasted_iota(jnp.int32, (L, L), 0)
    cidx = lax.broadcasted_iota(jnp.int32, (L, L), 1)
    causal = r >= cidx
    lower = causal.astype(BF16)
    upper = (r <= cidx).astype(BF16)

    g_rows = grow_ref[...]
    cum_rows = sum(jnp.dot(lower, part, preferred_element_type=F32) for part in _split3(_log_sigmoid(g_rows)))
    g_cols = gcol_ref[0]
    cum_cols = sum(jnp.dot(part, upper, preferred_element_type=F32) for part in _split3(_log_sigmoid(g_cols)))

    for hh in range(heads):
        h = hp * heads + hh
        cols = slice(hh * LANES, (hh + 1) * LANES)
        q = conv_silu(mq_ref[0, :, cols], xq_sc.at[hh], cwq_ref[hh], cbq_ref[hh])
        k = conv_silu(mk_ref[0, :, cols], xk_sc.at[hh], cwk_ref[hh], cbk_ref[hh]) * (M_DH ** -0.5)
        v = mv_ref[0, :, cols]

        ig_col = _pick_lane(g_rows, h)
        b_col = _pick_lane(cum_rows, M_HEADS + h)
        ig_row = _pick_row(g_cols, h)
        b_row = _pick_row(cum_cols, M_HEADS + h)
        b_last = b_col[L - 1:L, :]
        m_prev = m_sc[hh]

        dlog = jnp.where(causal, b_col - b_row + ig_row, NEG)
        inter = b_col + m_prev
        m_t = jnp.maximum(inter, jnp.max(dlog, axis=1, keepdims=True))
        qb = q.astype(BF16)
        kb = k.astype(BF16)
        sw = (lax.dot_general(qb, kb, (((1,), (1,)), ((), ())), preferred_element_type=F32)
              * jnp.exp(dlog - m_t))
        dec = jnp.exp(inter - m_t)
        num = (dec * jnp.dot(qb, c_sc[hh].astype(BF16), preferred_element_type=F32)
               + jnp.dot(sw.astype(BF16), v, preferred_element_type=F32))
        den = dec * jnp.sum(q * n_sc[hh], axis=1, keepdims=True) + jnp.sum(sw, axis=1, keepdims=True)
        h_t = num * (1.0 / jnp.maximum(jnp.abs(den), jnp.exp(-m_t)))

        a_col = b_last - b_col + ig_col
        m_new = jnp.maximum(b_last + m_prev, jnp.max(a_col, axis=0, keepdims=True))
        decay = jnp.exp(b_last + m_prev - m_new)
        kw = k * jnp.exp(a_col - m_new)
        c_sc[hh] = decay * c_sc[hh] + lax.dot_general(kw.astype(BF16), v, (((0,), (0,)), ((), ())),
                                                      preferred_element_type=F32)
        n_sc[hh] = decay * n_sc[hh] + jnp.sum(kw, axis=0, keepdims=True)
        m_sc[hh] = m_new

        hm = _sigmoid(mo_ref[0, :, cols].astype(F32)) * h_t
        o_ref[0, :, cols] = _rms(hm, ng_ref[hh]).astype(BF16)


def _mlstm(mq, mk, mv, mo, grow, gcol, conv_w, conv_b, norm_g):
    batch, seq, _ = mq.shape
    L = MLSTM_CHUNK
    hps = MLSTM_HEADS_PER_STEP
    groups = M_HEADS // hps
    kernel = functools.partial(_mlstm_kernel, chunk=L, heads=hps)
    slab = pl.BlockSpec((1, L, hps * LANES), lambda b, h, c: (b, c, h))
    nchunk = seq // L
    return pl.pallas_call(
        kernel,
        out_shape=jax.ShapeDtypeStruct(mq.shape, BF16),
        grid=(batch, groups, nchunk),
        in_specs=[
            slab, slab, slab, slab,
            pl.BlockSpec((L, LANES), lambda b, h, c: (b * nchunk + c, 0)),
            pl.BlockSpec((1, 8, L), lambda b, h, c: (b, 0, c)),
            pl.BlockSpec((hps, CONV_K, LANES), lambda b, h, c: (h, 0, 0)),
            pl.BlockSpec((hps, CONV_K, LANES), lambda b, h, c: (groups + h, 0, 0)),
            pl.BlockSpec((hps, 1, LANES), lambda b, h, c: (h, 0, 0)),
            pl.BlockSpec((hps, 1, LANES), lambda b, h, c: (groups + h, 0, 0)),
            pl.BlockSpec((hps, 1, LANES), lambda b, h, c: (h, 0, 0)),
        ],
        out_specs=slab,
        scratch_shapes=[pltpu.VMEM((hps, L + 8, LANES), F32), pltpu.VMEM((hps, L + 8, LANES), F32),
                        pltpu.VMEM((hps, M_DH, M_DH), F32), pltpu.VMEM((hps, 1, M_DH), F32),
                        pltpu.VMEM((hps, 1, 1), F32)],
        compiler_params=_params("parallel", "parallel", "arbitrary"),
        name="mlstm",
    )(mq, mk, mv, mo, grow, gcol, conv_w, conv_w, conv_b, conv_b, norm_g)


def _outproj_kernel(*refs, moe):
    if moe:
        (a_ref, hm_ref, x_ref, mod_ref, ng_ref, wt_ref, wb_ref, rw_ref,
         x1_ref, h2_ref, gates_ref, pos_ref, post_ref, cnt_ref) = refs
    else:
        a_ref, hm_ref, x_ref, mod_ref, ng_ref, wt_ref, wb_ref, x1_ref, h2_ref = refs
    y = (jnp.dot(a_ref[...], wt_ref[...], preferred_element_type=F32)
         + jnp.dot(hm_ref[...], wb_ref[...], preferred_element_type=F32))
    mod = mod_ref[0]
    ng = ng_ref[...]
    x1 = x_ref[...] + mod[2:3] * _rms(y, ng[1:2])
    x1_ref[...] = x1
    h2 = _rms(x1, ng[2:3]) * (1.0 + mod[4:5]) + mod[3:4]
    h2_ref[...] = h2.astype(BF16)
    if moe:
        h_hi, h_mid, _ = _split3(h2)
        w_hi, w_mid, _ = _split3(rw_ref[...])
        logits = (jnp.dot(h_hi, w_hi, preferred_element_type=F32)
                  + jnp.dot(h_hi, w_mid, preferred_element_type=F32)
                  + jnp.dot(h_mid, w_hi, preferred_element_type=F32))
        lane = lax.broadcasted_iota(jnp.int32, logits.shape, 1)
        lg = jnp.where(lane < N_EXPERTS, logits, NEG)
        v1 = jnp.max(lg, axis=1, keepdims=True)
        i1 = jnp.min(jnp.where(lg == v1, lane, LANES), axis=1, keepdims=True)
        lg2 = jnp.where(lane == i1, NEG, lg)
        v2 = jnp.max(lg2, axis=1, keepdims=True)
        i2 = jnp.min(jnp.where(lg2 == v2, lane, LANES), axis=1, keepdims=True)
        e2 = jnp.exp(v2 - v1)
        w1 = 1.0 / (1.0 + e2)
        gates_ref[...] = jnp.where(lane == i1, w1, 0.0) + jnp.where(lane == i2, e2 * w1, 0.0)
        sel = jnp.logical_or(lane == i1, lane == i2)
        t = MOE_BLOCK
        r = lax.broadcasted_iota(jnp.int32, (t, t), 0)
        cidx = lax.broadcasted_iota(jnp.int32, (t, t), 1)
        lower = (r >= cidx).astype(BF16)
        for blk in range(logits.shape[0] // t):
            sel_b = sel[blk * t:(blk + 1) * t]
            rank = jnp.dot(lower, sel_b.astype(BF16), preferred_element_type=F32)
            pos = jnp.where(sel_b, rank, 0.0)
            pos_ref[blk * t:(blk + 1) * t, :] = pos
            post_ref[blk] = pos.T[:N_EXPERTS]
            cnt_ref[blk] = rank[t - 1:t, :]


def _outproj(a_out, h_m, xf, mod, ng, w_top, w_bot, router_w, batch, seq):
    n, d = xf.shape
    tm = ROW_TILE
    tpb = seq // tm
    moe = router_w is not None
    half = pl.BlockSpec((tm, HEAD_GROUP_W), lambda i: (i, 0))
    full = pl.BlockSpec((tm, d), lambda i: (i, 0))
    in_specs = [half, half, full,
                pl.BlockSpec((1, 6, d), lambda i: (i // tpb, 0, 0)),
                pl.BlockSpec(ng.shape, lambda i: (0, 0)),
                pl.BlockSpec(w_top.shape, lambda i: (0, 0)),
                pl.BlockSpec(w_bot.shape, lambda i: (0, 0))]
    args = [a_out, h_m, xf, mod, ng, w_top, w_bot]
    out_shape = [jax.ShapeDtypeStruct((n, d), F32), jax.ShapeDtypeStruct((n, d), BF16)]
    out_specs = [full, full]
    if moe:
        in_specs.append(pl.BlockSpec(router_w.shape, lambda i: (0, 0)))
        args.append(router_w)
        bpt = tm // MOE_BLOCK
        out_shape += [jax.ShapeDtypeStruct((n, LANES), F32), jax.ShapeDtypeStruct((n, LANES), F32),
                      jax.ShapeDtypeStruct((n // MOE_BLOCK, N_EXPERTS, MOE_BLOCK), F32),
                      jax.ShapeDtypeStruct((n // MOE_BLOCK, 1, LANES), F32)]
        out_specs += [pl.BlockSpec((tm, LANES), lambda i: (i, 0)), pl.BlockSpec((tm, LANES), lambda i: (i, 0)),
                      pl.BlockSpec((bpt, N_EXPERTS, MOE_BLOCK), lambda i: (i, 0, 0)),
                      pl.BlockSpec((bpt, 1, LANES), lambda i: (i, 0, 0))]
    return pl.pallas_call(
        functools.partial(_outproj_kernel, moe=moe),
        out_shape=out_shape,
        grid=(n // tm,),
        in_specs=in_specs,
        out_specs=out_specs,
        compiler_params=_params("parallel"),
        name="out_proj_moe" if moe else "out_proj",
    )(*args)


def _ffn_kernel(h_ref, x_ref, mod_ref, ng_ref, w1_ref, w3_ref, w2_ref, o_ref, acc_sc, *, n_col):
    f = pl.program_id(1)

    @pl.when(f == 0)
    def _():
        acc_sc[...] = jnp.zeros(acc_sc.shape, F32)

    hb = h_ref[...]
    a = jnp.dot(hb, w1_ref[...], preferred_element_type=F32)
    b = jnp.dot(hb, w3_ref[...], preferred_element_type=F32)
    hh = a * _sigmoid(a) * b
    acc_sc[...] += jnp.dot(hh.astype(BF16), w2_ref[...], preferred_element_type=F32)

    @pl.when(f == n_col - 1)
    def _():
        mod = mod_ref[0]
        o_ref[...] = x_ref[...] + mod[5:6] * _rms(acc_sc[...], ng_ref[3:4, :])


def _ffn(h2, x1, mod, ng, w1, w3, w2, batch, seq):
    n, d = x1.shape
    dff = w1.shape[1]
    tm, tf = FFN_ROW_TILE, FFN_COL_TILE
    tpb = seq // tm
    n_col = dff // tf
    row = lambda i, f: (i, 0)
    return pl.pallas_call(
        functools.partial(_ffn_kernel, n_col=n_col),
        out_shape=jax.ShapeDtypeStruct((n, d), F32),
        grid=(n // tm, n_col),
        in_specs=[pl.BlockSpec((tm, d), row), pl.BlockSpec((tm, d), row),
                  pl.BlockSpec((1, 6, d), lambda i, f: (i // tpb, 0, 0)),
                  pl.BlockSpec(ng.shape, lambda i, f: (0, 0)),
                  pl.BlockSpec((d, tf), lambda i, f: (0, f)), pl.BlockSpec((d, tf), lambda i, f: (0, f)),
                  pl.BlockSpec((tf, d), lambda i, f: (f, 0))],
        out_specs=pl.BlockSpec((tm, d), row),
        scratch_shapes=[pltpu.VMEM((tm, d), F32)],
        compiler_params=_params("parallel", "arbitrary"),
        name="dense_ffn",
    )(h2, x1, mod, ng, w1, w3, w2)


def _moe_tables(cnt, batch, seq):
    t = MOE_BLOCK
    e_n = N_EXPERTS
    nbb = seq // t
    ntb = 2 * nbb + e_n
    c = cnt.reshape(batch, nbb, e_n)
    cum = jnp.cumsum(c, axis=1) - c
    tot = jnp.sum(c, axis=1)
    ntile = (tot + t - 1) // t
    tile_end = jnp.cumsum(ntile, axis=1)
    tile_base = tile_end - ntile
    tl = jnp.arange(ntb, dtype=jnp.int32)
    e_t = jnp.sum((tl[None, :, None] >= tile_end[:, None, :]).astype(jnp.int32), axis=2)
    valid = e_t < e_n
    e_c = jnp.minimum(e_t, e_n - 1)
    s0 = (tl[None, :] - jnp.take_along_axis(tile_base, e_c, axis=1)) * t
    bidx = jnp.arange(batch)[:, None]
    cum_sel = cum.transpose(0, 2, 1)[bidx, e_c]
    end_sel = cum_sel + c.transpose(0, 2, 1)[bidx, e_c]
    lo = jnp.sum((end_sel <= s0[..., None]).astype(jnp.int32), axis=2)
    hi = jnp.sum((cum_sel < (s0 + t)[..., None]).astype(jnp.int32), axis=2)

    def flat(a, tail):
        return jnp.concatenate([a.reshape(-1).astype(jnp.int32), jnp.full((1,), tail, jnp.int32)])

    tile_tables = (flat(jnp.broadcast_to(bidx, (batch, ntb)), batch - 1), flat(e_c, e_n - 1), flat(valid, 0),
                   flat(lo, 0), flat(hi, 0), flat(s0, 0), cum.reshape(-1).astype(jnp.int32))
    row0 = (bidx[:, :, None] * ntb + tile_base[:, None, :]) * t + cum
    seg_tables = ((row0 // t).reshape(-1).astype(jnp.int32), (row0 % t).reshape(-1).astype(jnp.int32))
    return tile_tables, seg_tables


def _moe_expert_kernel(tb_ref, te_ref, tv_ref, lo_ref, hi_ref, s0_ref, cum_ref,
                       h_ref, post_ref, w1_ref, w3_ref, w2_ref, o_ref, xg_sc, *, n_col, tf, nbb):
    ti = pl.program_id(0)
    t = MOE_BLOCK

    @pl.when(tv_ref[ti] == 0)
    def _():
        o_ref[...] = jnp.zeros(o_ref.shape, o_ref.dtype)

    @pl.when(tv_ref[ti] == 1)
    def _():
        b = tb_ref[ti]
        e = te_ref[ti]
        s0 = s0_ref[ti]
        xg_sc[...] = jnp.zeros(xg_sc.shape, F32)
        slot = lax.broadcasted_iota(jnp.int32, (t, t), 0).astype(F32)

        def gather(j, carry):
            blk = b * nbb + j
            rank = post_ref[blk, pl.ds(e, 1), :]
            shift = (cum_ref[blk * N_EXPERTS + e] - s0 - 1).astype(F32)
            onehot = jnp.logical_and(rank > 0.0, rank + shift == slot).astype(BF16)
            rows = h_ref[0, pl.ds(pl.multiple_of(j * t, t), t), :]
            xg_sc[...] += jnp.dot(onehot, rows, preferred_element_type=F32)
            return carry

        lax.fori_loop(lo_ref[ti], hi_ref[ti], gather, 0)
        x = xg_sc[...].astype(BF16)
        acc = jnp.zeros(xg_sc.shape, F32)
        for f in range(n_col):
            a = jnp.dot(x, w1_ref[0, :, f * tf:(f + 1) * tf], preferred_element_type=F32)
            g = jnp.dot(x, w3_ref[0, :, f * tf:(f + 1) * tf], preferred_element_type=F32)
            hh = (a * _sigmoid(a) * g).astype(BF16)
            acc = acc + jnp.dot(hh, w2_ref[0, f * tf:(f + 1) * tf, :], preferred_element_type=F32)
        o_ref[...] = acc.astype(BF16)


def _moe_experts(h2, post, tables, w1, w3, w2, batch, seq):
    n, d = h2.shape
    t = MOE_BLOCK
    nbb = seq // t
    n_tiles = tables[0].shape[0]
    dff = w1.shape[2]
    tf = FFN_COL_TILE
    once = pl.Buffered(1)
    grid_spec = pltpu.PrefetchScalarGridSpec(
        num_scalar_prefetch=len(tables),
        grid=(n_tiles,),
        in_specs=[
            pl.BlockSpec((1, seq, d), lambda i, tb, te, *_: (tb[i], 0, 0), pipeline_mode=once),
            pl.BlockSpec(post.shape, lambda i, *_: (0, 0, 0)),
            pl.BlockSpec((1, d, dff), lambda i, tb, te, *_: (te[i], 0, 0), pipeline_mode=once),
            pl.BlockSpec((1, d, dff), lambda i, tb, te, *_: (te[i], 0, 0), pipeline_mode=once),
            pl.BlockSpec((1, dff, d), lambda i, tb, te, *_: (te[i], 0, 0), pipeline_mode=once),
        ],
        out_specs=pl.BlockSpec((t, d), lambda i, *_: (i, 0)),
        scratch_shapes=[pltpu.VMEM((t, d), F32)],
    )
    return pl.pallas_call(
        functools.partial(_moe_expert_kernel, n_col=dff // tf, tf=tf, nbb=nbb),
        out_shape=jax.ShapeDtypeStruct((n_tiles * t, d), BF16),
        grid_spec=grid_spec,
        compiler_params=_params("arbitrary"),
        name="moe_experts",
    )(*tables, h2.reshape(batch, seq, d), post, w1, w3, w2)


def _moe_combine_kernel(st_ref, so_ref, *refs):
    y_refs = refs[:2 * N_EXPERTS]
    pos_ref, gates_ref, x_ref, mod_ref, ng_ref, o_ref = refs[2 * N_EXPERTS:]
    i = pl.program_id(0)
    t = MOE_BLOCK
    pos = pos_ref[...]
    gates = gates_ref[...]
    slot = lax.broadcasted_iota(jnp.int32, (t, 2 * t), 1).astype(F32)
    acc = jnp.zeros(x_ref.shape, F32)
    for e in range(N_EXPERTS):
        rank = pos[:, e:e + 1]
        off = so_ref[i * N_EXPERTS + e].astype(F32)
        onehot = jnp.logical_and(rank > 0.0, rank - 1.0 + off == slot).astype(BF16)
        ys = jnp.concatenate([y_refs[2 * e][...], y_refs[2 * e + 1][...]], axis=0)
        acc = acc + gates[:, e:e + 1] * jnp.dot(onehot, ys, preferred_element_type=F32)
    mod = mod_ref[0]
    o_ref[...] = x_ref[...] + mod[5:6] * _rms(acc, ng_ref[3:4, :])


def _moe_combine(ys, pos, gates, x1, mod, ng, seg_tables, batch, seq):
    n, d = x1.shape
    t = MOE_BLOCK
    nbb = seq // t
    y_specs = []
    for e in range(N_EXPERTS):
        for nxt in range(2):
            y_specs.append(pl.BlockSpec((t, d), functools.partial(
                lambda i, st, so, e, nxt: (st[i * N_EXPERTS + e] + nxt, 0), e=e, nxt=nxt)))
    grid_spec = pltpu.PrefetchScalarGridSpec(
        num_scalar_prefetch=2,
        grid=(n // t,),
        in_specs=y_specs + [
            pl.BlockSpec((t, LANES), lambda i, st, so: (i, 0)),
            pl.BlockSpec((t, LANES), lambda i, st, so: (i, 0)),
            pl.BlockSpec((t, d), lambda i, st, so: (i, 0)),
            pl.BlockSpec((1, 6, d), lambda i, st, so: (i // nbb, 0, 0)),
            pl.BlockSpec(ng.shape, lambda i, st, so: (0, 0)),
        ],
        out_specs=pl.BlockSpec((t, d), lambda i, st, so: (i, 0)),
    )
    return pl.pallas_call(
        _moe_combine_kernel,
        out_shape=jax.ShapeDtypeStruct((n, d), F32),
        grid_spec=grid_spec,
        compiler_params=_params("parallel"),
        name="moe_combine",
    )(*seg_tables, *([ys] * (2 * N_EXPERTS)), pos, gates, x1, mod, ng)


@jax.jit
def _forward(x, c, rel_bias, ada_w, ada_b, norm_g, w_in, gate_b, conv_w, conv_b, lam_vec,
             attn_norm_g, mlstm_norm_g, w_out, ffn_w1, ffn_w3, ffn_w2, router_w, moe_w1, moe_w3, moe_w2):
    batch, seq, d = x.shape
    depth = ada_w.shape[0]
    n = batch * seq
    n_main = N_SLABS * HEAD_GROUP_W
    assert seq % ATTN_BLOCK == 0 and seq % FFN_ROW_TILE == 0 and seq % MLSTM_CHUNK == 0
    assert ROW_TILE % MOE_BLOCK == 0
    assert seq >= 2 * ATTN_BLOCK and ROW_TILE == ATTN_BLOCK
    assert w_in.shape[2] == n_main + 2 * M_HEADS

    mod_all = _ada(c, ada_w, ada_b)
    bias_tiles = _bias_tables(rel_bias, seq)

    xf = x.reshape(n, d)
    for i in range(depth):
        mod = mod_all[i]
        ng = norm_g[i]
        lam_init = 0.8 - 0.6 * math.exp(-0.3 * i)

        w_gate = w_in[i][:, n_main:]
        w_bf = w_in[i][:, :n_main].astype(BF16)
        gw = HEAD_GROUP_W
        slabs = _inproj(
            xf, mod, ng[0:1], jnp.concatenate([w_bf[:, gw:2 * gw], w_bf[:, 3 * gw:]], axis=1),
            w_bf[:, :gw].T, w_bf[:, 2 * gw:3 * gw].T,
            jnp.zeros((d, LANES), BF16).at[:, :2 * M_HEADS].set(w_gate.astype(BF16)),
            w_gate.T.astype(BF16),
            jnp.zeros((1, LANES), F32).at[0, :2 * M_HEADS].set(gate_b[i].reshape(-1)),
            gate_b[i].reshape(2 * M_HEADS, 1), batch, seq)
        ak, mq, mk, mv, mo = (s.reshape(batch, seq, HEAD_GROUP_W) for s in slabs[:N_SLABS - 2])
        aqt, avt, grow, gcol = slabs[N_SLABS - 2:]

        scal = jnp.full((1,), lam_init, F32)
        a_out = _attention(aqt, ak, avt, scal, bias_tiles, lam_vec[i], attn_norm_g[i].reshape(LANES, 1))

        h_m = _mlstm(mq, mk, mv, mo, grow, gcol,
                     conv_w[i].reshape(CONV_K, 2 * M_HEADS, LANES).transpose(1, 0, 2),
                     conv_b[i].reshape(2 * M_HEADS, 1, LANES),
                     mlstm_norm_g[i].reshape(M_HEADS, 1, LANES))

        w_o = w_out[i].astype(BF16)
        j = i // 2
        if i % 2 == 0:
            x1, h2 = _outproj(a_out.reshape(n, -1), h_m.reshape(n, -1), xf, mod, ng,
                              w_o[:HEAD_GROUP_W], w_o[HEAD_GROUP_W:], None, batch, seq)
            xf = _ffn(h2, x1, mod, ng, ffn_w1[j].astype(BF16), ffn_w3[j].astype(BF16),
                      ffn_w2[j].astype(BF16), batch, seq)
        else:
            rw = jnp.zeros((d, LANES), F32).at[:, :N_EXPERTS].set(router_w[j])
            x1, h2, gates, pos, post, cnt = _outproj(a_out.reshape(n, -1), h_m.reshape(n, -1), xf, mod, ng,
                                                     w_o[:HEAD_GROUP_W], w_o[HEAD_GROUP_W:], rw, batch, seq)
            tile_tables, seg_tables = _moe_tables(cnt[:, 0, :N_EXPERTS].astype(jnp.int32), batch, seq)
            ys = _moe_experts(h2, post, tile_tables, moe_w1[j].astype(BF16), moe_w3[j].astype(BF16),
                              moe_w2[j].astype(BF16), batch, seq)
            xf = _moe_combine(ys, pos, gates, x1, mod, ng, seg_tables, batch, seq)
    return xf.reshape(batch, seq, d)


def kernel(x, c, rel_bias, ada_w, ada_b, norm_g, w_in, gate_b, conv_w, conv_b, lam_vec, attn_norm_g,
           mlstm_norm_g, w_out, ffn_w1, ffn_w3, ffn_w2, router_w, moe_w1, moe_w3, moe_w2):
    return _forward(x, c, rel_bias, ada_w, ada_b, norm_g, w_in, gate_b, conv_w, conv_b, lam_vec,
                    attn_norm_g, mlstm_norm_g, w_out, ffn_w1, ffn_w3, ffn_w2, router_w, moe_w1, moe_w3, moe_w2)
```

```python
import functools
import math

import numpy as np
import jax
import jax.numpy as jnp
from jax import lax
from jax.experimental import pallas as pl
from jax.experimental.pallas import tpu as pltpu

F32 = jnp.float32
BF16 = jnp.bfloat16
HIGHEST = lax.Precision.HIGHEST

A_HEADS = 4
A_DH = 64
M_HEADS = 4
M_DH = 128
CONV_K = 4
N_BUCKETS = 32
MAX_DIST = 128
N_EXPERTS = 8
EPS = 1e-6

LANES = 128
HEAD_GROUP_W = 512
N_SLABS = 7
NEG = -1e30
LOG2E = math.log2(math.e)
Q_SCALE = A_DH ** -0.5 * LOG2E
VMEM_LIMIT_BYTES = 56 * 1024 * 1024

ATTN_BLOCK = 512
MLSTM_CHUNK = 256
MLSTM_HEADS_PER_STEP = 4
ROW_TILE = 512
FFN_ROW_TILE = 512
FFN_COL_TILE = 1792
MOE_COL_TILE = 512
ADA_COL_TILE = 1536
MOE_BLOCK = 256


def _params(*sem, flags=None):
    return pltpu.CompilerParams(dimension_semantics=sem, vmem_limit_bytes=VMEM_LIMIT_BYTES, flags=flags)


def _rms(x, g):
    return x * lax.rsqrt(jnp.mean(x * x, axis=-1, keepdims=True) + EPS) * g


def _sigmoid(x):
    return 1.0 / (1.0 + jnp.exp(-x))


def _log_sigmoid(x):
    return jnp.minimum(x, 0.0) - jnp.log(1.0 + jnp.exp(-jnp.abs(x)))


def _split3(x):
    hi = x.astype(BF16)
    r1 = x - hi.astype(F32)
    mid = r1.astype(BF16)
    lo = (r1 - mid.astype(F32)).astype(BF16)
    return hi, mid, lo


def _pick_lane(x, idx):
    lane = lax.broadcasted_iota(jnp.int32, x.shape, 1)
    return jnp.sum(jnp.where(lane == idx, x, 0.0), axis=1, keepdims=True)


def _pick_row(x, idx):
    row = lax.broadcasted_iota(jnp.int32, x.shape, 0)
    return jnp.sum(jnp.where(row == idx, x, 0.0), axis=0, keepdims=True)


def _ada_kernel(c_ref, w_ref, b_ref, o_ref):
    c = c_ref[...]
    ca = c * _sigmoid(c)
    o_ref[0] = jnp.dot(ca, w_ref[0], precision=HIGHEST, preferred_element_type=F32) + b_ref[0]


def _ada(c, ada_w, ada_b):
    depth, d, d6 = ada_w.shape
    b = c.shape[0]
    rows = 8
    cp = jnp.zeros((rows, d), F32).at[:b].set(c)
    out = pl.pallas_call(
        _ada_kernel,
        out_shape=jax.ShapeDtypeStruct((depth, rows, d6), F32),
        grid=(depth, d6 // ADA_COL_TILE),
        in_specs=[
            pl.BlockSpec((rows, d), lambda l, j: (0, 0)),
            pl.BlockSpec((1, d, ADA_COL_TILE), lambda l, j: (l, 0, j)),
            pl.BlockSpec((1, 1, ADA_COL_TILE), lambda l, j: (l, 0, j)),
        ],
        out_specs=pl.BlockSpec((1, rows, ADA_COL_TILE), lambda l, j: (l, 0, j)),
        compiler_params=_params("parallel", "parallel"),
        name="ada_mod",
    )(cp, ada_w, ada_b.reshape(depth, 1, d6))
    return out[:, :b].reshape(depth, b, 6, d)


def _inproj_kernel(x_ref, mod_ref, g_ref, w_ref, wqt_ref, wvt_ref, wg_ref, wgt_ref, gb_ref, gbt_ref,
                   ak, mq, mk, mv, mo, aqt, avt, grow, gcol):
    mod = mod_ref[0]
    h = _rms(x_ref[...], g_ref[...]) * (1.0 + mod[1:2]) + mod[0:1]
    hb = h.astype(BF16)
    for s, o in enumerate((ak, mq, mk, mv, mo)):
        w = w_ref[:, s * HEAD_GROUP_W:(s + 1) * HEAD_GROUP_W]
        o[...] = jnp.dot(hb, w, preferred_element_type=F32).astype(BF16)
    nt = (((1,), (1,)), ((), ()))
    aqt[0, 0] = (lax.dot_general(wqt_ref[...], hb, nt, preferred_element_type=F32) * Q_SCALE).astype(BF16)
    avt[0, 0] = lax.dot_general(wvt_ref[...], hb, nt, preferred_element_type=F32).astype(BF16)
    grow[...] = jnp.dot(hb, wg_ref[...], preferred_element_type=F32) + gb_ref[...]
    gcol[0] = lax.dot_general(wgt_ref[...], hb, (((1,), (1,)), ((), ())),
                              preferred_element_type=F32) + gbt_ref[...]


def _inproj(xf, mod, g, w_main, w_qt, w_vt, wg, wgt, gb, gbt, batch, seq):
    n, d = xf.shape
    tm = ROW_TILE
    tpb = seq // tm
    n_row_slabs = N_SLABS - 2
    slab = jax.ShapeDtypeStruct((n, HEAD_GROUP_W), BF16)
    slab_t = jax.ShapeDtypeStruct((batch, tpb, HEAD_GROUP_W, tm), BF16)
    row_spec = pl.BlockSpec((tm, HEAD_GROUP_W), lambda i: (i, 0))
    t_spec = pl.BlockSpec((1, 1, HEAD_GROUP_W, tm), lambda i: (i // tpb, i % tpb, 0, 0))
    const = lambda i: (0, 0)
    return pl.pallas_call(
        _inproj_kernel,
        out_shape=[slab] * n_row_slabs + [slab_t, slab_t, jax.ShapeDtypeStruct((n, LANES), F32),
                                          jax.ShapeDtypeStruct((batch, 8, seq), F32)],
        grid=(n // tm,),
        in_specs=[
            pl.BlockSpec((tm, d), lambda i: (i, 0)),
            pl.BlockSpec((1, 6, d), lambda i: (i // tpb, 0, 0)),
            pl.BlockSpec((1, d), const),
            pl.BlockSpec(w_main.shape, const),
            pl.BlockSpec(w_qt.shape, const),
            pl.BlockSpec(w_vt.shape, const),
            pl.BlockSpec(wg.shape, const),
            pl.BlockSpec(wgt.shape, const),
            pl.BlockSpec(gb.shape, const),
            pl.BlockSpec(gbt.shape, const),
        ],
        out_specs=[row_spec] * n_row_slabs + [t_spec, t_spec,
                                              pl.BlockSpec((tm, LANES), lambda i: (i, 0)),
                                              pl.BlockSpec((1, 8, tm), lambda i: (i // tpb, 0, i % tpb))],
        compiler_params=_params("parallel"),
        name="in_proj",
    )(xf, mod, g, w_main, w_qt, w_vt, wg, wgt, gb, gbt)


ACC_ROWS = LANES + 16


def _attn_kernel(scal_ref, qt_ref, k_ref, vt_ref, bias_ref, lv_ref, g_ref, o_ref, m_sc, acc_sc,
                 sa_sc, sb_sc, mxa_sc, mxb_sc, *, blk):
    qi = pl.program_id(2)
    lam_init = scal_ref[0]

    feat = lax.broadcasted_iota(jnp.int32, (LANES, blk), 0)
    qt = qt_ref[0, 0]
    zero = jnp.zeros_like(qt)
    qz = (jnp.where(feat < A_DH, qt, zero), jnp.where(feat >= A_DH, qt, zero))
    ones_rows = jnp.ones((ACC_ROWS - LANES, blk), BF16)

    m_sc[...] = jnp.full(m_sc.shape, NEG, F32)
    acc_sc[...] = jnp.zeros(acc_sc.shape, F32)

    def scores(kj, s_sc, mx_sc, near=None):
        start = pl.multiple_of(kj * blk, blk)
        kt = k_ref[0, pl.ds(start, blk), :]
        for c in range(2):
            s = jnp.dot(kt, qz[c], preferred_element_type=F32)
            if near is not None:
                s = s + bias_ref[0, near]
            s_sc[c] = s
            mx_sc[c] = jnp.max(s, axis=0, keepdims=True)

    def accumulate(kj, s_sc, mx_sc):
        vext = jnp.concatenate([vt_ref[0, kj], ones_rows], axis=0)
        for c in range(2):
            m_old = m_sc[c]
            m_new = jnp.maximum(m_old, mx_sc[c])
            alpha = jnp.exp2(m_old - m_new)
            p = jnp.exp2(s_sc[c] - m_new).astype(BF16)
            pv = jnp.dot(vext, p, preferred_element_type=F32)
            acc_sc[c] = alpha * acc_sc[c] + pv
            m_sc[c] = m_new

    n_far = jnp.maximum(qi - 1, 0)
    scores(qi, sa_sc, mxa_sc, near=0)

    @pl.when(qi == 0)
    def _():
        accumulate(qi, sa_sc, mxa_sc)

    @pl.when(qi == 1)
    def _():
        scores(qi - 1, sb_sc, mxb_sc, near=1)
        accumulate(qi, sa_sc, mxa_sc)
        accumulate(qi - 1, sb_sc, mxb_sc)

    @pl.when(qi >= 2)
    def _():
        scores(qi - 1, sb_sc, mxb_sc, near=1)
        accumulate(qi, sa_sc, mxa_sc)
        scores(0, sa_sc, mxa_sc)
        accumulate(qi - 1, sb_sc, mxb_sc)

    def pair_body(u, carry):
        scores(2 * u + 1, sb_sc, mxb_sc)
        accumulate(2 * u, sa_sc, mxa_sc)
        scores(jnp.minimum(2 * u + 2, n_far - 1), sa_sc, mxa_sc)
        accumulate(2 * u + 1, sb_sc, mxb_sc)
        return carry

    lax.fori_loop(0, n_far // 2, pair_body, 0)

    @pl.when(n_far % 2 == 1)
    def _():
        accumulate(n_far - 1, sa_sc, mxa_sc)

    acc0 = acc_sc[0]
    acc1 = acc_sc[1]
    o0 = acc0[:LANES] * (1.0 / acc0[LANES:LANES + 1])
    o1 = acc1[:LANES] * (1.0 / acc1[LANES:LANES + 1])
    lv = lv_ref[...]
    lam = (jnp.exp(jnp.sum(lv[0:1] * lv[1:2], axis=1, keepdims=True))
           - jnp.exp(jnp.sum(lv[2:3] * lv[3:4], axis=1, keepdims=True)) + lam_init)
    a = o0 - lam * o1
    y = a * lax.rsqrt(jnp.mean(a * a, axis=0, keepdims=True) + EPS) * g_ref[...] * (1.0 - lam_init)
    o_ref[0] = y.T.astype(BF16)


def _attention(aqt, ak, avt, scal, bias_tiles, lam_vec, norm_g):
    batch, seq, _ = ak.shape
    blk = ATTN_BLOCK
    nk = seq // blk
    kernel = functools.partial(_attn_kernel, blk=blk)
    grid_spec = pltpu.PrefetchScalarGridSpec(
        num_scalar_prefetch=1,
        grid=(batch, A_HEADS, nk),
        in_specs=[
            pl.BlockSpec((1, 1, LANES, blk), lambda b, h, i, s: (b, i, h, 0)),
            pl.BlockSpec((1, seq, LANES), lambda b, h, i, s: (b, 0, h)),
            pl.BlockSpec((1, nk, LANES, blk), lambda b, h, i, s: (b, 0, h, 0)),
            pl.BlockSpec((1, 2, blk, blk), lambda b, h, i, s: (h, 0, 0, 0)),
            pl.BlockSpec(lam_vec.shape, lambda b, h, i, s: (0, 0)),
            pl.BlockSpec(norm_g.shape, lambda b, h, i, s: (0, 0)),
        ],
        out_specs=pl.BlockSpec((1, blk, LANES), lambda b, h, i, s: (b, i, h)),
        scratch_shapes=[pltpu.VMEM((2, 1, blk), F32), pltpu.VMEM((2, ACC_ROWS, blk), F32),
                        pltpu.VMEM((2, blk, blk), F32), pltpu.VMEM((2, blk, blk), F32),
                        pltpu.VMEM((2, 1, blk), F32), pltpu.VMEM((2, 1, blk), F32)],
    )
    return pl.pallas_call(
        kernel,
        out_shape=jax.ShapeDtypeStruct(ak.shape, BF16),
        grid_spec=grid_spec,
        compiler_params=_params("parallel", "parallel", "arbitrary"),
        name="diff_attn",
    )(scal, aqt, ak, avt, bias_tiles, lam_vec, norm_g)


def _t5_bucket(n):
    max_exact = N_BUCKETS // 2
    nf = jnp.maximum(n, 1).astype(F32)
    large = max_exact + (jnp.log(nf / max_exact) / math.log(MAX_DIST / max_exact)
                         * (N_BUCKETS - max_exact)).astype(jnp.int32)
    large = jnp.minimum(large, N_BUCKETS - 1)
    return jnp.where(n < max_exact, n, large)


def _bias_tables(rel_bias, seq):
    blk = ATTN_BLOCK
    far = np.arange(blk + 1, max(seq, blk + 2), dtype=np.float64)
    sat = (N_BUCKETS // 2) + np.log(far / (N_BUCKETS // 2)) / math.log(MAX_DIST / (N_BUCKETS // 2)) * (N_BUCKETS // 2)
    assert np.all(sat >= N_BUCKETS - 0.5), "far tiles need a saturated distance bucket"
    dist_bias = rel_bias[_t5_bucket(jnp.arange(seq, dtype=jnp.int32))].T.astype(F32)
    heads = dist_bias.shape[0]
    last = dist_bias[:, seq - 1:seq]
    padded = jnp.concatenate([jnp.full((heads, blk), NEG, F32), dist_bias,
                              jnp.broadcast_to(last, (heads, blk))], axis=1)
    tiles = []
    for d in (0, 1):
        base = blk + d * blk
        w = jnp.concatenate([padded[:, base:base + blk + 1], padded[:, base - blk + 1:base]], axis=1)
        t = jnp.tile(w, (1, blk))[:, :blk * (2 * blk - 1)].reshape(heads, blk, 2 * blk - 1)[:, :, :blk]
        tiles.append(t - last[:, :, None])
    return jnp.stack(tiles, axis=1) * LOG2E


def _mlstm_kernel(mq_ref, mk_ref, mv_ref, mo_ref, grow_ref, gcol_ref, cwq_ref, cwk_ref, cbq_ref, cbk_ref,
                  ng_ref, o_ref, xq_sc, xk_sc, c_sc, n_sc, m_sc, *, chunk, heads):
    L = chunk
    hp = pl.program_id(1)
    ci = pl.program_id(2)

    @pl.when(ci == 0)
    def _():
        xq_sc[:, 0:8, :] = jnp.zeros((heads, 8, LANES), F32)
        xk_sc[:, 0:8, :] = jnp.zeros((heads, 8, LANES), F32)
        c_sc[...] = jnp.zeros(c_sc.shape, F32)
        n_sc[...] = jnp.zeros(n_sc.shape, F32)
        m_sc[...] = jnp.zeros(m_sc.shape, F32)

    def conv_silu(x, sc, w, b):
        sc[8:8 + L, :] = x.astype(F32)
        y = b
        for j in range(CONV_K):
            y = y + sc[5 + j:5 + j + L, :] * w[j:j + 1]
        sc[0:8, :] = sc[L:L + 8, :]
        return y * _sigmoid(y)

    r = lax.broadcasted_iota(jnp.int32, (L, L), 0)
    cidx = lax.broadcasted_iota(jnp.int32, (L, L), 1)
    causal = r >= cidx
    lower = causal.astype(BF16)
    upper = (r <= cidx).astype(BF16)

    g_rows = grow_ref[...]
    cum_rows = sum(jnp.dot(lower, part, preferred_element_type=F32) for part in _split3(_log_sigmoid(g_rows)))
    g_cols = gcol_ref[0]
    cum_cols = sum(jnp.dot(part, upper, preferred_element_type=F32) for part in _split3(_log_sigmoid(g_cols)))

    for hh in range(heads):
        h = hp * heads + hh
        cols = slice(hh * LANES, (hh + 1) * LANES)
        q = conv_silu(mq_ref[0, :, cols], xq_sc.at[hh], cwq_ref[hh], cbq_ref[hh])
        k = conv_silu(mk_ref[0, :, cols], xk_sc.at[hh], cwk_ref[hh], cbk_ref[hh]) * (M_DH ** -0.5)
        v = mv_ref[0, :, cols]

        ig_col = _pick_lane(g_rows, h)
        b_col = _pick_lane(cum_rows, M_HEADS + h)
        ig_row = _pick_row(g_cols, h)
        b_row = _pick_row(cum_cols, M_HEADS + h)
        b_last = b_col[L - 1:L, :]
        m_prev = m_sc[hh]

        dlog = jnp.where(causal, b_col - b_row + ig_row, NEG)
        inter = b_col + m_prev
        m_t = jnp.maximum(inter, jnp.max(dlog, axis=1, keepdims=True))
        qb = q.astype(BF16)
        kb = k.astype(BF16)
        sw = (lax.dot_general(qb, kb, (((1,), (1,)), ((), ())), preferred_element_type=F32)
              * jnp.exp(dlog - m_t))
        dec = jnp.exp(inter - m_t)
        num = (dec * jnp.dot(qb, c_sc[hh].astype(BF16), preferred_element_type=F32)
               + jnp.dot(sw.astype(BF16), v, preferred_element_type=F32))
        den = dec * jnp.sum(q * n_sc[hh], axis=1, keepdims=True) + jnp.sum(sw, axis=1, keepdims=True)
        h_t = num * (1.0 / jnp.maximum(jnp.abs(den), jnp.exp(-m_t)))

        a_col = b_last - b_col + ig_col
        m_new = jnp.maximum(b_last + m_prev, jnp.max(a_col, axis=0, keepdims=True))
        decay = jnp.exp(b_last + m_prev - m_new)
        kw = k * jnp.exp(a_col - m_new)
        c_sc[hh] = decay * c_sc[hh] + lax.dot_general(kw.astype(BF16), v, (((0,), (0,)), ((), ())),
                                                      preferred_element_type=F32)
        n_sc[hh] = decay * n_sc[hh] + jnp.sum(kw, axis=0, keepdims=True)
        m_sc[hh] = m_new

        hm = _sigmoid(mo_ref[0, :, cols].astype(F32)) * h_t
        o_ref[0, :, cols] = _rms(hm, ng_ref[hh]).astype(BF16)


def _mlstm(mq, mk, mv, mo, grow, gcol, conv_w, conv_b, norm_g):
    batch, seq, _ = mq.shape
    L = MLSTM_CHUNK
    hps = MLSTM_HEADS_PER_STEP
    groups = M_HEADS // hps
    kernel = functools.partial(_mlstm_kernel, chunk=L, heads=hps)
    slab = pl.BlockSpec((1, L, hps * LANES), lambda b, h, c: (b, c, h))
    nchunk = seq // L
    return pl.pallas_call(
        kernel,
        out_shape=jax.ShapeDtypeStruct(mq.shape, BF16),
        grid=(batch, groups, nchunk),
        in_specs=[
            slab, slab, slab, slab,
            pl.BlockSpec((L, LANES), lambda b, h, c: (b * nchunk + c, 0)),
            pl.BlockSpec((1, 8, L), lambda b, h, c: (b, 0, c)),
            pl.BlockSpec((hps, CONV_K, LANES), lambda b, h, c: (h, 0, 0)),
            pl.BlockSpec((hps, CONV_K, LANES), lambda b, h, c: (groups + h, 0, 0)),
            pl.BlockSpec((hps, 1, LANES), lambda b, h, c: (h, 0, 0)),
            pl.BlockSpec((hps, 1, LANES), lambda b, h, c: (groups + h, 0, 0)),
            pl.BlockSpec((hps, 1, LANES), lambda b, h, c: (h, 0, 0)),
        ],
        out_specs=slab,
        scratch_shapes=[pltpu.VMEM((hps, L + 8, LANES), F32), pltpu.VMEM((hps, L + 8, LANES), F32),
                        pltpu.VMEM((hps, M_DH, M_DH), F32), pltpu.VMEM((hps, 1, M_DH), F32),
                        pltpu.VMEM((hps, 1, 1), F32)],
        compiler_params=_params("parallel", "parallel", "arbitrary"),
        name="mlstm",
    )(mq, mk, mv, mo, grow, gcol, conv_w, conv_w, conv_b, conv_b, norm_g)


def _outproj_kernel(*refs, moe):
    if moe:
        (a_ref, hm_ref, x_ref, mod_ref, ng_ref, wt_ref, wb_ref, rw_ref,
         x1_ref, h2_ref, gates_ref, pos_ref, post_ref, cnt_ref) = refs
    else:
        a_ref, hm_ref, x_ref, mod_ref, ng_ref, wt_ref, wb_ref, x1_ref, h2_ref = refs
    y = (jnp.dot(a_ref[...], wt_ref[...], preferred_element_type=F32)
         + jnp.dot(hm_ref[...], wb_ref[...], preferred_element_type=F32))
    mod = mod_ref[0]
    ng = ng_ref[...]
    x1 = x_ref[...] + mod[2:3] * _rms(y, ng[1:2])
    x1_ref[...] = x1
    h2 = _rms(x1, ng[2:3]) * (1.0 + mod[4:5]) + mod[3:4]
    h2_ref[...] = h2.astype(BF16)
    if moe:
        h_hi, h_mid, _ = _split3(h2)
        w_hi, w_mid, _ = _split3(rw_ref[...])
        logits = (jnp.dot(h_hi, w_hi, preferred_element_type=F32)
                  + jnp.dot(h_hi, w_mid, preferred_element_type=F32)
                  + jnp.dot(h_mid, w_hi, preferred_element_type=F32))
        lane = lax.broadcasted_iota(jnp.int32, logits.shape, 1)
        lg = jnp.where(lane < N_EXPERTS, logits, NEG)
        v1 = jnp.max(lg, axis=1, keepdims=True)
        i1 = jnp.min(jnp.where(lg == v1, lane, LANES), axis=1, keepdims=True)
        lg2 = jnp.where(lane == i1, NEG, lg)
        v2 = jnp.max(lg2, axis=1, keepdims=True)
        i2 = jnp.min(jnp.where(lg2 == v2, lane, LANES), axis=1, keepdims=True)
        e2 = jnp.exp(v2 - v1)
        w1 = 1.0 / (1.0 + e2)
        gates_ref[...] = jnp.where(lane == i1, w1, 0.0) + jnp.where(lane == i2, e2 * w1, 0.0)
        sel = jnp.logical_or(lane == i1, lane == i2)
        t = MOE_BLOCK
        r = lax.broadcasted_iota(jnp.int32, (t, t), 0)
        cidx = lax.broadcasted_iota(jnp.int32, (t, t), 1)
        lower = (r >= cidx).astype(BF16)
        for blk in range(logits.shape[0] // t):
            sel_b = sel[blk * t:(blk + 1) * t]
            rank = jnp.dot(lower, sel_b.astype(BF16), preferred_element_type=F32)
            pos = jnp.where(sel_b, rank, 0.0)
            pos_ref[blk * t:(blk + 1) * t, :] = pos
            post_ref[blk] = pos.T[:N_EXPERTS]
            cnt_ref[blk] = rank[t - 1:t, :]


def _outproj(a_out, h_m, xf, mod, ng, w_top, w_bot, router_w, batch, seq):
    n, d = xf.shape
    tm = ROW_TILE
    tpb = seq // tm
    moe = router_w is not None
    half = pl.BlockSpec((tm, HEAD_GROUP_W), lambda i: (i, 0))
    full = pl.BlockSpec((tm, d), lambda i: (i, 0))
    in_specs = [half, half, full,
                pl.BlockSpec((1, 6, d), lambda i: (i // tpb, 0, 0)),
                pl.BlockSpec(ng.shape, lambda i: (0, 0)),
                pl.BlockSpec(w_top.shape, lambda i: (0, 0)),
                pl.BlockSpec(w_bot.shape, lambda i: (0, 0))]
    args = [a_out, h_m, xf, mod, ng, w_top, w_bot]
    out_shape = [jax.ShapeDtypeStruct((n, d), F32), jax.ShapeDtypeStruct((n, d), BF16)]
    out_specs = [full, full]
    if moe:
        in_specs.append(pl.BlockSpec(router_w.shape, lambda i: (0, 0)))
        args.append(router_w)
        bpt = tm // MOE_BLOCK
        out_shape += [jax.ShapeDtypeStruct((n, LANES), F32), jax.ShapeDtypeStruct((n, LANES), F32),
                      jax.ShapeDtypeStruct((n // MOE_BLOCK, N_EXPERTS, MOE_BLOCK), F32),
                      jax.ShapeDtypeStruct((n // MOE_BLOCK, 1, LANES), F32)]
        out_specs += [pl.BlockSpec((tm, LANES), lambda i: (i, 0)), pl.BlockSpec((tm, LANES), lambda i: (i, 0)),
                      pl.BlockSpec((bpt, N_EXPERTS, MOE_BLOCK), lambda i: (i, 0, 0)),
                      pl.BlockSpec((bpt, 1, LANES), lambda i: (i, 0, 0))]
    return pl.pallas_call(
        functools.partial(_outproj_kernel, moe=moe),
        out_shape=out_shape,
        grid=(n // tm,),
        in_specs=in_specs,
        out_specs=out_specs,
        compiler_params=_params("parallel"),
        name="out_proj_moe" if moe else "out_proj",
    )(*args)


def _ffn_kernel(h_ref, x_ref, mod_ref, ng_ref, w1_ref, w3_ref, w2_ref, o_ref, acc_sc, *, n_col):
    f = pl.program_id(1)

    @pl.when(f == 0)
    def _():
        acc_sc[...] = jnp.zeros(acc_sc.shape, F32)

    hb = h_ref[...]
    a = jnp.dot(hb, w1_ref[...], preferred_element_type=F32)
    b = jnp.dot(hb, w3_ref[...], preferred_element_type=F32)
    hh = a * _sigmoid(a) * b
    acc_sc[...] += jnp.dot(hh.astype(BF16), w2_ref[...], preferred_element_type=F32)

    @pl.when(f == n_col - 1)
    def _():
        mod = mod_ref[0]
        o_ref[...] = x_ref[...] + mod[5:6] * _rms(acc_sc[...], ng_ref[3:4, :])


def _ffn(h2, x1, mod, ng, w1, w3, w2, batch, seq):
    n, d = x1.shape
    dff = w1.shape[1]
    tm, tf = FFN_ROW_TILE, FFN_COL_TILE
    tpb = seq // tm
    n_col = dff // tf
    row = lambda i, f: (i, 0)
    return pl.pallas_call(
        functools.partial(_ffn_kernel, n_col=n_col),
        out_shape=jax.ShapeDtypeStruct((n, d), F32),
        grid=(n // tm, n_col),
        in_specs=[pl.BlockSpec((tm, d), row), pl.BlockSpec((tm, d), row),
                  pl.BlockSpec((1, 6, d), lambda i, f: (i // tpb, 0, 0)),
                  pl.BlockSpec(ng.shape, lambda i, f: (0, 0)),
                  pl.BlockSpec((d, tf), lambda i, f: (0, f)), pl.BlockSpec((d, tf), lambda i, f: (0, f)),
                  pl.BlockSpec((tf, d), lambda i, f: (f, 0))],
        out_specs=pl.BlockSpec((tm, d), row),
        scratch_shapes=[pltpu.VMEM((tm, d), F32)],
        compiler_params=_params("parallel", "arbitrary"),
        name="dense_ffn",
    )(h2, x1, mod, ng, w1, w3, w2)


def _moe_tables(cnt, batch, seq):
    t = MOE_BLOCK
    e_n = N_EXPERTS
    nbb = seq // t
    ntb = 2 * nbb + e_n
    c = cnt.reshape(batch, nbb, e_n)
    cum = jnp.cumsum(c, axis=1) - c
    tot = jnp.sum(c, axis=1)
    ntile = (tot + t - 1) // t
    tile_end = jnp.cumsum(ntile, axis=1)
    tile_base = tile_end - ntile
    tl = jnp.arange(ntb, dtype=jnp.int32)
    e_t = jnp.sum((tl[None, :, None] >= tile_end[:, None, :]).astype(jnp.int32), axis=2)
    valid = e_t < e_n
    e_c = jnp.minimum(e_t, e_n - 1)
    s0 = (tl[None, :] - jnp.take_along_axis(tile_base, e_c, axis=1)) * t
    bidx = jnp.arange(batch)[:, None]
    cum_sel = cum.transpose(0, 2, 1)[bidx, e_c]
    end_sel = cum_sel + c.transpose(0, 2, 1)[bidx, e_c]
    lo = jnp.sum((end_sel <= s0[..., None]).astype(jnp.int32), axis=2)
    hi = jnp.sum((cum_sel < (s0 + t)[..., None]).astype(jnp.int32), axis=2)

    def flat(a, tail):
        return jnp.concatenate([a.reshape(-1).astype(jnp.int32), jnp.full((1,), tail, jnp.int32)])

    tile_tables = (flat(jnp.broadcast_to(bidx, (batch, ntb)), batch - 1), flat(e_c, e_n - 1), flat(valid, 0),
                   flat(lo, 0), flat(hi, 0), flat(s0, 0), cum.reshape(-1).astype(jnp.int32))
    row0 = (bidx[:, :, None] * ntb + tile_base[:, None, :]) * t + cum
    seg_tables = ((row0 // t).reshape(-1).astype(jnp.int32), (row0 % t).reshape(-1).astype(jnp.int32))
    return tile_tables, seg_tables


def _moe_expert_kernel(tb_ref, te_ref, tv_ref, lo_ref, hi_ref, s0_ref, cum_ref,
                       h_ref, post_ref, w1_ref, w3_ref, w2_ref, o_ref, xg_sc, *, n_col, tf, nbb):
    ti = pl.program_id(0)
    t = MOE_BLOCK

    @pl.when(tv_ref[ti] == 0)
    def _():
        o_ref[...] = jnp.zeros(o_ref.shape, o_ref.dtype)

    @pl.when(tv_ref[ti] == 1)
    def _():
        b = tb_ref[ti]
        e = te_ref[ti]
        s0 = s0_ref[ti]
        xg_sc[...] = jnp.zeros(xg_sc.shape, F32)
        slot = lax.broadcasted_iota(jnp.int32, (t, t), 0).astype(F32)

        def gather(j, carry):
            blk = b * nbb + j
            rank = post_ref[blk, pl.ds(e, 1), :]
            shift = (cum_ref[blk * N_EXPERTS + e] - s0 - 1).astype(F32)
            onehot = jnp.logical_and(rank > 0.0, rank + shift == slot).astype(BF16)
            rows = h_ref[0, pl.ds(pl.multiple_of(j * t, t), t), :]
            xg_sc[...] += jnp.dot(onehot, rows, preferred_element_type=F32)
            return carry

        lax.fori_loop(lo_ref[ti], hi_ref[ti], gather, 0)
        x = xg_sc[...].astype(BF16)
        acc = jnp.zeros(xg_sc.shape, F32)
        for f in range(n_col):
            a = jnp.dot(x, w1_ref[0, :, f * tf:(f + 1) * tf], preferred_element_type=F32)
            g = jnp.dot(x, w3_ref[0, :, f * tf:(f + 1) * tf], preferred_element_type=F32)
            hh = (a * _sigmoid(a) * g).astype(BF16)
            acc = acc + jnp.dot(hh, w2_ref[0, f * tf:(f + 1) * tf, :], preferred_element_type=F32)
        o_ref[...] = acc.astype(BF16)


def _moe_experts(h2, post, tables, w1, w3, w2, batch, seq):
    n, d = h2.shape
    t = MOE_BLOCK
    nbb = seq // t
    n_tiles = tables[0].shape[0]
    dff = w1.shape[2]
    tf = MOE_COL_TILE
    once = pl.Buffered(1)
    grid_spec = pltpu.PrefetchScalarGridSpec(
        num_scalar_prefetch=len(tables),
        grid=(n_tiles,),
        in_specs=[
            pl.BlockSpec((1, seq, d), lambda i, tb, te, *_: (tb[i], 0, 0), pipeline_mode=once),
            pl.BlockSpec(post.shape, lambda i, *_: (0, 0, 0)),
            pl.BlockSpec((1, d, dff), lambda i, tb, te, *_: (te[i], 0, 0), pipeline_mode=once),
            pl.BlockSpec((1, d, dff), lambda i, tb, te, *_: (te[i], 0, 0), pipeline_mode=once),
            pl.BlockSpec((1, dff, d), lambda i, tb, te, *_: (te[i], 0, 0), pipeline_mode=once),
        ],
        out_specs=pl.BlockSpec((t, d), lambda i, *_: (i, 0)),
        scratch_shapes=[pltpu.VMEM((t, d), F32)],
    )
    return pl.pallas_call(
        functools.partial(_moe_expert_kernel, n_col=dff // tf, tf=tf, nbb=nbb),
        out_shape=jax.ShapeDtypeStruct((n_tiles * t, d), BF16),
        grid_spec=grid_spec,
        compiler_params=_params("arbitrary"),
        name="moe_experts",
    )(*tables, h2.reshape(batch, seq, d), post, w1, w3, w2)


def _moe_combine_kernel(st_ref, so_ref, *refs):
    y_refs = refs[:2 * N_EXPERTS]
    pos_ref, gates_ref, x_ref, mod_ref, ng_ref, o_ref = refs[2 * N_EXPERTS:]
    i = pl.program_id(0)
    t = MOE_BLOCK
    pos = pos_ref[...]
    gates = gates_ref[...]
    slot = lax.broadcasted_iota(jnp.int32, (t, 2 * t), 1).astype(F32)
    acc = jnp.zeros(x_ref.shape, F32)
    for e in range(N_EXPERTS):
        rank = pos[:, e:e + 1]
        off = so_ref[i * N_EXPERTS + e].astype(F32)
        onehot = jnp.logical_and(rank > 0.0, rank - 1.0 + off == slot).astype(BF16)
        ys = jnp.concatenate([y_refs[2 * e][...], y_refs[2 * e + 1][...]], axis=0)
        acc = acc + gates[:, e:e + 1] * jnp.dot(onehot, ys, preferred_element_type=F32)
    mod = mod_ref[0]
    o_ref[...] = x_ref[...] + mod[5:6] * _rms(acc, ng_ref[3:4, :])


def _moe_combine(ys, pos, gates, x1, mod, ng, seg_tables, batch, seq):
    n, d = x1.shape
    t = MOE_BLOCK
    nbb = seq // t
    y_specs = []
    for e in range(N_EXPERTS):
        for nxt in range(2):
            y_specs.append(pl.BlockSpec((t, d), functools.partial(
                lambda i, st, so, e, nxt: (st[i * N_EXPERTS + e] + nxt, 0), e=e, nxt=nxt)))
    grid_spec = pltpu.PrefetchScalarGridSpec(
        num_scalar_prefetch=2,
        grid=(n // t,),
        in_specs=y_specs + [
            pl.BlockSpec((t, LANES), lambda i, st, so: (i, 0)),
            pl.BlockSpec((t, LANES), lambda i, st, so: (i, 0)),
            pl.BlockSpec((t, d), lambda i, st, so: (i, 0)),
            pl.BlockSpec((1, 6, d), lambda i, st, so: (i // nbb, 0, 0)),
            pl.BlockSpec(ng.shape, lambda i, st, so: (0, 0)),
        ],
        out_specs=pl.BlockSpec((t, d), lambda i, st, so: (i, 0)),
    )
    return pl.pallas_call(
        _moe_combine_kernel,
        out_shape=jax.ShapeDtypeStruct((n, d), F32),
        grid_spec=grid_spec,
        compiler_params=_params("parallel"),
        name="moe_combine",
    )(*seg_tables, *([ys] * (2 * N_EXPERTS)), pos, gates, x1, mod, ng)


@jax.jit
def _forward(x, c, rel_bias, ada_w, ada_b, norm_g, w_in, gate_b, conv_w, conv_b, lam_vec,
             attn_norm_g, mlstm_norm_g, w_out, ffn_w1, ffn_w3, ffn_w2, router_w, moe_w1, moe_w3, moe_w2):
    batch, seq, d = x.shape
    depth = ada_w.shape[0]
    n = batch * seq
    n_main = N_SLABS * HEAD_GROUP_W
    assert seq % ATTN_BLOCK == 0 and seq % FFN_ROW_TILE == 0 and seq % MLSTM_CHUNK == 0
    assert ROW_TILE % MOE_BLOCK == 0
    assert seq >= 2 * ATTN_BLOCK and ROW_TILE == ATTN_BLOCK
    assert w_in.shape[2] == n_main + 2 * M_HEADS

    mod_all = _ada(c, ada_w, ada_b)
    bias_tiles = _bias_tables(rel_bias, seq)

    xf = x.reshape(n, d)
    for i in range(depth):
        mod = mod_all[i]
        ng = norm_g[i]
        lam_init = 0.8 - 0.6 * math.exp(-0.3 * i)

        w_gate = w_in[i][:, n_main:]
        w_bf = w_in[i][:, :n_main].astype(BF16)
        gw = HEAD_GROUP_W
        slabs = _inproj(
            xf, mod, ng[0:1], jnp.concatenate([w_bf[:, gw:2 * gw], w_bf[:, 3 * gw:]], axis=1),
            w_bf[:, :gw].T, w_bf[:, 2 * gw:3 * gw].T,
            jnp.zeros((d, LANES), BF16).at[:, :2 * M_HEADS].set(w_gate.astype(BF16)),
            w_gate.T.astype(BF16),
            jnp.zeros((1, LANES), F32).at[0, :2 * M_HEADS].set(gate_b[i].reshape(-1)),
            gate_b[i].reshape(2 * M_HEADS, 1), batch, seq)
        ak, mq, mk, mv, mo = (s.reshape(batch, seq, HEAD_GROUP_W) for s in slabs[:N_SLABS - 2])
        aqt, avt, grow, gcol = slabs[N_SLABS - 2:]

        scal = jnp.full((1,), lam_init, F32)
        a_out = _attention(aqt, ak, avt, scal, bias_tiles, lam_vec[i], attn_norm_g[i].reshape(LANES, 1))

        h_m = _mlstm(mq, mk, mv, mo, grow, gcol,
                     conv_w[i].reshape(CONV_K, 2 * M_HEADS, LANES).transpose(1, 0, 2),
                     conv_b[i].reshape(2 * M_HEADS, 1, LANES),
                     mlstm_norm_g[i].reshape(M_HEADS, 1, LANES))

        w_o = w_out[i].astype(BF16)
        j = i // 2
        if i % 2 == 0:
            x1, h2 = _outproj(a_out.reshape(n, -1), h_m.reshape(n, -1), xf, mod, ng,
                              w_o[:HEAD_GROUP_W], w_o[HEAD_GROUP_W:], None, batch, seq)
            xf = _ffn(h2, x1, mod, ng, ffn_w1[j].astype(BF16), ffn_w3[j].astype(BF16),
                      ffn_w2[j].astype(BF16), batch, seq)
        else:
            rw = jnp.zeros((d, LANES), F32).at[:, :N_EXPERTS].set(router_w[j])
            x1, h2, gates, pos, post, cnt = _outproj(a_out.reshape(n, -1), h_m.reshape(n, -1), xf, mod, ng,
                                                     w_o[:HEAD_GROUP_W], w_o[HEAD_GROUP_W:], rw, batch, seq)
            tile_tables, seg_tables = _moe_tables(cnt[:, 0, :N_EXPERTS].astype(jnp.int32), batch, seq)
            ys = _moe_experts(h2, post, tile_tables, moe_w1[j].astype(BF16), moe_w3[j].astype(BF16),
                              moe_w2[j].astype(BF16), batch, seq)
            xf = _moe_combine(ys, pos, gates, x1, mod, ng, seg_tables, batch, seq)
    return xf.reshape(batch, seq, d)


def kernel(x, c, rel_bias, ada_w, ada_b, norm_g, w_in, gate_b, conv_w, conv_b, lam_vec, attn_norm_g,
           mlstm_norm_g, w_out, ffn_w1, ffn_w3, ffn_w2, router_w, moe_w1, moe_w3, moe_w2):
    return _forward(x, c, rel_bias, ada_w, ada_b, norm_g, w_in, gate_b, conv_w, conv_b, lam_vec,
                    attn_norm_g, mlstm_norm_g, w_out, ffn_w1, ffn_w3, ffn_w2, router_w, moe_w1, moe_w3, moe_w2)
```

```python
import functools
import math

import numpy as np
import jax
import jax.numpy as jnp
from jax import lax
from jax.experimental import pallas as pl
from jax.experimental.pallas import tpu as pltpu

F32 = jnp.float32
BF16 = jnp.bfloat16
HIGHEST = lax.Precision.HIGHEST

A_HEADS = 4
A_DH = 64
M_HEADS = 4
M_DH = 128
CONV_K = 4
N_BUCKETS = 32
MAX_DIST = 128
N_EXPERTS = 8
EPS = 1e-6

LANES = 128
HEAD_GROUP_W = 512
N_SLABS = 7
NEG = -1e30
LOG2E = math.log2(math.e)
Q_SCALE = A_DH ** -0.5 * LOG2E
VMEM_LIMIT_BYTES = 56 * 1024 * 1024

ATTN_BLOCK = 512
MLSTM_CHUNK = 256
MLSTM_HEADS_PER_STEP = 4
ROW_TILE = 512
FFN_ROW_TILE = 512
FFN_COL_TILE = 1792
MOE_COL_TILE = 512
ADA_COL_TILE = 1536
MOE_BLOCK = 256


def _params(*sem, flags=None):
    return pltpu.CompilerParams(dimension_semantics=sem, vmem_limit_bytes=VMEM_LIMIT_BYTES, flags=flags)


def _rms(x, g):
    return x * lax.rsqrt(jnp.mean(x * x, axis=-1, keepdims=True) + EPS) * g


def _sigmoid(x):
    return 1.0 / (1.0 + jnp.exp(-x))


def _log_sigmoid(x):
    return jnp.minimum(x, 0.0) - jnp.log(1.0 + jnp.exp(-jnp.abs(x)))


def _split3(x):
    hi = x.astype(BF16)
    r1 = x - hi.astype(F32)
    mid = r1.astype(BF16)
    lo = (r1 - mid.astype(F32)).astype(BF16)
    return hi, mid, lo


def _pick_lane(x, idx):
    lane = lax.broadcasted_iota(jnp.int32, x.shape, 1)
    return jnp.sum(jnp.where(lane == idx, x, 0.0), axis=1, keepdims=True)


def _pick_row(x, idx):
    row = lax.broadcasted_iota(jnp.int32, x.shape, 0)
    return jnp.sum(jnp.where(row == idx, x, 0.0), axis=0, keepdims=True)


def _ada_kernel(c_ref, w_ref, b_ref, o_ref):
    c = c_ref[...]
    ca = c * _sigmoid(c)
    o_ref[0] = jnp.dot(ca, w_ref[0], precision=HIGHEST, preferred_element_type=F32) + b_ref[0]


def _ada(c, ada_w, ada_b):
    depth, d, d6 = ada_w.shape
    b = c.shape[0]
    rows = 8
    cp = jnp.zeros((rows, d), F32).at[:b].set(c)
    out = pl.pallas_call(
        _ada_kernel,
        out_shape=jax.ShapeDtypeStruct((depth, rows, d6), F32),
        grid=(depth, d6 // ADA_COL_TILE),
        in_specs=[
            pl.BlockSpec((rows, d), lambda l, j: (0, 0)),
            pl.BlockSpec((1, d, ADA_COL_TILE), lambda l, j: (l, 0, j)),
            pl.BlockSpec((1, 1, ADA_COL_TILE), lambda l, j: (l, 0, j)),
        ],
        out_specs=pl.BlockSpec((1, rows, ADA_COL_TILE), lambda l, j: (l, 0, j)),
        compiler_params=_params("parallel", "parallel"),
        name="ada_mod",
    )(cp, ada_w, ada_b.reshape(depth, 1, d6))
    return out[:, :b].reshape(depth, b, 6, d)


def _inproj_kernel(x_ref, mod_ref, g_ref, w_ref, wqt_ref, wvt_ref, wg_ref, wgt_ref, gb_ref, gbt_ref,
                   ak, mq, mk, mv, mo, aqt, avt, grow, gcol):
    mod = mod_ref[0]
    h = _rms(x_ref[...], g_ref[...]) * (1.0 + mod[1:2]) + mod[0:1]
    hb = h.astype(BF16)
    for s, o in enumerate((ak, mq, mk, mv, mo)):
        w = w_ref[:, s * HEAD_GROUP_W:(s + 1) * HEAD_GROUP_W]
        o[...] = jnp.dot(hb, w, preferred_element_type=F32).astype(BF16)
    nt = (((1,), (1,)), ((), ()))
    aqt[0, 0] = (lax.dot_general(wqt_ref[...], hb, nt, preferred_element_type=F32) * Q_SCALE).astype(BF16)
    avt[0, 0] = lax.dot_general(wvt_ref[...], hb, nt, preferred_element_type=F32).astype(BF16)
    grow[...] = jnp.dot(hb, wg_ref[...], preferred_element_type=F32) + gb_ref[...]
    gcol[0] = lax.dot_general(wgt_ref[...], hb, (((1,), (1,)), ((), ())),
                              preferred_element_type=F32) + gbt_ref[...]


def _inproj(xf, mod, g, w_main, w_qt, w_vt, wg, wgt, gb, gbt, batch, seq):
    n, d = xf.shape
    tm = ROW_TILE
    tpb = seq // tm
    n_row_slabs = N_SLABS - 2
    slab = jax.ShapeDtypeStruct((n, HEAD_GROUP_W), BF16)
    slab_t = jax.ShapeDtypeStruct((batch, tpb, HEAD_GROUP_W, tm), BF16)
    row_spec = pl.BlockSpec((tm, HEAD_GROUP_W), lambda i: (i, 0))
    t_spec = pl.BlockSpec((1, 1, HEAD_GROUP_W, tm), lambda i: (i // tpb, i % tpb, 0, 0))
    const = lambda i: (0, 0)
    return pl.pallas_call(
        _inproj_kernel,
        out_shape=[slab] * n_row_slabs + [slab_t, slab_t, jax.ShapeDtypeStruct((n, LANES), F32),
                                          jax.ShapeDtypeStruct((batch, 8, seq), F32)],
        grid=(n // tm,),
        in_specs=[
            pl.BlockSpec((tm, d), lambda i: (i, 0)),
            pl.BlockSpec((1, 6, d), lambda i: (i // tpb, 0, 0)),
            pl.BlockSpec((1, d), const),
            pl.BlockSpec(w_main.shape, const),
            pl.BlockSpec(w_qt.shape, const),
            pl.BlockSpec(w_vt.shape, const),
            pl.BlockSpec(wg.shape, const),
            pl.BlockSpec(wgt.shape, const),
            pl.BlockSpec(gb.shape, const),
            pl.BlockSpec(gbt.shape, const),
        ],
        out_specs=[row_spec] * n_row_slabs + [t_spec, t_spec,
                                              pl.BlockSpec((tm, LANES), lambda i: (i, 0)),
                                              pl.BlockSpec((1, 8, tm), lambda i: (i // tpb, 0, i % tpb))],
        compiler_params=_params("parallel"),
        name="in_proj",
    )(xf, mod, g, w_main, w_qt, w_vt, wg, wgt, gb, gbt)


ACC_ROWS = LANES + 16


def _attn_kernel(scal_ref, qt_ref, k_ref, vt_ref, bias_ref, lv_ref, g_ref, o_ref, m_sc, acc_sc,
                 sa_sc, sb_sc, mxa_sc, mxb_sc, *, blk):
    qi = pl.program_id(2)
    lam_init = scal_ref[0]

    feat = lax.broadcasted_iota(jnp.int32, (LANES, blk), 0)
    qt = qt_ref[0, 0]
    zero = jnp.zeros_like(qt)
    qz = (jnp.where(feat < A_DH, qt, zero), jnp.where(feat >= A_DH, qt, zero))
    ones_rows = jnp.ones((ACC_ROWS - LANES, blk), BF16)

    m_sc[...] = jnp.full(m_sc.shape, NEG, F32)
    acc_sc[...] = jnp.zeros(acc_sc.shape, F32)

    def scores(kj, s_sc, mx_sc, near=None):
        start = pl.multiple_of(kj * blk, blk)
        kt = k_ref[0, pl.ds(start, blk), :]
        for c in range(2):
            s = jnp.dot(kt, qz[c], preferred_element_type=F32)
            if near is not None:
                s = s + bias_ref[0, near]
            s_sc[c] = s
            mx_sc[c] = jnp.max(s, axis=0, keepdims=True)

    def accumulate(kj, s_sc, mx_sc):
        vext = jnp.concatenate([vt_ref[0, kj], ones_rows], axis=0)
        for c in range(2):
            m_old = m_sc[c]
            m_new = jnp.maximum(m_old, mx_sc[c])
            alpha = jnp.exp2(m_old - m_new)
            p = jnp.exp2(s_sc[c] - m_new).astype(BF16)
            pv = jnp.dot(vext, p, preferred_element_type=F32)
            acc_sc[c] = alpha * acc_sc[c] + pv
            m_sc[c] = m_new

    n_far = jnp.maximum(qi - 1, 0)
    scores(qi, sa_sc, mxa_sc, near=0)

    @pl.when(qi == 0)
    def _():
        accumulate(qi, sa_sc, mxa_sc)

    @pl.when(qi == 1)
    def _():
        scores(qi - 1, sb_sc, mxb_sc, near=1)
        accumulate(qi, sa_sc, mxa_sc)
        accumulate(qi - 1, sb_sc, mxb_sc)

    @pl.when(qi >= 2)
    def _():
        scores(qi - 1, sb_sc, mxb_sc, near=1)
        accumulate(qi, sa_sc, mxa_sc)
        scores(0, sa_sc, mxa_sc)
        accumulate(qi - 1, sb_sc, mxb_sc)

    def pair_body(u, carry):
        scores(2 * u + 1, sb_sc, mxb_sc)
        accumulate(2 * u, sa_sc, mxa_sc)
        scores(jnp.minimum(2 * u + 2, n_far - 1), sa_sc, mxa_sc)
        accumulate(2 * u + 1, sb_sc, mxb_sc)
        return carry

    lax.fori_loop(0, n_far // 2, pair_body, 0)

    @pl.when(n_far % 2 == 1)
    def _():
        accumulate(n_far - 1, sa_sc, mxa_sc)

    acc0 = acc_sc[0]
    acc1 = acc_sc[1]
    o0 = acc0[:LANES] * (1.0 / acc0[LANES:LANES + 1])
    o1 = acc1[:LANES] * (1.0 / acc1[LANES:LANES + 1])
    lv = lv_ref[...]
    lam = (jnp.exp(jnp.sum(lv[0:1] * lv[1:2], axis=1, keepdims=True))
           - jnp.exp(jnp.sum(lv[2:3] * lv[3:4], axis=1, keepdims=True)) + lam_init)
    a = o0 - lam * o1
    y = a * lax.rsqrt(jnp.mean(a * a, axis=0, keepdims=True) + EPS) * g_ref[...] * (1.0 - lam_init)
    o_ref[0] = y.T.astype(BF16)


def _attention(aqt, ak, avt, scal, bias_tiles, lam_vec, norm_g):
    batch, seq, _ = ak.shape
    blk = ATTN_BLOCK
    nk = seq // blk
    kernel = functools.partial(_attn_kernel, blk=blk)
    grid_spec = pltpu.PrefetchScalarGridSpec(
        num_scalar_prefetch=1,
        grid=(batch, A_HEADS, nk),
        in_specs=[
            pl.BlockSpec((1, 1, LANES, blk), lambda b, h, i, s: (b, i, h, 0)),
            pl.BlockSpec((1, seq, LANES), lambda b, h, i, s: (b, 0, h)),
            pl.BlockSpec((1, nk, LANES, blk), lambda b, h, i, s: (b, 0, h, 0)),
            pl.BlockSpec((1, 2, blk, blk), lambda b, h, i, s: (h, 0, 0, 0)),
            pl.BlockSpec(lam_vec.shape, lambda b, h, i, s: (0, 0)),
            pl.BlockSpec(norm_g.shape, lambda b, h, i, s: (0, 0)),
        ],
        out_specs=pl.BlockSpec((1, blk, LANES), lambda b, h, i, s: (b, i, h)),
        scratch_shapes=[pltpu.VMEM((2, 1, blk), F32), pltpu.VMEM((2, ACC_ROWS, blk), F32),
                        pltpu.VMEM((2, blk, blk), F32), pltpu.VMEM((2, blk, blk), F32),
                        pltpu.VMEM((2, 1, blk), F32), pltpu.VMEM((2, 1, blk), F32)],
    )
    return pl.pallas_call(
        kernel,
        out_shape=jax.ShapeDtypeStruct(ak.shape, BF16),
        grid_spec=grid_spec,
        compiler_params=_params("parallel", "parallel", "arbitrary"),
        name="diff_attn",
    )(scal, aqt, ak, avt, bias_tiles, lam_vec, norm_g)


def _t5_bucket(n):
    max_exact = N_BUCKETS // 2
    nf = jnp.maximum(n, 1).astype(F32)
    large = max_exact + (jnp.log(nf / max_exact) / math.log(MAX_DIST / max_exact)
                         * (N_BUCKETS - max_exact)).astype(jnp.int32)
    large = jnp.minimum(large, N_BUCKETS - 1)
    return jnp.where(n < max_exact, n, large)


def _bias_tables(rel_bias, seq):
    blk = ATTN_BLOCK
    far = np.arange(blk + 1, max(seq, blk + 2), dtype=np.float64)
    sat = (N_BUCKETS // 2) + np.log(far / (N_BUCKETS // 2)) / math.log(MAX_DIST / (N_BUCKETS // 2)) * (N_BUCKETS // 2)
    assert np.all(sat >= N_BUCKETS - 0.5), "far tiles need a saturated distance bucket"
    dist_bias = rel_bias[_t5_bucket(jnp.arange(seq, dtype=jnp.int32))].T.astype(F32)
    heads = dist_bias.shape[0]
    last = dist_bias[:, seq - 1:seq]
    padded = jnp.concatenate([jnp.full((heads, blk), NEG, F32), dist_bias,
                              jnp.broadcast_to(last, (heads, blk))], axis=1)
    tiles = []
    for d in (0, 1):
        base = blk + d * blk
        w = jnp.concatenate([padded[:, base:base + blk + 1], padded[:, base - blk + 1:base]], axis=1)
        t = jnp.tile(w, (1, blk))[:, :blk * (2 * blk - 1)].reshape(heads, blk, 2 * blk - 1)[:, :, :blk]
        tiles.append(t - last[:, :, None])
    return jnp.stack(tiles, axis=1) * LOG2E


def _mlstm_kernel(mq_ref, mk_ref, mv_ref, mo_ref, grow_ref, gcol_ref, cwq_ref, cwk_ref, cbq_ref, cbk_ref,
                  ng_ref, o_ref, xq_sc, xk_sc, c_sc, n_sc, m_sc, *, chunk, heads):
    L = chunk
    hp = pl.program_id(1)
    ci = pl.program_id(2)

    @pl.when(ci == 0)
    def _():
        xq_sc[:, 0:8, :] = jnp.zeros((heads, 8, LANES), F32)
        xk_sc[:, 0:8, :] = jnp.zeros((heads, 8, LANES), F32)
        c_sc[...] = jnp.zeros(c_sc.shape, F32)
        n_sc[...] = jnp.zeros(n_sc.shape, F32)
        m_sc[...] = jnp.zeros(m_sc.shape, F32)

    def conv_silu(x, sc, w, b):
        sc[8:8 + L, :] = x.astype(F32)
        y = b
        for j in range(CONV_K):
            y = y + sc[5 + j:5 + j + L, :] * w[j:j + 1]
        sc[0:8, :] = sc[L:L + 8, :]
        return y * _sigmoid(y)

    r = lax.broadcasted_iota(jnp.int32, (L, L), 0)
    cidx = lax.broadcasted_iota(jnp.int32, (L, L), 1)
    causal = r >= cidx
    lower = causal.astype(BF16)
    upper = (r <= cidx).astype(BF16)

    g_rows = grow_ref[...]
    cum_rows = sum(jnp.dot(lower, part, preferred_element_type=F32) for part in _split3(_log_sigmoid(g_rows)))
    g_cols = gcol_ref[0]
    cum_cols = sum(jnp.dot(part, upper, preferred_element_type=F32) for part in _split3(_log_sigmoid(g_cols)))

    for hh in range(heads):
        h = hp * heads + hh
        cols = slice(hh * LANES, (hh + 1) * LANES)
        q = conv_silu(mq_ref[0, :, cols], xq_sc.at[hh], cwq_ref[hh], cbq_ref[hh])
        k = conv_silu(mk_ref[0, :, cols], xk_sc.at[hh], cwk_ref[hh], cbk_ref[hh]) * (M_DH ** -0.5)
        v = mv_ref[0, :, cols]

        ig_col = _pick_lane(g_rows, h)
        b_col = _pick_lane(cum_rows, M_HEADS + h)
        ig_row = _pick_row(g_cols, h)
        b_row = _pick_row(cum_cols, M_HEADS + h)
        b_last = b_col[L - 1:L, :]
        m_prev = m_sc[hh]

        dlog = jnp.where(causal, b_col - b_row + ig_row, NEG)
        inter = b_col + m_prev
        m_t = jnp.maximum(inter, jnp.max(dlog, axis=1, keepdims=True))
        qb = q.astype(BF16)
        kb = k.astype(BF16)
        sw = (lax.dot_general(qb, kb, (((1,), (1,)), ((), ())), preferred_element_type=F32)
              * jnp.exp(dlog - m_t))
        dec = jnp.exp(inter - m_t)
        num = (dec * jnp.dot(qb, c_sc[hh].astype(BF16), preferred_element_type=F32)
               + jnp.dot(sw.astype(BF16), v, preferred_element_type=F32))
        den = dec * jnp.sum(q * n_sc[hh], axis=1, keepdims=True) + jnp.sum(sw, axis=1, keepdims=True)
        h_t = num * (1.0 / jnp.maximum(jnp.abs(den), jnp.exp(-m_t)))

        a_col = b_last - b_col + ig_col
        m_new = jnp.maximum(b_last + m_prev, jnp.max(a_col, axis=0, keepdims=True))
        decay = jnp.exp(b_last + m_prev - m_new)
        kw = k * jnp.exp(a_col - m_new)
        c_sc[hh] = decay * c_sc[hh] + lax.dot_general(kw.astype(BF16), v, (((0,), (0,)), ((), ())),
                                                      preferred_element_type=F32)
        n_sc[hh] = decay * n_sc[hh] + jnp.sum(kw, axis=0, keepdims=True)
        m_sc[hh] = m_new

        hm = _sigmoid(mo_ref[0, :, cols].astype(F32)) * h_t
        o_ref[0, :, cols] = _rms(hm, ng_ref[hh]).astype(BF16)


def _mlstm(mq, mk, mv, mo, grow, gcol, conv_w, conv_b, norm_g):
    batch, seq, _ = mq.shape
    L = MLSTM_CHUNK
    hps = MLSTM_HEADS_PER_STEP
    groups = M_HEADS // hps
    kernel = functools.partial(_mlstm_kernel, chunk=L, heads=hps)
    slab = pl.BlockSpec((1, L, hps * LANES), lambda b, h, c: (b, c, h))
    nchunk = seq // L
    return pl.pallas_call(
        kernel,
        out_shape=jax.ShapeDtypeStruct(mq.shape, BF16),
        grid=(batch, groups, nchunk),
        in_specs=[
            slab, slab, slab, slab,
            pl.BlockSpec((L, LANES), lambda b, h, c: (b * nchunk + c, 0)),
            pl.BlockSpec((1, 8, L), lambda b, h, c: (b, 0, c)),
            pl.BlockSpec((hps, CONV_K, LANES), lambda b, h, c: (h, 0, 0)),
            pl.BlockSpec((hps, CONV_K, LANES), lambda b, h, c: (groups + h, 0, 0)),
            pl.BlockSpec((hps, 1, LANES), lambda b, h, c: (h, 0, 0)),
            pl.BlockSpec((hps, 1, LANES), lambda b, h, c: (groups + h, 0, 0)),
            pl.BlockSpec((hps, 1, LANES), lambda b, h, c: (h, 0, 0)),
        ],
        out_specs=slab,
        scratch_shapes=[pltpu.VMEM((hps, L + 8, LANES), F32), pltpu.VMEM((hps, L + 8, LANES), F32),
                        pltpu.VMEM((hps, M_DH, M_DH), F32), pltpu.VMEM((hps, 1, M_DH), F32),
                        pltpu.VMEM((hps, 1, 1), F32)],
        compiler_params=_params("parallel", "parallel", "arbitrary"),
        name="mlstm",
    )(mq, mk, mv, mo, grow, gcol, conv_w, conv_w, conv_b, conv_b, norm_g)


def _outproj_kernel(*refs, moe):
    if moe:
        (a_ref, hm_ref, x_ref, mod_ref, ng_ref, wt_ref, wb_ref, rw_ref,
         x1_ref, h2_ref, gates_ref, pos_ref, post_ref, cnt_ref) = refs
    else:
        a_ref, hm_ref, x_ref, mod_ref, ng_ref, wt_ref, wb_ref, x1_ref, h2_ref = refs
    y = (jnp.dot(a_ref[...], wt_ref[...], preferred_element_type=F32)
         + jnp.dot(hm_ref[...], wb_ref[...], preferred_element_type=F32))
    mod = mod_ref[0]
    ng = ng_ref[...]
    x1 = x_ref[...] + mod[2:3] * _rms(y, ng[1:2])
    x1_ref[...] = x1
    h2 = _rms(x1, ng[2:3]) * (1.0 + mod[4:5]) + mod[3:4]
    h2_ref[...] = h2.astype(BF16)
    if moe:
        h_hi, h_mid, _ = _split3(h2)
        w_hi, w_mid, _ = _split3(rw_ref[...])
        logits = (jnp.dot(h_hi, w_hi, preferred_element_type=F32)
                  + jnp.dot(h_hi, w_mid, preferred_element_type=F32)
                  + jnp.dot(h_mid, w_hi, preferred_element_type=F32))
        lane = lax.broadcasted_iota(jnp.int32, logits.shape, 1)
        lg = jnp.where(lane < N_EXPERTS, logits, NEG)
        v1 = jnp.max(lg, axis=1, keepdims=True)
        i1 = jnp.min(jnp.where(lg == v1, lane, LANES), axis=1, keepdims=True)
        lg2 = jnp.where(lane == i1, NEG, lg)
        v2 = jnp.max(lg2, axis=1, keepdims=True)
        i2 = jnp.min(jnp.where(lg2 == v2, lane, LANES), axis=1, keepdims=True)
        e2 = jnp.exp(v2 - v1)
        w1 = 1.0 / (1.0 + e2)
        gates_ref[...] = jnp.where(lane == i1, w1, 0.0) + jnp.where(lane == i2, e2 * w1, 0.0)
        sel = jnp.logical_or(lane == i1, lane == i2)
        t = MOE_BLOCK
        r = lax.broadcasted_iota(jnp.int32, (t, t), 0)
        cidx = lax.broadcasted_iota(jnp.int32, (t, t), 1)
        lower = (r >= cidx).astype(BF16)
        for blk in range(logits.shape[0] // t):
            sel_b = sel[blk * t:(blk + 1) * t]
            rank = jnp.dot(lower, sel_b.astype(BF16), preferred_element_type=F32)
            pos = jnp.where(sel_b, rank, 0.0)
            pos_ref[blk * t:(blk + 1) * t, :] = pos
            post_ref[blk] = pos.T[:N_EXPERTS]
            cnt_ref[blk] = rank[t - 1:t, :]


def _outproj(a_out, h_m, xf, mod, ng, w_top, w_bot, router_w, batch, seq):
    n, d = xf.shape
    tm = ROW_TILE
    tpb = seq // tm
    moe = router_w is not None
    half = pl.BlockSpec((tm, HEAD_GROUP_W), lambda i: (i, 0))
    full = pl.BlockSpec((tm, d), lambda i: (i, 0))
    in_specs = [half, half, full,
                pl.BlockSpec((1, 6, d), lambda i: (i // tpb, 0, 0)),
                pl.BlockSpec(ng.shape, lambda i: (0, 0)),
                pl.BlockSpec(w_top.shape, lambda i: (0, 0)),
                pl.BlockSpec(w_bot.shape, lambda i: (0, 0))]
    args = [a_out, h_m, xf, mod, ng, w_top, w_bot]
    out_shape = [jax.ShapeDtypeStruct((n, d), F32), jax.ShapeDtypeStruct((n, d), BF16)]
    out_specs = [full, full]
    if moe:
        in_specs.append(pl.BlockSpec(router_w.shape, lambda i: (0, 0)))
        args.append(router_w)
        bpt = tm // MOE_BLOCK
        out_shape += [jax.ShapeDtypeStruct((n, LANES), F32), jax.ShapeDtypeStruct((n, LANES), F32),
                      jax.ShapeDtypeStruct((n // MOE_BLOCK, N_EXPERTS, MOE_BLOCK), F32),
                      jax.ShapeDtypeStruct((n // MOE_BLOCK, 1, LANES), F32)]
        out_specs += [pl.BlockSpec((tm, LANES), lambda i: (i, 0)), pl.BlockSpec((tm, LANES), lambda i: (i, 0)),
                      pl.BlockSpec((bpt, N_EXPERTS, MOE_BLOCK), lambda i: (i, 0, 0)),
                      pl.BlockSpec((bpt, 1, LANES), lambda i: (i, 0, 0))]
    return pl.pallas_call(
        functools.partial(_outproj_kernel, moe=moe),
        out_shape=out_shape,
        grid=(n // tm,),
        in_specs=in_specs,
        out_specs=out_specs,
        compiler_params=_params("parallel"),
        name="out_proj_moe" if moe else "out_proj",
    )(*args)


def _ffn_kernel(h_ref, x_ref, mod_ref, ng_ref, w1_ref, w3_ref, w2_ref, o_ref, acc_sc, *, n_col):
    f = pl.program_id(1)

    @pl.when(f == 0)
    def _():
        acc_sc[...] = jnp.zeros(acc_sc.shape, F32)

    hb = h_ref[...]
    a = jnp.dot(hb, w1_ref[0], preferred_element_type=F32)
    b = jnp.dot(hb, w3_ref[0], preferred_element_type=F32)
    hh = a * _sigmoid(a) * b
    acc_sc[...] += jnp.dot(hh.astype(BF16), w2_ref[0], preferred_element_type=F32)

    @pl.when(f == n_col - 1)
    def _():
        mod = mod_ref[0]
        o_ref[...] = x_ref[...] + mod[5:6] * _rms(acc_sc[...], ng_ref[3:4, :])


def _ffn(h2, x1, mod, ng, w1, w3, w2, layer, batch, seq):
    n, d = x1.shape
    dff = w1.shape[2]
    tm, tf = FFN_ROW_TILE, FFN_COL_TILE
    tpb = seq // tm
    n_col = dff // tf
    row = lambda i, f: (i, 0)
    return pl.pallas_call(
        functools.partial(_ffn_kernel, n_col=n_col),
        out_shape=jax.ShapeDtypeStruct((n, d), F32),
        grid=(n // tm, n_col),
        in_specs=[pl.BlockSpec((tm, d), row), pl.BlockSpec((tm, d), row),
                  pl.BlockSpec((1, 6, d), lambda i, f: (i // tpb, 0, 0)),
                  pl.BlockSpec(ng.shape, lambda i, f: (0, 0)),
                  pl.BlockSpec((1, d, tf), lambda i, f: (layer, 0, f)),
                  pl.BlockSpec((1, d, tf), lambda i, f: (layer, 0, f)),
                  pl.BlockSpec((1, tf, d), lambda i, f: (layer, f, 0))],
        out_specs=pl.BlockSpec((tm, d), row),
        scratch_shapes=[pltpu.VMEM((tm, d), F32)],
        compiler_params=_params("parallel", "arbitrary"),
        name="dense_ffn",
    )(h2, x1, mod, ng, w1, w3, w2)


def _moe_tables(cnt, batch, seq):
    t = MOE_BLOCK
    e_n = N_EXPERTS
    nbb = seq // t
    ntb = 2 * nbb + e_n
    c = cnt.reshape(batch, nbb, e_n)
    cum = jnp.cumsum(c, axis=1) - c
    tot = jnp.sum(c, axis=1)
    ntile = (tot + t - 1) // t
    tile_end = jnp.cumsum(ntile, axis=1)
    tile_base = tile_end - ntile
    tl = jnp.arange(ntb, dtype=jnp.int32)
    e_t = jnp.sum((tl[None, :, None] >= tile_end[:, None, :]).astype(jnp.int32), axis=2)
    valid = e_t < e_n
    e_c = jnp.minimum(e_t, e_n - 1)
    s0 = (tl[None, :] - jnp.take_along_axis(tile_base, e_c, axis=1)) * t
    bidx = jnp.arange(batch)[:, None]
    cum_sel = cum.transpose(0, 2, 1)[bidx, e_c]
    end_sel = cum_sel + c.transpose(0, 2, 1)[bidx, e_c]
    lo = jnp.sum((end_sel <= s0[..., None]).astype(jnp.int32), axis=2)
    hi = jnp.sum((cum_sel < (s0 + t)[..., None]).astype(jnp.int32), axis=2)

    def flat(a, tail):
        return jnp.concatenate([a.reshape(-1).astype(jnp.int32), jnp.full((1,), tail, jnp.int32)])

    tile_tables = (flat(jnp.broadcast_to(bidx, (batch, ntb)), batch - 1), flat(e_c, e_n - 1), flat(valid, 0),
                   flat(lo, 0), flat(hi, 0), flat(s0, 0), cum.reshape(-1).astype(jnp.int32))
    row0 = (bidx[:, :, None] * ntb + tile_base[:, None, :]) * t + cum
    seg_tables = ((row0 // t).reshape(-1).astype(jnp.int32), (row0 % t).reshape(-1).astype(jnp.int32))
    return tile_tables, seg_tables


def _moe_expert_kernel(tb_ref, te_ref, tv_ref, lo_ref, hi_ref, s0_ref, cum_ref,
                       h_ref, post_ref, w1_ref, w3_ref, w2_ref, o_ref, xg_sc, *, n_col, tf, nbb):
    ti = pl.program_id(0)
    t = MOE_BLOCK

    @pl.when(tv_ref[ti] == 0)
    def _():
        o_ref[...] = jnp.zeros(o_ref.shape, o_ref.dtype)

    @pl.when(tv_ref[ti] == 1)
    def _():
        b = tb_ref[ti]
        e = te_ref[ti]
        s0 = s0_ref[ti]
        xg_sc[...] = jnp.zeros(xg_sc.shape, F32)
        slot = lax.broadcasted_iota(jnp.int32, (t, t), 0).astype(F32)

        def gather(j, carry):
            blk = b * nbb + j
            rank = post_ref[blk, pl.ds(e, 1), :]
            shift = (cum_ref[blk * N_EXPERTS + e] - s0 - 1).astype(F32)
            onehot = jnp.logical_and(rank > 0.0, rank + shift == slot).astype(BF16)
            rows = h_ref[0, pl.ds(pl.multiple_of(j * t, t), t), :]
            xg_sc[...] += jnp.dot(onehot, rows, preferred_element_type=F32)
            return carry

        lax.fori_loop(lo_ref[ti], hi_ref[ti], gather, 0)
        x = xg_sc[...].astype(BF16)
        acc = jnp.zeros(xg_sc.shape, F32)
        for f in range(n_col):
            a = jnp.dot(x, w1_ref[0, 0, :, f * tf:(f + 1) * tf], preferred_element_type=F32)
            g = jnp.dot(x, w3_ref[0, 0, :, f * tf:(f + 1) * tf], preferred_element_type=F32)
            hh = (a * _sigmoid(a) * g).astype(BF16)
            acc = acc + jnp.dot(hh, w2_ref[0, 0, f * tf:(f + 1) * tf, :], preferred_element_type=F32)
        o_ref[...] = acc.astype(BF16)


def _moe_experts(h2, post, tables, w1, w3, w2, layer, batch, seq):
    n, d = h2.shape
    t = MOE_BLOCK
    nbb = seq // t
    n_tiles = tables[0].shape[0]
    dff = w1.shape[3]
    tf = MOE_COL_TILE
    once = pl.Buffered(1)
    grid_spec = pltpu.PrefetchScalarGridSpec(
        num_scalar_prefetch=len(tables),
        grid=(n_tiles,),
        in_specs=[
            pl.BlockSpec((1, seq, d), lambda i, tb, te, *_: (tb[i], 0, 0), pipeline_mode=once),
            pl.BlockSpec(post.shape, lambda i, *_: (0, 0, 0)),
            pl.BlockSpec((1, 1, d, dff), lambda i, tb, te, *_: (layer, te[i], 0, 0), pipeline_mode=once),
            pl.BlockSpec((1, 1, d, dff), lambda i, tb, te, *_: (layer, te[i], 0, 0), pipeline_mode=once),
            pl.BlockSpec((1, 1, dff, d), lambda i, tb, te, *_: (layer, te[i], 0, 0), pipeline_mode=once),
        ],
        out_specs=pl.BlockSpec((t, d), lambda i, *_: (i, 0)),
        scratch_shapes=[pltpu.VMEM((t, d), F32)],
    )
    return pl.pallas_call(
        functools.partial(_moe_expert_kernel, n_col=dff // tf, tf=tf, nbb=nbb),
        out_shape=jax.ShapeDtypeStruct((n_tiles * t, d), BF16),
        grid_spec=grid_spec,
        compiler_params=_params("arbitrary"),
        name="moe_experts",
    )(*tables, h2.reshape(batch, seq, d), post, w1, w3, w2)


def _moe_combine_kernel(st_ref, so_ref, *refs):
    y_refs = refs[:2 * N_EXPERTS]
    pos_ref, gates_ref, x_ref, mod_ref, ng_ref, o_ref = refs[2 * N_EXPERTS:]
    i = pl.program_id(0)
    t = MOE_BLOCK
    pos = pos_ref[...]
    gates = gates_ref[...]
    slot = lax.broadcasted_iota(jnp.int32, (t, 2 * t), 1).astype(F32)
    acc = jnp.zeros(x_ref.shape, F32)
    for e in range(N_EXPERTS):
        rank = pos[:, e:e + 1]
        off = so_ref[i * N_EXPERTS + e].astype(F32)
        onehot = jnp.logical_and(rank > 0.0, rank - 1.0 + off == slot).astype(BF16)
        ys = jnp.concatenate([y_refs[2 * e][...], y_refs[2 * e + 1][...]], axis=0)
        acc = acc + gates[:, e:e + 1] * jnp.dot(onehot, ys, preferred_element_type=F32)
    mod = mod_ref[0]
    o_ref[...] = x_ref[...] + mod[5:6] * _rms(acc, ng_ref[3:4, :])


def _moe_combine(ys, pos, gates, x1, mod, ng, seg_tables, batch, seq):
    n, d = x1.shape
    t = MOE_BLOCK
    nbb = seq // t
    y_specs = []
    for e in range(N_EXPERTS):
        for nxt in range(2):
            y_specs.append(pl.BlockSpec((t, d), functools.partial(
                lambda i, st, so, e, nxt: (st[i * N_EXPERTS + e] + nxt, 0), e=e, nxt=nxt)))
    grid_spec = pltpu.PrefetchScalarGridSpec(
        num_scalar_prefetch=2,
        grid=(n // t,),
        in_specs=y_specs + [
            pl.BlockSpec((t, LANES), lambda i, st, so: (i, 0)),
            pl.BlockSpec((t, LANES), lambda i, st, so: (i, 0)),
            pl.BlockSpec((t, d), lambda i, st, so: (i, 0)),
            pl.BlockSpec((1, 6, d), lambda i, st, so: (i // nbb, 0, 0)),
            pl.BlockSpec(ng.shape, lambda i, st, so: (0, 0)),
        ],
        out_specs=pl.BlockSpec((t, d), lambda i, st, so: (i, 0)),
    )
    return pl.pallas_call(
        _moe_combine_kernel,
        out_shape=jax.ShapeDtypeStruct((n, d), F32),
        grid_spec=grid_spec,
        compiler_params=_params("parallel"),
        name="moe_combine",
    )(*seg_tables, *([ys] * (2 * N_EXPERTS)), pos, gates, x1, mod, ng)


@jax.jit
def _forward(x, c, rel_bias, ada_w, ada_b, norm_g, w_in, gate_b, conv_w, conv_b, lam_vec,
             attn_norm_g, mlstm_norm_g, w_out, ffn_w1, ffn_w3, ffn_w2, router_w, moe_w1, moe_w3, moe_w2):
    batch, seq, d = x.shape
    depth = ada_w.shape[0]
    n = batch * seq
    n_main = N_SLABS * HEAD_GROUP_W
    assert seq % ATTN_BLOCK == 0 and seq % FFN_ROW_TILE == 0 and seq % MLSTM_CHUNK == 0
    assert ROW_TILE % MOE_BLOCK == 0
    assert seq >= 2 * ATTN_BLOCK and ROW_TILE == ATTN_BLOCK
    assert w_in.shape[2] == n_main + 2 * M_HEADS

    mod_all = _ada(c, ada_w, ada_b)
    bias_tiles = _bias_tables(rel_bias, seq)

    ffn_w = tuple(w.astype(BF16) for w in (ffn_w1, ffn_w3, ffn_w2))
    moe_w = tuple(w.astype(BF16) for w in (moe_w1, moe_w3, moe_w2))
    xf = x.reshape(n, d)
    for i in range(depth):
        mod = mod_all[i]
        ng = norm_g[i]
        lam_init = 0.8 - 0.6 * math.exp(-0.3 * i)

        w_gate = w_in[i][:, n_main:]
        w_bf = w_in[i][:, :n_main].astype(BF16)
        gw = HEAD_GROUP_W
        slabs = _inproj(
            xf, mod, ng[0:1], jnp.concatenate([w_bf[:, gw:2 * gw], w_bf[:, 3 * gw:]], axis=1),
            w_bf[:, :gw].T, w_bf[:, 2 * gw:3 * gw].T,
            jnp.zeros((d, LANES), BF16).at[:, :2 * M_HEADS].set(w_gate.astype(BF16)),
            w_gate.T.astype(BF16),
            jnp.zeros((1, LANES), F32).at[0, :2 * M_HEADS].set(gate_b[i].reshape(-1)),
            gate_b[i].reshape(2 * M_HEADS, 1), batch, seq)
        ak, mq, mk, mv, mo = (s.reshape(batch, seq, HEAD_GROUP_W) for s in slabs[:N_SLABS - 2])
        aqt, avt, grow, gcol = slabs[N_SLABS - 2:]

        scal = jnp.full((1,), lam_init, F32)
        a_out = _attention(aqt, ak, avt, scal, bias_tiles, lam_vec[i], attn_norm_g[i].reshape(LANES, 1))

        h_m = _mlstm(mq, mk, mv, mo, grow, gcol,
                     conv_w[i].reshape(CONV_K, 2 * M_HEADS, LANES).transpose(1, 0, 2),
                     conv_b[i].reshape(2 * M_HEADS, 1, LANES),
                     mlstm_norm_g[i].reshape(M_HEADS, 1, LANES))

        w_o = w_out[i].astype(BF16)
        j = i // 2
        if i % 2 == 0:
            x1, h2 = _outproj(a_out.reshape(n, -1), h_m.reshape(n, -1), xf, mod, ng,
                              w_o[:HEAD_GROUP_W], w_o[HEAD_GROUP_W:], None, batch, seq)
            xf = _ffn(h2, x1, mod, ng, *ffn_w, j, batch, seq)
        else:
            rw = jnp.zeros((d, LANES), F32).at[:, :N_EXPERTS].set(router_w[j])
            x1, h2, gates, pos, post, cnt = _outproj(a_out.reshape(n, -1), h_m.reshape(n, -1), xf, mod, ng,
                                                     w_o[:HEAD_GROUP_W], w_o[HEAD_GROUP_W:], rw, batch, seq)
            tile_tables, seg_tables = _moe_tables(cnt[:, 0, :N_EXPERTS].astype(jnp.int32), batch, seq)
            ys = _moe_experts(h2, post, tile_tables, *moe_w, j, batch, seq)
            xf = _moe_combine(ys, pos, gates, x1, mod, ng, seg_tables, batch, seq)
    return xf.reshape(batch, seq, d)


def kernel(x, c, rel_bias, ada_w, ada_b, norm_g, w_in, gate_b, conv_w, conv_b, lam_vec, attn_norm_g,
           mlstm_norm_g, w_out, ffn_w1, ffn_w3, ffn_w2, router_w, moe_w1, moe_w3, moe_w2):
    return _forward(x, c, rel_bias, ada_w, ada_b, norm_g, w_in, gate_b, conv_w, conv_b, lam_vec,
                    attn_norm_g, mlstm_norm_g, w_out, ffn_w1, ffn_w3, ffn_w2, router_w, moe_w1, moe_w3, moe_w2)
```

```python
import functools
import math

import numpy as np
import jax
import jax.numpy as jnp
from jax import lax
from jax.experimental import pallas as pl
from jax.experimental.pallas import tpu as pltpu

F32 = jnp.float32
BF16 = jnp.bfloat16
HIGHEST = lax.Precision.HIGHEST

A_HEADS = 4
A_DH = 64
M_HEADS = 4
M_DH = 128
CONV_K = 4
N_BUCKETS = 32
MAX_DIST = 128
N_EXPERTS = 8
EPS = 1e-6

LANES = 128
HEAD_GROUP_W = 512
N_SLABS = 7
NEG = -1e30
LOG2E = math.log2(math.e)
Q_SCALE = A_DH ** -0.5 * LOG2E
VMEM_LIMIT_BYTES = 56 * 1024 * 1024

ATTN_BLOCK = 512
MLSTM_CHUNK = 256
MLSTM_HEADS_PER_STEP = 4
ROW_TILE = 512
FFN_ROW_TILE = 512
FFN_COL_TILE = 1792
MOE_COL_TILE = 512
ADA_COL_TILE = 1536
MOE_BLOCK = 256


def _params(*sem, flags=None):
    return pltpu.CompilerParams(dimension_semantics=sem, vmem_limit_bytes=VMEM_LIMIT_BYTES, flags=flags)


def _rms(x, g):
    return x * lax.rsqrt(jnp.mean(x * x, axis=-1, keepdims=True) + EPS) * g


def _sigmoid(x):
    return 1.0 / (1.0 + jnp.exp(-x))


def _log_sigmoid(x):
    return jnp.minimum(x, 0.0) - jnp.log(1.0 + jnp.exp(-jnp.abs(x)))


def _split3(x):
    hi = x.astype(BF16)
    r1 = x - hi.astype(F32)
    mid = r1.astype(BF16)
    lo = (r1 - mid.astype(F32)).astype(BF16)
    return hi, mid, lo


def _pick_lane(x, idx):
    lane = lax.broadcasted_iota(jnp.int32, x.shape, 1)
    return jnp.sum(jnp.where(lane == idx, x, 0.0), axis=1, keepdims=True)


def _pick_row(x, idx):
    row = lax.broadcasted_iota(jnp.int32, x.shape, 0)
    return jnp.sum(jnp.where(row == idx, x, 0.0), axis=0, keepdims=True)


def _ada_kernel(c_ref, w_ref, b_ref, o_ref):
    c = c_ref[...]
    ca = c * _sigmoid(c)
    o_ref[0] = jnp.dot(ca, w_ref[0], precision=HIGHEST, preferred_element_type=F32) + b_ref[0]


def _ada(c, ada_w, ada_b):
    depth, d, d6 = ada_w.shape
    b = c.shape[0]
    rows = 8
    cp = jnp.zeros((rows, d), F32).at[:b].set(c)
    out = pl.pallas_call(
        _ada_kernel,
        out_shape=jax.ShapeDtypeStruct((depth, rows, d6), F32),
        grid=(depth, d6 // ADA_COL_TILE),
        in_specs=[
            pl.BlockSpec((rows, d), lambda l, j: (0, 0)),
            pl.BlockSpec((1, d, ADA_COL_TILE), lambda l, j: (l, 0, j)),
            pl.BlockSpec((1, 1, ADA_COL_TILE), lambda l, j: (l, 0, j)),
        ],
        out_specs=pl.BlockSpec((1, rows, ADA_COL_TILE), lambda l, j: (l, 0, j)),
        compiler_params=_params("parallel", "parallel"),
        name="ada_mod",
    )(cp, ada_w, ada_b.reshape(depth, 1, d6))
    return out[:, :b].reshape(depth, b, 6, d)


def _inproj_kernel(x_ref, mod_ref, g_ref, w_ref, wqt_ref, wvt_ref, wg_ref, wgt_ref, gb_ref, gbt_ref,
                   ak, mq, mk, mv, mo, aqt, avt, grow, gcol):
    mod = mod_ref[0]
    h = _rms(x_ref[...], g_ref[...]) * (1.0 + mod[1:2]) + mod[0:1]
    hb = h.astype(BF16)
    for s, o in enumerate((ak, mq, mk, mv, mo)):
        w = w_ref[:, s * HEAD_GROUP_W:(s + 1) * HEAD_GROUP_W]
        o[...] = jnp.dot(hb, w, preferred_element_type=F32).astype(BF16)
    nt = (((1,), (1,)), ((), ()))
    aqt[0, 0] = (lax.dot_general(wqt_ref[...], hb, nt, preferred_element_type=F32) * Q_SCALE).astype(BF16)
    avt[0, 0] = lax.dot_general(wvt_ref[...], hb, nt, preferred_element_type=F32).astype(BF16)
    grow[...] = jnp.dot(hb, wg_ref[...], preferred_element_type=F32) + gb_ref[...]
    gcol[0] = lax.dot_general(wgt_ref[...], hb, (((1,), (1,)), ((), ())),
                              preferred_element_type=F32) + gbt_ref[...]


def _inproj(xf, mod, g, w_main, w_qt, w_vt, wg, wgt, gb, gbt, batch, seq):
    n, d = xf.shape
    tm = ROW_TILE
    tpb = seq // tm
    n_row_slabs = N_SLABS - 2
    slab = jax.ShapeDtypeStruct((n, HEAD_GROUP_W), BF16)
    slab_t = jax.ShapeDtypeStruct((batch, tpb, HEAD_GROUP_W, tm), BF16)
    row_spec = pl.BlockSpec((tm, HEAD_GROUP_W), lambda i: (i, 0))
    t_spec = pl.BlockSpec((1, 1, HEAD_GROUP_W, tm), lambda i: (i // tpb, i % tpb, 0, 0))
    const = lambda i: (0, 0)
    return pl.pallas_call(
        _inproj_kernel,
        out_shape=[slab] * n_row_slabs + [slab_t, slab_t, jax.ShapeDtypeStruct((n, LANES), F32),
                                          jax.ShapeDtypeStruct((batch, 8, seq), F32)],
        grid=(n // tm,),
        in_specs=[
            pl.BlockSpec((tm, d), lambda i: (i, 0)),
            pl.BlockSpec((1, 6, d), lambda i: (i // tpb, 0, 0)),
            pl.BlockSpec((1, d), const),
            pl.BlockSpec(w_main.shape, const),
            pl.BlockSpec(w_qt.shape, const),
            pl.BlockSpec(w_vt.shape, const),
            pl.BlockSpec(wg.shape, const),
            pl.BlockSpec(wgt.shape, const),
            pl.BlockSpec(gb.shape, const),
            pl.BlockSpec(gbt.shape, const),
        ],
        out_specs=[row_spec] * n_row_slabs + [t_spec, t_spec,
                                              pl.BlockSpec((tm, LANES), lambda i: (i, 0)),
                                              pl.BlockSpec((1, 8, tm), lambda i: (i // tpb, 0, i % tpb))],
        compiler_params=_params("parallel"),
        name="in_proj",
    )(xf, mod, g, w_main, w_qt, w_vt, wg, wgt, gb, gbt)


ACC_ROWS = LANES + 16


def _attn_kernel(scal_ref, qt_ref, k_ref, vt_ref, bias_ref, lv_ref, g_ref, o_ref, m_sc, acc_sc,
                 sa_sc, sb_sc, mxa_sc, mxb_sc, *, blk):
    qi = pl.program_id(2)
    lam_init = scal_ref[0]

    feat = lax.broadcasted_iota(jnp.int32, (LANES, blk), 0)
    qt = qt_ref[0, 0]
    zero = jnp.zeros_like(qt)
    qz = (jnp.where(feat < A_DH, qt, zero), jnp.where(feat >= A_DH, qt, zero))
    ones_rows = jnp.ones((ACC_ROWS - LANES, blk), BF16)

    m_sc[...] = jnp.full(m_sc.shape, NEG, F32)
    acc_sc[...] = jnp.zeros(acc_sc.shape, F32)

    def scores(kj, s_sc, mx_sc, near=None):
        start = pl.multiple_of(kj * blk, blk)
        kt = k_ref[0, pl.ds(start, blk), :]
        for c in range(2):
            s = jnp.dot(kt, qz[c], preferred_element_type=F32)
            if near is not None:
                s = s + bias_ref[0, near]
            s_sc[c] = s
            mx_sc[c] = jnp.max(s, axis=0, keepdims=True)

    def accumulate(kj, s_sc, mx_sc):
        vext = jnp.concatenate([vt_ref[0, kj], ones_rows], axis=0)
        for c in range(2):
            m_old = m_sc[c]
            m_new = jnp.maximum(m_old, mx_sc[c])
            alpha = jnp.exp2(m_old - m_new)
            p = jnp.exp2(s_sc[c] - m_new).astype(BF16)
            pv = jnp.dot(vext, p, preferred_element_type=F32)
            acc_sc[c] = alpha * acc_sc[c] + pv
            m_sc[c] = m_new

    n_far = jnp.maximum(qi - 1, 0)
    scores(qi, sa_sc, mxa_sc, near=0)

    @pl.when(qi == 0)
    def _():
        accumulate(qi, sa_sc, mxa_sc)

    @pl.when(qi == 1)
    def _():
        scores(qi - 1, sb_sc, mxb_sc, near=1)
        accumulate(qi, sa_sc, mxa_sc)
        accumulate(qi - 1, sb_sc, mxb_sc)

    @pl.when(qi >= 2)
    def _():
        scores(qi - 1, sb_sc, mxb_sc, near=1)
        accumulate(qi, sa_sc, mxa_sc)
        scores(0, sa_sc, mxa_sc)
        accumulate(qi - 1, sb_sc, mxb_sc)

    def pair_body(u, carry):
        scores(2 * u + 1, sb_sc, mxb_sc)
        accumulate(2 * u, sa_sc, mxa_sc)
        scores(jnp.minimum(2 * u + 2, n_far - 1), sa_sc, mxa_sc)
        accumulate(2 * u + 1, sb_sc, mxb_sc)
        return carry

    lax.fori_loop(0, n_far // 2, pair_body, 0)

    @pl.when(n_far % 2 == 1)
    def _():
        accumulate(n_far - 1, sa_sc, mxa_sc)

    acc0 = acc_sc[0]
    acc1 = acc_sc[1]
    o0 = acc0[:LANES] * (1.0 / acc0[LANES:LANES + 1])
    o1 = acc1[:LANES] * (1.0 / acc1[LANES:LANES + 1])
    lv = lv_ref[...]
    lam = (jnp.exp(jnp.sum(lv[0:1] * lv[1:2], axis=1, keepdims=True))
           - jnp.exp(jnp.sum(lv[2:3] * lv[3:4], axis=1, keepdims=True)) + lam_init)
    a = o0 - lam * o1
    y = a * lax.rsqrt(jnp.mean(a * a, axis=0, keepdims=True) + EPS) * g_ref[...] * (1.0 - lam_init)
    o_ref[0] = y.T.astype(BF16)


def _attention(aqt, ak, avt, scal, bias_tiles, lam_vec, norm_g):
    batch, seq, _ = ak.shape
    blk = ATTN_BLOCK
    nk = seq // blk
    kernel = functools.partial(_attn_kernel, blk=blk)
    grid_spec = pltpu.PrefetchScalarGridSpec(
        num_scalar_prefetch=1,
        grid=(batch, A_HEADS, nk),
        in_specs=[
            pl.BlockSpec((1, 1, LANES, blk), lambda b, h, i, s: (b, i, h, 0)),
            pl.BlockSpec((1, seq, LANES), lambda b, h, i, s: (b, 0, h)),
            pl.BlockSpec((1, nk, LANES, blk), lambda b, h, i, s: (b, 0, h, 0)),
            pl.BlockSpec((1, 2, blk, blk), lambda b, h, i, s: (h, 0, 0, 0)),
            pl.BlockSpec(lam_vec.shape, lambda b, h, i, s: (0, 0)),
            pl.BlockSpec(norm_g.shape, lambda b, h, i, s: (0, 0)),
        ],
        out_specs=pl.BlockSpec((1, blk, LANES), lambda b, h, i, s: (b, i, h)),
        scratch_shapes=[pltpu.VMEM((2, 1, blk), F32), pltpu.VMEM((2, ACC_ROWS, blk), F32),
                        pltpu.VMEM((2, blk, blk), F32), pltpu.VMEM((2, blk, blk), F32),
                        pltpu.VMEM((2, 1, blk), F32), pltpu.VMEM((2, 1, blk), F32)],
    )
    return pl.pallas_call(
        kernel,
        out_shape=jax.ShapeDtypeStruct(ak.shape, BF16),
        grid_spec=grid_spec,
        compiler_params=_params("parallel", "parallel", "arbitrary"),
        name="diff_attn",
    )(scal, aqt, ak, avt, bias_tiles, lam_vec, norm_g)


def _t5_bucket(n):
    max_exact = N_BUCKETS // 2
    nf = jnp.maximum(n, 1).astype(F32)
    large = max_exact + (jnp.log(nf / max_exact) / math.log(MAX_DIST / max_exact)
                         * (N_BUCKETS - max_exact)).astype(jnp.int32)
    large = jnp.minimum(large, N_BUCKETS - 1)
    return jnp.where(n < max_exact, n, large)


def _bias_tables(rel_bias, seq):
    blk = ATTN_BLOCK
    far = np.arange(blk + 1, max(seq, blk + 2), dtype=np.float64)
    sat = (N_BUCKETS // 2) + np.log(far / (N_BUCKETS // 2)) / math.log(MAX_DIST / (N_BUCKETS // 2)) * (N_BUCKETS // 2)
    assert np.all(sat >= N_BUCKETS - 0.5), "far tiles need a saturated distance bucket"
    dist_bias = rel_bias[_t5_bucket(jnp.arange(seq, dtype=jnp.int32))].T.astype(F32)
    heads = dist_bias.shape[0]
    last = dist_bias[:, seq - 1:seq]
    padded = jnp.concatenate([jnp.full((heads, blk), NEG, F32), dist_bias,
                              jnp.broadcast_to(last, (heads, blk))], axis=1)
    tiles = []
    for d in (0, 1):
        base = blk + d * blk
        w = jnp.concatenate([padded[:, base:base + blk + 1], padded[:, base - blk + 1:base]], axis=1)
        t = jnp.tile(w, (1, blk))[:, :blk * (2 * blk - 1)].reshape(heads, blk, 2 * blk - 1)[:, :, :blk]
        tiles.append(t - last[:, :, None])
    return jnp.stack(tiles, axis=1) * LOG2E


def _mlstm_kernel(mq_ref, mk_ref, mv_ref, mo_ref, grow_ref, gcol_ref, cwq_ref, cwk_ref, cbq_ref, cbk_ref,
                  ng_ref, o_ref, xq_sc, xk_sc, c_sc, n_sc, m_sc, *, chunk, heads):
    L = chunk
    hp = pl.program_id(1)
    ci = pl.program_id(2)

    @pl.when(ci == 0)
    def _():
        xq_sc[:, 0:8, :] = jnp.zeros((heads, 8, LANES), F32)
        xk_sc[:, 0:8, :] = jnp.zeros((heads, 8, LANES), F32)
        c_sc[...] = jnp.zeros(c_sc.shape, F32)
        n_sc[...] = jnp.zeros(n_sc.shape, F32)
        m_sc[...] = jnp.zeros(m_sc.shape, F32)

    def conv_silu(x, sc, w, b):
        sc[8:8 + L, :] = x.astype(F32)
        y = b
        for j in range(CONV_K):
            y = y + sc[5 + j:5 + j + L, :] * w[j:j + 1]
        sc[0:8, :] = sc[L:L + 8, :]
        return y * _sigmoid(y)

    r = lax.broadcasted_iota(jnp.int32, (L, L), 0)
    cidx = lax.broadcasted_iota(jnp.int32, (L, L), 1)
    causal = r >= cidx
    lower = causal.astype(BF16)
    upper = (r <= cidx).astype(BF16)

    g_rows = grow_ref[...]
    cum_rows = sum(jnp.dot(lower, part, preferred_element_type=F32) for part in _split3(_log_sigmoid(g_rows)))
    g_cols = gcol_ref[0]
    cum_cols = sum(jnp.dot(part, upper, preferred_element_type=F32) for part in _split3(_log_sigmoid(g_cols)))

    for hh in range(heads):
        h = hp * heads + hh
        cols = slice(hh * LANES, (hh + 1) * LANES)
        q = conv_silu(mq_ref[0, :, cols], xq_sc.at[hh], cwq_ref[hh], cbq_ref[hh])
        k = conv_silu(mk_ref[0, :, cols], xk_sc.at[hh], cwk_ref[hh], cbk_ref[hh]) * (M_DH ** -0.5)
        v = mv_ref[0, :, cols]

        ig_col = _pick_lane(g_rows, h)
        b_col = _pick_lane(cum_rows, M_HEADS + h)
        ig_row = _pick_row(g_cols, h)
        b_row = _pick_row(cum_cols, M_HEADS + h)
        b_last = b_col[L - 1:L, :]
        m_prev = m_sc[hh]

        dlog = jnp.where(causal, b_col - b_row + ig_row, NEG)
        inter = b_col + m_prev
        m_t = jnp.maximum(inter, jnp.max(dlog, axis=1, keepdims=True))
        qb = q.astype(BF16)
        kb = k.astype(BF16)
        sw = (lax.dot_general(qb, kb, (((1,), (1,)), ((), ())), preferred_element_type=F32)
              * jnp.exp(dlog - m_t))
        dec = jnp.exp(inter - m_t)
        num = (dec * jnp.dot(qb, c_sc[hh].astype(BF16), preferred_element_type=F32)
               + jnp.dot(sw.astype(BF16), v, preferred_element_type=F32))
        den = dec * jnp.sum(q * n_sc[hh], axis=1, keepdims=True) + jnp.sum(sw, axis=1, keepdims=True)
        h_t = num * (1.0 / jnp.maximum(jnp.abs(den), jnp.exp(-m_t)))

        a_col = b_last - b_col + ig_col
        m_new = jnp.maximum(b_last + m_prev, jnp.max(a_col, axis=0, keepdims=True))
        decay = jnp.exp(b_last + m_prev - m_new)
        kw = k * jnp.exp(a_col - m_new)
        c_sc[hh] = decay * c_sc[hh] + lax.dot_general(kw.astype(BF16), v, (((0,), (0,)), ((), ())),
                                                      preferred_element_type=F32)
        n_sc[hh] = decay * n_sc[hh] + jnp.sum(kw, axis=0, keepdims=True)
        m_sc[hh] = m_new

        hm = _sigmoid(mo_ref[0, :, cols].astype(F32)) * h_t
        o_ref[0, :, cols] = _rms(hm, ng_ref[hh]).astype(BF16)


def _mlstm(mq, mk, mv, mo, grow, gcol, conv_w, conv_b, norm_g):
    batch, seq, _ = mq.shape
    L = MLSTM_CHUNK
    hps = MLSTM_HEADS_PER_STEP
    groups = M_HEADS // hps
    kernel = functools.partial(_mlstm_kernel, chunk=L, heads=hps)
    slab = pl.BlockSpec((1, L, hps * LANES), lambda b, h, c: (b, c, h))
    nchunk = seq // L
    return pl.pallas_call(
        kernel,
        out_shape=jax.ShapeDtypeStruct(mq.shape, BF16),
        grid=(batch, groups, nchunk),
        in_specs=[
            slab, slab, slab, slab,
            pl.BlockSpec((L, LANES), lambda b, h, c: (b * nchunk + c, 0)),
            pl.BlockSpec((1, 8, L), lambda b, h, c: (b, 0, c)),
            pl.BlockSpec((hps, CONV_K, LANES), lambda b, h, c: (h, 0, 0)),
            pl.BlockSpec((hps, CONV_K, LANES), lambda b, h, c: (groups + h, 0, 0)),
            pl.BlockSpec((hps, 1, LANES), lambda b, h, c: (h, 0, 0)),
            pl.BlockSpec((hps, 1, LANES), lambda b, h, c: (groups + h, 0, 0)),
            pl.BlockSpec((hps, 1, LANES), lambda b, h, c: (h, 0, 0)),
        ],
        out_specs=slab,
        scratch_shapes=[pltpu.VMEM((hps, L + 8, LANES), F32), pltpu.VMEM((hps, L + 8, LANES), F32),
                        pltpu.VMEM((hps, M_DH, M_DH), F32), pltpu.VMEM((hps, 1, M_DH), F32),
                        pltpu.VMEM((hps, 1, 1), F32)],
        compiler_params=_params("parallel", "parallel", "arbitrary"),
        name="mlstm",
    )(mq, mk, mv, mo, grow, gcol, conv_w, conv_w, conv_b, conv_b, norm_g)


def _outproj_kernel(*refs, moe):
    if moe:
        (a_ref, hm_ref, x_ref, mod_ref, ng_ref, wt_ref, wb_ref, rw_ref,
         x1_ref, h2_ref, gates_ref, pos_ref, post_ref, cnt_ref) = refs
    else:
        a_ref, hm_ref, x_ref, mod_ref, ng_ref, wt_ref, wb_ref, x1_ref, h2_ref = refs
    y = (jnp.dot(a_ref[...], wt_ref[...], preferred_element_type=F32)
         + jnp.dot(hm_ref[...], wb_ref[...], preferred_element_type=F32))
    mod = mod_ref[0]
    ng = ng_ref[...]
    x1 = x_ref[...] + mod[2:3] * _rms(y, ng[1:2])
    x1_ref[...] = x1
    h2 = _rms(x1, ng[2:3]) * (1.0 + mod[4:5]) + mod[3:4]
    h2_ref[...] = h2.astype(BF16)
    if moe:
        h_hi, h_mid, _ = _split3(h2)
        w_hi, w_mid, _ = _split3(rw_ref[...])
        logits = (jnp.dot(h_hi, w_hi, preferred_element_type=F32)
                  + jnp.dot(h_hi, w_mid, preferred_element_type=F32)
                  + jnp.dot(h_mid, w_hi, preferred_element_type=F32))
        lane = lax.broadcasted_iota(jnp.int32, logits.shape, 1)
        lg = jnp.where(lane < N_EXPERTS, logits, NEG)
        v1 = jnp.max(lg, axis=1, keepdims=True)
        i1 = jnp.min(jnp.where(lg == v1, lane, LANES), axis=1, keepdims=True)
        lg2 = jnp.where(lane == i1, NEG, lg)
        v2 = jnp.max(lg2, axis=1, keepdims=True)
        i2 = jnp.min(jnp.where(lg2 == v2, lane, LANES), axis=1, keepdims=True)
        e2 = jnp.exp(v2 - v1)
        w1 = 1.0 / (1.0 + e2)
        gates_ref[...] = jnp.where(lane == i1, w1, 0.0) + jnp.where(lane == i2, e2 * w1, 0.0)
        sel = jnp.logical_or(lane == i1, lane == i2)
        t = MOE_BLOCK
        r = lax.broadcasted_iota(jnp.int32, (t, t), 0)
        cidx = lax.broadcasted_iota(jnp.int32, (t, t), 1)
        lower = (r >= cidx).astype(BF16)
        for blk in range(logits.shape[0] // t):
            sel_b = sel[blk * t:(blk + 1) * t]
            rank = jnp.dot(lower, sel_b.astype(BF16), preferred_element_type=F32)
            pos = jnp.where(sel_b, rank, 0.0)
            pos_ref[blk * t:(blk + 1) * t, :] = pos
            post_ref[blk] = pos.T[:N_EXPERTS]
            cnt_ref[blk] = rank[t - 1:t, :]


def _outproj(a_out, h_m, xf, mod, ng, w_top, w_bot, router_w, batch, seq):
    n, d = xf.shape
    tm = ROW_TILE
    tpb = seq // tm
    moe = router_w is not None
    half = pl.BlockSpec((tm, HEAD_GROUP_W), lambda i: (i, 0))
    full = pl.BlockSpec((tm, d), lambda i: (i, 0))
    in_specs = [half, half, full,
                pl.BlockSpec((1, 6, d), lambda i: (i // tpb, 0, 0)),
                pl.BlockSpec(ng.shape, lambda i: (0, 0)),
                pl.BlockSpec(w_top.shape, lambda i: (0, 0)),
                pl.BlockSpec(w_bot.shape, lambda i: (0, 0))]
    args = [a_out, h_m, xf, mod, ng, w_top, w_bot]
    out_shape = [jax.ShapeDtypeStruct((n, d), F32), jax.ShapeDtypeStruct((n, d), BF16)]
    out_specs = [full, full]
    if moe:
        in_specs.append(pl.BlockSpec(router_w.shape, lambda i: (0, 0)))
        args.append(router_w)
        bpt = tm // MOE_BLOCK
        out_shape += [jax.ShapeDtypeStruct((n, LANES), F32), jax.ShapeDtypeStruct((n, LANES), F32),
                      jax.ShapeDtypeStruct((n // MOE_BLOCK, N_EXPERTS, MOE_BLOCK), F32),
                      jax.ShapeDtypeStruct((n // MOE_BLOCK, 1, LANES), F32)]
        out_specs += [pl.BlockSpec((tm, LANES), lambda i: (i, 0)), pl.BlockSpec((tm, LANES), lambda i: (i, 0)),
                      pl.BlockSpec((bpt, N_EXPERTS, MOE_BLOCK), lambda i: (i, 0, 0)),
                      pl.BlockSpec((bpt, 1, LANES), lambda i: (i, 0, 0))]
    return pl.pallas_call(
        functools.partial(_outproj_kernel, moe=moe),
        out_shape=out_shape,
        grid=(n // tm,),
        in_specs=in_specs,
        out_specs=out_specs,
        compiler_params=_params("parallel"),
        name="out_proj_moe" if moe else "out_proj",
    )(*args)


def _ffn_kernel(h_ref, x_ref, mod_ref, ng_ref, w1_ref, w3_ref, w2_ref, o_ref, acc_sc, *, n_col):
    f = pl.program_id(1)

    @pl.when(f == 0)
    def _():
        acc_sc[...] = jnp.zeros(acc_sc.shape, F32)

    hb = h_ref[...]
    a = jnp.dot(hb, w1_ref[0], preferred_element_type=F32)
    b = jnp.dot(hb, w3_ref[0], preferred_element_type=F32)
    hh = a * _sigmoid(a) * b
    acc_sc[...] += jnp.dot(hh.astype(BF16), w2_ref[0], preferred_element_type=F32)

    @pl.when(f == n_col - 1)
    def _():
        mod = mod_ref[0]
        o_ref[...] = x_ref[...] + mod[5:6] * _rms(acc_sc[...], ng_ref[3:4, :])


def _ffn(h2, x1, mod, ng, w1, w3, w2, layer, batch, seq):
    n, d = x1.shape
    dff = w1.shape[2]
    tm, tf = FFN_ROW_TILE, FFN_COL_TILE
    tpb = seq // tm
    n_col = dff // tf
    row = lambda i, f: (i, 0)
    return pl.pallas_call(
        functools.partial(_ffn_kernel, n_col=n_col),
        out_shape=jax.ShapeDtypeStruct((n, d), F32),
        grid=(n // tm, n_col),
        in_specs=[pl.BlockSpec((tm, d), row), pl.BlockSpec((tm, d), row),
                  pl.BlockSpec((1, 6, d), lambda i, f: (i // tpb, 0, 0)),
                  pl.BlockSpec(ng.shape, lambda i, f: (0, 0)),
                  pl.BlockSpec((1, d, tf), lambda i, f: (layer, 0, f)),
                  pl.BlockSpec((1, d, tf), lambda i, f: (layer, 0, f)),
                  pl.BlockSpec((1, tf, d), lambda i, f: (layer, f, 0))],
        out_specs=pl.BlockSpec((tm, d), row),
        scratch_shapes=[pltpu.VMEM((tm, d), F32)],
        compiler_params=_params("parallel", "arbitrary"),
        name="dense_ffn",
    )(h2, x1, mod, ng, w1, w3, w2)


def _moe_tables(cnt, batch, seq):
    t = MOE_BLOCK
    e_n = N_EXPERTS
    nbb = seq // t
    ntb = 2 * nbb + e_n
    c = cnt.reshape(batch, nbb, e_n)
    cum = jnp.cumsum(c, axis=1) - c
    tot = jnp.sum(c, axis=1)
    ntile = (tot + t - 1) // t
    tile_end = jnp.cumsum(ntile, axis=1)
    tile_base = tile_end - ntile
    tl = jnp.arange(ntb, dtype=jnp.int32)
    e_t = jnp.sum((tl[None, :, None] >= tile_end[:, None, :]).astype(jnp.int32), axis=2)
    valid = e_t < e_n
    e_c = jnp.minimum(e_t, e_n - 1)
    s0 = (tl[None, :] - jnp.take_along_axis(tile_base, e_c, axis=1)) * t
    bidx = jnp.arange(batch)[:, None]
    cum_sel = cum.transpose(0, 2, 1)[bidx, e_c]
    end_sel = cum_sel + c.transpose(0, 2, 1)[bidx, e_c]
    lo = jnp.sum((end_sel <= s0[..., None]).astype(jnp.int32), axis=2)
    hi = jnp.sum((cum_sel < (s0 + t)[..., None]).astype(jnp.int32), axis=2)

    def flat(a, tail):
        return jnp.concatenate([a.reshape(-1).astype(jnp.int32), jnp.full((1,), tail, jnp.int32)])

    tile_e = flat(e_c, e_n - 1)
    tile_v = flat(valid, 0)
    order = jnp.arange(tile_v.shape[0], dtype=jnp.int32)
    last_valid = lax.cummax(jnp.where(tile_v == 1, order, -1), axis=0)
    prev_valid = jnp.concatenate([jnp.full((1,), -1, jnp.int32), last_valid[:-1]])
    prev_e = jnp.where(prev_valid >= 0, tile_e[jnp.maximum(prev_valid, 0)], -1)
    tile_new = jnp.logical_and(tile_v == 1, tile_e != prev_e).astype(jnp.int32)
    tile_tables = (flat(jnp.broadcast_to(bidx, (batch, ntb)), batch - 1), tile_e, tile_v, tile_new,
                   flat(lo, 0), flat(hi, 0), flat(s0, 0), cum.reshape(-1).astype(jnp.int32))
    row0 = (bidx[:, :, None] * ntb + tile_base[:, None, :]) * t + cum
    seg_tables = ((row0 // t).reshape(-1).astype(jnp.int32), (row0 % t).reshape(-1).astype(jnp.int32))
    return tile_tables, seg_tables


def _moe_expert_kernel(tb_ref, te_ref, tv_ref, tn_ref, lo_ref, hi_ref, s0_ref, cum_ref,
                       h_ref, post_ref, w1_hbm, w3_hbm, w2_hbm, o_ref,
                       xg_sc, w1_sc, w3_sc, w2_sc, sem, *, layer, n_col, tf, nbb):
    ti = pl.program_id(0)
    t = MOE_BLOCK
    e = te_ref[ti]

    def chunk_copies(f):
        cols = pl.ds(f * tf, tf)
        return (pltpu.make_async_copy(w1_hbm.at[layer, e, :, cols], w1_sc.at[f], sem.at[0, f]),
                pltpu.make_async_copy(w3_hbm.at[layer, e, :, cols], w3_sc.at[f], sem.at[1, f]),
                pltpu.make_async_copy(w2_hbm.at[layer, e, cols, :], w2_sc.at[f], sem.at[2, f]))

    def gather_rows():
        b = tb_ref[ti]
        s0 = s0_ref[ti]
        xg_sc[...] = jnp.zeros(xg_sc.shape, F32)
        slot = lax.broadcasted_iota(jnp.int32, (t, t), 0).astype(F32)

        def gather(j, carry):
            blk = b * nbb + j
            rank = post_ref[blk, pl.ds(e, 1), :]
            shift = (cum_ref[blk * N_EXPERTS + e] - s0 - 1).astype(F32)
            onehot = jnp.logical_and(rank > 0.0, rank + shift == slot).astype(BF16)
            rows = h_ref[0, pl.ds(pl.multiple_of(j * t, t), t), :]
            xg_sc[...] += jnp.dot(onehot, rows, preferred_element_type=F32)
            return carry

        lax.fori_loop(lo_ref[ti], hi_ref[ti], gather, 0)
        return xg_sc[...].astype(BF16)

    def swiglu(x, wait_chunks):
        acc = jnp.zeros(xg_sc.shape, F32)
        for f in range(n_col):
            if wait_chunks:
                for cp in chunk_copies(f):
                    cp.wait()
            a = jnp.dot(x, w1_sc[f], preferred_element_type=F32)
            g = jnp.dot(x, w3_sc[f], preferred_element_type=F32)
            hh = (a * _sigmoid(a) * g).astype(BF16)
            acc = acc + jnp.dot(hh, w2_sc[f], preferred_element_type=F32)
        o_ref[...] = acc.astype(BF16)

    @pl.when(tv_ref[ti] == 0)
    def _():
        o_ref[...] = jnp.zeros(o_ref.shape, o_ref.dtype)

    @pl.when(jnp.logical_and(tv_ref[ti] == 1, tn_ref[ti] == 1))
    def _():
        for f in range(n_col):
            for cp in chunk_copies(f):
                cp.start()
        swiglu(gather_rows(), wait_chunks=True)

    @pl.when(jnp.logical_and(tv_ref[ti] == 1, tn_ref[ti] == 0))
    def _():
        swiglu(gather_rows(), wait_chunks=False)


def _moe_experts(h2, post, tables, w1, w3, w2, layer, batch, seq):
    n, d = h2.shape
    t = MOE_BLOCK
    nbb = seq // t
    n_tiles = tables[0].shape[0]
    dff = w1.shape[3]
    tf = MOE_COL_TILE
    n_col = dff // tf
    hbm = pl.BlockSpec(memory_space=pl.ANY)
    grid_spec = pltpu.PrefetchScalarGridSpec(
        num_scalar_prefetch=len(tables),
        grid=(n_tiles,),
        in_specs=[
            pl.BlockSpec((1, seq, d), lambda i, tb, *_: (tb[i], 0, 0), pipeline_mode=pl.Buffered(1)),
            pl.BlockSpec(post.shape, lambda i, *_: (0, 0, 0)),
            hbm, hbm, hbm,
        ],
        out_specs=pl.BlockSpec((t, d), lambda i, *_: (i, 0)),
        scratch_shapes=[pltpu.VMEM((t, d), F32),
                        pltpu.VMEM((n_col, d, tf), BF16), pltpu.VMEM((n_col, d, tf), BF16),
                        pltpu.VMEM((n_col, tf, d), BF16), pltpu.SemaphoreType.DMA((3, n_col))],
    )
    return pl.pallas_call(
        functools.partial(_moe_expert_kernel, layer=layer, n_col=n_col, tf=tf, nbb=nbb),
        out_shape=jax.ShapeDtypeStruct((n_tiles * t, d), BF16),
        grid_spec=grid_spec,
        compiler_params=_params("arbitrary"),
        name="moe_experts",
    )(*tables, h2.reshape(batch, seq, d), post, w1, w3, w2)


def _moe_combine_kernel(st_ref, so_ref, *refs):
    y_refs = refs[:2 * N_EXPERTS]
    pos_ref, gates_ref, x_ref, mod_ref, ng_ref, o_ref = refs[2 * N_EXPERTS:]
    i = pl.program_id(0)
    t = MOE_BLOCK
    pos = pos_ref[...]
    gates = gates_ref[...]
    slot = lax.broadcasted_iota(jnp.int32, (t, 2 * t), 1).astype(F32)
    acc = jnp.zeros(x_ref.shape, F32)
    for e in range(N_EXPERTS):
        rank = pos[:, e:e + 1]
        off = so_ref[i * N_EXPERTS + e].astype(F32)
        onehot = jnp.logical_and(rank > 0.0, rank - 1.0 + off == slot).astype(BF16)
        ys = jnp.concatenate([y_refs[2 * e][...], y_refs[2 * e + 1][...]], axis=0)
        acc = acc + gates[:, e:e + 1] * jnp.dot(onehot, ys, preferred_element_type=F32)
    mod = mod_ref[0]
    o_ref[...] = x_ref[...] + mod[5:6] * _rms(acc, ng_ref[3:4, :])


def _moe_combine(ys, pos, gates, x1, mod, ng, seg_tables, batch, seq):
    n, d = x1.shape
    t = MOE_BLOCK
    nbb = seq // t
    y_specs = []
    for e in range(N_EXPERTS):
        for nxt in range(2):
            y_specs.append(pl.BlockSpec((t, d), functools.partial(
                lambda i, st, so, e, nxt: (st[i * N_EXPERTS + e] + nxt, 0), e=e, nxt=nxt)))
    grid_spec = pltpu.PrefetchScalarGridSpec(
        num_scalar_prefetch=2,
        grid=(n // t,),
        in_specs=y_specs + [
            pl.BlockSpec((t, LANES), lambda i, st, so: (i, 0)),
            pl.BlockSpec((t, LANES), lambda i, st, so: (i, 0)),
            pl.BlockSpec((t, d), lambda i, st, so: (i, 0)),
            pl.BlockSpec((1, 6, d), lambda i, st, so: (i // nbb, 0, 0)),
            pl.BlockSpec(ng.shape, lambda i, st, so: (0, 0)),
        ],
        out_specs=pl.BlockSpec((t, d), lambda i, st, so: (i, 0)),
    )
    return pl.pallas_call(
        _moe_combine_kernel,
        out_shape=jax.ShapeDtypeStruct((n, d), F32),
        grid_spec=grid_spec,
        compiler_params=_params("parallel"),
        name="moe_combine",
    )(*seg_tables, *([ys] * (2 * N_EXPERTS)), pos, gates, x1, mod, ng)


@jax.jit
def _forward(x, c, rel_bias, ada_w, ada_b, norm_g, w_in, gate_b, conv_w, conv_b, lam_vec,
             attn_norm_g, mlstm_norm_g, w_out, ffn_w1, ffn_w3, ffn_w2, router_w, moe_w1, moe_w3, moe_w2):
    batch, seq, d = x.shape
    depth = ada_w.shape[0]
    n = batch * seq
    n_main = N_SLABS * HEAD_GROUP_W
    assert seq % ATTN_BLOCK == 0 and seq % FFN_ROW_TILE == 0 and seq % MLSTM_CHUNK == 0
    assert ROW_TILE % MOE_BLOCK == 0
    assert seq >= 2 * ATTN_BLOCK and ROW_TILE == ATTN_BLOCK
    assert w_in.shape[2] == n_main + 2 * M_HEADS

    mod_all = _ada(c, ada_w, ada_b)
    bias_tiles = _bias_tables(rel_bias, seq)

    ffn_w = tuple(w.astype(BF16) for w in (ffn_w1, ffn_w3, ffn_w2))
    moe_w = tuple(w.astype(BF16) for w in (moe_w1, moe_w3, moe_w2))
    xf = x.reshape(n, d)
    for i in range(depth):
        mod = mod_all[i]
        ng = norm_g[i]
        lam_init = 0.8 - 0.6 * math.exp(-0.3 * i)

        w_gate = w_in[i][:, n_main:]
        w_bf = w_in[i][:, :n_main].astype(BF16)
        gw = HEAD_GROUP_W
        slabs = _inproj(
            xf, mod, ng[0:1], jnp.concatenate([w_bf[:, gw:2 * gw], w_bf[:, 3 * gw:]], axis=1),
            w_bf[:, :gw].T, w_bf[:, 2 * gw:3 * gw].T,
            jnp.zeros((d, LANES), BF16).at[:, :2 * M_HEADS].set(w_gate.astype(BF16)),
            w_gate.T.astype(BF16),
            jnp.zeros((1, LANES), F32).at[0, :2 * M_HEADS].set(gate_b[i].reshape(-1)),
            gate_b[i].reshape(2 * M_HEADS, 1), batch, seq)
        ak, mq, mk, mv, mo = (s.reshape(batch, seq, HEAD_GROUP_W) for s in slabs[:N_SLABS - 2])
        aqt, avt, grow, gcol = slabs[N_SLABS - 2:]

        scal = jnp.full((1,), lam_init, F32)
        a_out = _attention(aqt, ak, avt, scal, bias_tiles, lam_vec[i], attn_norm_g[i].reshape(LANES, 1))

        h_m = _mlstm(mq, mk, mv, mo, grow, gcol,
                     conv_w[i].reshape(CONV_K, 2 * M_HEADS, LANES).transpose(1, 0, 2),
                     conv_b[i].reshape(2 * M_HEADS, 1, LANES),
                     mlstm_norm_g[i].reshape(M_HEADS, 1, LANES))

        w_o = w_out[i].astype(BF16)
        j = i // 2
        if i % 2 == 0:
            x1, h2 = _outproj(a_out.reshape(n, -1), h_m.reshape(n, -1), xf, mod, ng,
                              w_o[:HEAD_GROUP_W], w_o[HEAD_GROUP_W:], None, batch, seq)
            xf = _ffn(h2, x1, mod, ng, *ffn_w, j, batch, seq)
        else:
            rw = jnp.zeros((d, LANES), F32).at[:, :N_EXPERTS].set(router_w[j])
            x1, h2, gates, pos, post, cnt = _outproj(a_out.reshape(n, -1), h_m.reshape(n, -1), xf, mod, ng,
                                                     w_o[:HEAD_GROUP_W], w_o[HEAD_GROUP_W:], rw, batch, seq)
            tile_tables, seg_tables = _moe_tables(cnt[:, 0, :N_EXPERTS].astype(jnp.int32), batch, seq)
            ys = _moe_experts(h2, post, tile_tables, *moe_w, j, batch, seq)
            xf = _moe_combine(ys, pos, gates, x1, mod, ng, seg_tables, batch, seq)
    return xf.reshape(batch, seq, d)


def kernel(x, c, rel_bias, ada_w, ada_b, norm_g, w_in, gate_b, conv_w, conv_b, lam_vec, attn_norm_g,
           mlstm_norm_g, w_out, ffn_w1, ffn_w3, ffn_w2, router_w, moe_w1, moe_w3, moe_w2):
    return _forward(x, c, rel_bias, ada_w, ada_b, norm_g, w_in, gate_b, conv_w, conv_b, lam_vec,
                    attn_norm_g, mlstm_norm_g, w_out, ffn_w1, ffn_w3, ffn_w2, router_w, moe_w1, moe_w3, moe_w2)
```

```python
import functools
import math

import numpy as np
import jax
import jax.numpy as jnp
from jax import lax
from jax.experimental import pallas as pl
from jax.experimental.pallas import tpu as pltpu

F32 = jnp.float32
BF16 = jnp.bfloat16
HIGHEST = lax.Precision.HIGHEST

A_HEADS = 4
A_DH = 64
M_HEADS = 4
M_DH = 128
CONV_K = 4
N_BUCKETS = 32
MAX_DIST = 128
N_EXPERTS = 8
EPS = 1e-6

LANES = 128
HEAD_GROUP_W = 512
N_SLABS = 7
NEG = -1e30
LOG2E = math.log2(math.e)
Q_SCALE = A_DH ** -0.5 * LOG2E
VMEM_LIMIT_BYTES = 56 * 1024 * 1024

ATTN_BLOCK = 512
MLSTM_CHUNK = 256
MLSTM_HEADS_PER_STEP = 4
ROW_TILE = 512
FFN_ROW_TILE = 512
FFN_COL_TILE = 1792
MOE_COL_TILE = 512
ADA_COL_TILE = 1536
MOE_BLOCK = 256


def _params(*sem, flags=None):
    return pltpu.CompilerParams(dimension_semantics=sem, vmem_limit_bytes=VMEM_LIMIT_BYTES, flags=flags)


def _rms(x, g):
    return x * lax.rsqrt(jnp.mean(x * x, axis=-1, keepdims=True) + EPS) * g


def _sigmoid(x):
    return 1.0 / (1.0 + jnp.exp(-x))


def _log_sigmoid(x):
    return jnp.minimum(x, 0.0) - jnp.log(1.0 + jnp.exp(-jnp.abs(x)))


def _split3(x):
    hi = x.astype(BF16)
    r1 = x - hi.astype(F32)
    mid = r1.astype(BF16)
    lo = (r1 - mid.astype(F32)).astype(BF16)
    return hi, mid, lo


def _pick_lane(x, idx):
    lane = lax.broadcasted_iota(jnp.int32, x.shape, 1)
    return jnp.sum(jnp.where(lane == idx, x, 0.0), axis=1, keepdims=True)


def _pick_row(x, idx):
    row = lax.broadcasted_iota(jnp.int32, x.shape, 0)
    return jnp.sum(jnp.where(row == idx, x, 0.0), axis=0, keepdims=True)


def _ada_kernel(c_ref, w_ref, b_ref, o_ref):
    c = c_ref[...]
    ca = c * _sigmoid(c)
    o_ref[0] = jnp.dot(ca, w_ref[0], precision=HIGHEST, preferred_element_type=F32) + b_ref[0]


def _ada(c, ada_w, ada_b):
    depth, d, d6 = ada_w.shape
    b = c.shape[0]
    rows = 8
    cp = jnp.zeros((rows, d), F32).at[:b].set(c)
    out = pl.pallas_call(
        _ada_kernel,
        out_shape=jax.ShapeDtypeStruct((depth, rows, d6), F32),
        grid=(depth, d6 // ADA_COL_TILE),
        in_specs=[
            pl.BlockSpec((rows, d), lambda l, j: (0, 0)),
            pl.BlockSpec((1, d, ADA_COL_TILE), lambda l, j: (l, 0, j)),
            pl.BlockSpec((1, 1, ADA_COL_TILE), lambda l, j: (l, 0, j)),
        ],
        out_specs=pl.BlockSpec((1, rows, ADA_COL_TILE), lambda l, j: (l, 0, j)),
        compiler_params=_params("parallel", "parallel"),
        name="ada_mod",
    )(cp, ada_w, ada_b.reshape(depth, 1, d6))
    return out[:, :b].reshape(depth, b, 6, d)


def _inproj_kernel(x_ref, mod_ref, g_ref, w_ref, wqt_ref, wvt_ref, wg_ref, wgt_ref, gb_ref, gbt_ref,
                   ak, mq, mk, mv, mo, aqt, avt, grow, gcol):
    mod = mod_ref[0]
    h = _rms(x_ref[...], g_ref[...]) * (1.0 + mod[1:2]) + mod[0:1]
    hb = h.astype(BF16)
    for s, o in enumerate((ak, mq, mk, mv, mo)):
        w = w_ref[:, s * HEAD_GROUP_W:(s + 1) * HEAD_GROUP_W]
        o[...] = jnp.dot(hb, w, preferred_element_type=F32).astype(BF16)
    nt = (((1,), (1,)), ((), ()))
    aqt[0, 0] = (lax.dot_general(wqt_ref[...], hb, nt, preferred_element_type=F32) * Q_SCALE).astype(BF16)
    avt[0, 0] = lax.dot_general(wvt_ref[...], hb, nt, preferred_element_type=F32).astype(BF16)
    grow[...] = jnp.dot(hb, wg_ref[...], preferred_element_type=F32) + gb_ref[...]
    gcol[0] = lax.dot_general(wgt_ref[...], hb, (((1,), (1,)), ((), ())),
                              preferred_element_type=F32) + gbt_ref[...]


def _inproj(xf, mod, g, w_main, w_qt, w_vt, wg, wgt, gb, gbt, batch, seq):
    n, d = xf.shape
    tm = ROW_TILE
    tpb = seq // tm
    n_row_slabs = N_SLABS - 2
    slab = jax.ShapeDtypeStruct((n, HEAD_GROUP_W), BF16)
    slab_t = jax.ShapeDtypeStruct((batch, tpb, HEAD_GROUP_W, tm), BF16)
    row_spec = pl.BlockSpec((tm, HEAD_GROUP_W), lambda i: (i, 0))
    t_spec = pl.BlockSpec((1, 1, HEAD_GROUP_W, tm), lambda i: (i // tpb, i % tpb, 0, 0))
    const = lambda i: (0, 0)
    return pl.pallas_call(
        _inproj_kernel,
        out_shape=[slab] * n_row_slabs + [slab_t, slab_t, jax.ShapeDtypeStruct((n, LANES), F32),
                                          jax.ShapeDtypeStruct((batch, 8, seq), F32)],
        grid=(n // tm,),
        in_specs=[
            pl.BlockSpec((tm, d), lambda i: (i, 0)),
            pl.BlockSpec((1, 6, d), lambda i: (i // tpb, 0, 0)),
            pl.BlockSpec((1, d), const),
            pl.BlockSpec(w_main.shape, const),
            pl.BlockSpec(w_qt.shape, const),
            pl.BlockSpec(w_vt.shape, const),
            pl.BlockSpec(wg.shape, const),
            pl.BlockSpec(wgt.shape, const),
            pl.BlockSpec(gb.shape, const),
            pl.BlockSpec(gbt.shape, const),
        ],
        out_specs=[row_spec] * n_row_slabs + [t_spec, t_spec,
                                              pl.BlockSpec((tm, LANES), lambda i: (i, 0)),
                                              pl.BlockSpec((1, 8, tm), lambda i: (i // tpb, 0, i % tpb))],
        compiler_params=_params("parallel"),
        name="in_proj",
    )(xf, mod, g, w_main, w_qt, w_vt, wg, wgt, gb, gbt)


ACC_ROWS = LANES + 16


def _attn_kernel(scal_ref, qt_ref, k_ref, vt_ref, bias_ref, lv_ref, g_ref, o_ref, m_sc, acc_sc,
                 sa_sc, sb_sc, mxa_sc, mxb_sc, *, blk):
    qi = pl.program_id(2)
    lam_init = scal_ref[0]

    feat = lax.broadcasted_iota(jnp.int32, (LANES, blk), 0)
    qt = qt_ref[0, 0]
    zero = jnp.zeros_like(qt)
    qz = (jnp.where(feat < A_DH, qt, zero), jnp.where(feat >= A_DH, qt, zero))
    ones_rows = jnp.ones((ACC_ROWS - LANES, blk), BF16)

    m_sc[...] = jnp.full(m_sc.shape, NEG, F32)
    acc_sc[...] = jnp.zeros(acc_sc.shape, F32)

    def scores(kj, s_sc, mx_sc, near=None):
        start = pl.multiple_of(kj * blk, blk)
        kt = k_ref[0, pl.ds(start, blk), :]
        for c in range(2):
            s = jnp.dot(kt, qz[c], preferred_element_type=F32)
            if near is not None:
                s = s + bias_ref[0, near]
            s_sc[c] = s
            mx_sc[c] = jnp.max(s, axis=0, keepdims=True)

    def accumulate(kj, s_sc, mx_sc):
        vext = jnp.concatenate([vt_ref[0, kj], ones_rows], axis=0)
        for c in range(2):
            m_old = m_sc[c]
            m_new = jnp.maximum(m_old, mx_sc[c])
            alpha = jnp.exp2(m_old - m_new)
            p = jnp.exp2(s_sc[c] - m_new).astype(BF16)
            pv = jnp.dot(vext, p, preferred_element_type=F32)
            acc_sc[c] = alpha * acc_sc[c] + pv
            m_sc[c] = m_new

    n_far = jnp.maximum(qi - 1, 0)
    scores(qi, sa_sc, mxa_sc, near=0)

    @pl.when(qi == 0)
    def _():
        accumulate(qi, sa_sc, mxa_sc)

    @pl.when(qi == 1)
    def _():
        scores(qi - 1, sb_sc, mxb_sc, near=1)
        accumulate(qi, sa_sc, mxa_sc)
        accumulate(qi - 1, sb_sc, mxb_sc)

    @pl.when(qi >= 2)
    def _():
        scores(qi - 1, sb_sc, mxb_sc, near=1)
        accumulate(qi, sa_sc, mxa_sc)
        scores(0, sa_sc, mxa_sc)
        accumulate(qi - 1, sb_sc, mxb_sc)

    def pair_body(u, carry):
        scores(2 * u + 1, sb_sc, mxb_sc)
        accumulate(2 * u, sa_sc, mxa_sc)
        scores(jnp.minimum(2 * u + 2, n_far - 1), sa_sc, mxa_sc)
        accumulate(2 * u + 1, sb_sc, mxb_sc)
        return carry

    lax.fori_loop(0, n_far // 2, pair_body, 0)

    @pl.when(n_far % 2 == 1)
    def _():
        accumulate(n_far - 1, sa_sc, mxa_sc)

    acc0 = acc_sc[0]
    acc1 = acc_sc[1]
    o0 = acc0[:LANES] * (1.0 / acc0[LANES:LANES + 1])
    o1 = acc1[:LANES] * (1.0 / acc1[LANES:LANES + 1])
    lv = lv_ref[...]
    lam = (jnp.exp(jnp.sum(lv[0:1] * lv[1:2], axis=1, keepdims=True))
           - jnp.exp(jnp.sum(lv[2:3] * lv[3:4], axis=1, keepdims=True)) + lam_init)
    a = o0 - lam * o1
    y = a * lax.rsqrt(jnp.mean(a * a, axis=0, keepdims=True) + EPS) * g_ref[...] * (1.0 - lam_init)
    o_ref[0] = y.T.astype(BF16)


def _attention(aqt, ak, avt, scal, bias_tiles, lam_vec, norm_g):
    batch, seq, _ = ak.shape
    blk = ATTN_BLOCK
    nk = seq // blk
    kernel = functools.partial(_attn_kernel, blk=blk)
    grid_spec = pltpu.PrefetchScalarGridSpec(
        num_scalar_prefetch=1,
        grid=(batch, A_HEADS, nk),
        in_specs=[
            pl.BlockSpec((1, 1, LANES, blk), lambda b, h, i, s: (b, i, h, 0)),
            pl.BlockSpec((1, seq, LANES), lambda b, h, i, s: (b, 0, h)),
            pl.BlockSpec((1, nk, LANES, blk), lambda b, h, i, s: (b, 0, h, 0)),
            pl.BlockSpec((1, 2, blk, blk), lambda b, h, i, s: (h, 0, 0, 0)),
            pl.BlockSpec(lam_vec.shape, lambda b, h, i, s: (0, 0)),
            pl.BlockSpec(norm_g.shape, lambda b, h, i, s: (0, 0)),
        ],
        out_specs=pl.BlockSpec((1, blk, LANES), lambda b, h, i, s: (b, i, h)),
        scratch_shapes=[pltpu.VMEM((2, 1, blk), F32), pltpu.VMEM((2, ACC_ROWS, blk), F32),
                        pltpu.VMEM((2, blk, blk), F32), pltpu.VMEM((2, blk, blk), F32),
                        pltpu.VMEM((2, 1, blk), F32), pltpu.VMEM((2, 1, blk), F32)],
    )
    return pl.pallas_call(
        kernel,
        out_shape=jax.ShapeDtypeStruct(ak.shape, BF16),
        grid_spec=grid_spec,
        compiler_params=_params("parallel", "parallel", "arbitrary"),
        name="diff_attn",
    )(scal, aqt, ak, avt, bias_tiles, lam_vec, norm_g)


def _t5_bucket(n):
    max_exact = N_BUCKETS // 2
    nf = jnp.maximum(n, 1).astype(F32)
    large = max_exact + (jnp.log(nf / max_exact) / math.log(MAX_DIST / max_exact)
                         * (N_BUCKETS - max_exact)).astype(jnp.int32)
    large = jnp.minimum(large, N_BUCKETS - 1)
    return jnp.where(n < max_exact, n, large)


def _bias_tables(rel_bias, seq):
    blk = ATTN_BLOCK
    far = np.arange(blk + 1, max(seq, blk + 2), dtype=np.float64)
    sat = (N_BUCKETS // 2) + np.log(far / (N_BUCKETS // 2)) / math.log(MAX_DIST / (N_BUCKETS // 2)) * (N_BUCKETS // 2)
    assert np.all(sat >= N_BUCKETS - 0.5), "far tiles need a saturated distance bucket"
    dist_bias = rel_bias[_t5_bucket(jnp.arange(seq, dtype=jnp.int32))].T.astype(F32)
    heads = dist_bias.shape[0]
    last = dist_bias[:, seq - 1:seq]
    padded = jnp.concatenate([jnp.full((heads, blk), NEG, F32), dist_bias,
                              jnp.broadcast_to(last, (heads, blk))], axis=1)
    tiles = []
    for d in (0, 1):
        base = blk + d * blk
        w = jnp.concatenate([padded[:, base:base + blk + 1], padded[:, base - blk + 1:base]], axis=1)
        t = jnp.tile(w, (1, blk))[:, :blk * (2 * blk - 1)].reshape(heads, blk, 2 * blk - 1)[:, :, :blk]
        tiles.append(t - last[:, :, None])
    return jnp.stack(tiles, axis=1) * LOG2E


def _mlstm_kernel(mq_ref, mk_ref, mv_ref, mo_ref, grow_ref, gcol_ref, cwq_ref, cwk_ref, cbq_ref, cbk_ref,
                  ng_ref, o_ref, xq_sc, xk_sc, c_sc, n_sc, m_sc, *, chunk, heads):
    L = chunk
    hp = pl.program_id(1)
    ci = pl.program_id(2)

    @pl.when(ci == 0)
    def _():
        xq_sc[:, 0:8, :] = jnp.zeros((heads, 8, LANES), F32)
        xk_sc[:, 0:8, :] = jnp.zeros((heads, 8, LANES), F32)
        c_sc[...] = jnp.zeros(c_sc.shape, F32)
        n_sc[...] = jnp.zeros(n_sc.shape, F32)
        m_sc[...] = jnp.zeros(m_sc.shape, F32)

    def conv_silu(x, sc, w, b):
        sc[8:8 + L, :] = x.astype(F32)
        y = b
        for j in range(CONV_K):
            y = y + sc[5 + j:5 + j + L, :] * w[j:j + 1]
        sc[0:8, :] = sc[L:L + 8, :]
        return y * _sigmoid(y)

    r = lax.broadcasted_iota(jnp.int32, (L, L), 0)
    cidx = lax.broadcasted_iota(jnp.int32, (L, L), 1)
    causal = r >= cidx
    lower = causal.astype(BF16)
    upper = (r <= cidx).astype(BF16)

    g_rows = grow_ref[...]
    cum_rows = sum(jnp.dot(lower, part, preferred_element_type=F32) for part in _split3(_log_sigmoid(g_rows)))
    g_cols = gcol_ref[0]
    cum_cols = sum(jnp.dot(part, upper, preferred_element_type=F32) for part in _split3(_log_sigmoid(g_cols)))

    for hh in range(heads):
        h = hp * heads + hh
        cols = slice(hh * LANES, (hh + 1) * LANES)
        q = conv_silu(mq_ref[0, :, cols], xq_sc.at[hh], cwq_ref[hh], cbq_ref[hh])
        k = conv_silu(mk_ref[0, :, cols], xk_sc.at[hh], cwk_ref[hh], cbk_ref[hh]) * (M_DH ** -0.5)
        v = mv_ref[0, :, cols]

        ig_col = _pick_lane(g_rows, h)
        b_col = _pick_lane(cum_rows, M_HEADS + h)
        ig_row = _pick_row(g_cols, h)
        b_row = _pick_row(cum_cols, M_HEADS + h)
        b_last = b_col[L - 1:L, :]
        m_prev = m_sc[hh]

        dlog = jnp.where(causal, b_col - b_row + ig_row, NEG)
        inter = b_col + m_prev
        m_t = jnp.maximum(inter, jnp.max(dlog, axis=1, keepdims=True))
        qb = q.astype(BF16)
        kb = k.astype(BF16)
        sw = (lax.dot_general(qb, kb, (((1,), (1,)), ((), ())), preferred_element_type=F32)
              * jnp.exp(dlog - m_t))
        dec = jnp.exp(inter - m_t)
        num = (dec * jnp.dot(qb, c_sc[hh].astype(BF16), preferred_element_type=F32)
               + jnp.dot(sw.astype(BF16), v, preferred_element_type=F32))
        den = dec * jnp.sum(q * n_sc[hh], axis=1, keepdims=True) + jnp.sum(sw, axis=1, keepdims=True)
        h_t = num * (1.0 / jnp.maximum(jnp.abs(den), jnp.exp(-m_t)))

        a_col = b_last - b_col + ig_col
        m_new = jnp.maximum(b_last + m_prev, jnp.max(a_col, axis=0, keepdims=True))
        decay = jnp.exp(b_last + m_prev - m_new)
        kw = k * jnp.exp(a_col - m_new)
        c_sc[hh] = decay * c_sc[hh] + lax.dot_general(kw.astype(BF16), v, (((0,), (0,)), ((), ())),
                                                      preferred_element_type=F32)
        n_sc[hh] = decay * n_sc[hh] + jnp.sum(kw, axis=0, keepdims=True)
        m_sc[hh] = m_new

        hm = _sigmoid(mo_ref[0, :, cols].astype(F32)) * h_t
        o_ref[0, :, cols] = _rms(hm, ng_ref[hh]).astype(BF16)


def _mlstm(mq, mk, mv, mo, grow, gcol, conv_w, conv_b, norm_g):
    batch, seq, _ = mq.shape
    L = MLSTM_CHUNK
    hps = MLSTM_HEADS_PER_STEP
    groups = M_HEADS // hps
    kernel = functools.partial(_mlstm_kernel, chunk=L, heads=hps)
    slab = pl.BlockSpec((1, L, hps * LANES), lambda b, h, c: (b, c, h))
    nchunk = seq // L
    return pl.pallas_call(
        kernel,
        out_shape=jax.ShapeDtypeStruct(mq.shape, BF16),
        grid=(batch, groups, nchunk),
        in_specs=[
            slab, slab, slab, slab,
            pl.BlockSpec((L, LANES), lambda b, h, c: (b * nchunk + c, 0)),
            pl.BlockSpec((1, 8, L), lambda b, h, c: (b, 0, c)),
            pl.BlockSpec((hps, CONV_K, LANES), lambda b, h, c: (h, 0, 0)),
            pl.BlockSpec((hps, CONV_K, LANES), lambda b, h, c: (groups + h, 0, 0)),
            pl.BlockSpec((hps, 1, LANES), lambda b, h, c: (h, 0, 0)),
            pl.BlockSpec((hps, 1, LANES), lambda b, h, c: (groups + h, 0, 0)),
            pl.BlockSpec((hps, 1, LANES), lambda b, h, c: (h, 0, 0)),
        ],
        out_specs=slab,
        scratch_shapes=[pltpu.VMEM((hps, L + 8, LANES), F32), pltpu.VMEM((hps, L + 8, LANES), F32),
                        pltpu.VMEM((hps, M_DH, M_DH), F32), pltpu.VMEM((hps, 1, M_DH), F32),
                        pltpu.VMEM((hps, 1, 1), F32)],
        compiler_params=_params("parallel", "parallel", "arbitrary"),
        name="mlstm",
    )(mq, mk, mv, mo, grow, gcol, conv_w, conv_w, conv_b, conv_b, norm_g)


def _outproj_kernel(a_ref, hm_ref, x_ref, mod_ref, ng_ref, wt_ref, wb_ref, rw_ref,
                    x1_ref, h2_ref, gates_ref, pos_ref, post_ref, cnt_ref):
    y = (jnp.dot(a_ref[...], wt_ref[...], preferred_element_type=F32)
         + jnp.dot(hm_ref[...], wb_ref[...], preferred_element_type=F32))
    mod = mod_ref[0]
    ng = ng_ref[...]
    x1 = x_ref[...] + mod[2:3] * _rms(y, ng[1:2])
    x1_ref[...] = x1
    h2 = _rms(x1, ng[2:3]) * (1.0 + mod[4:5]) + mod[3:4]
    h2_ref[...] = h2.astype(BF16)
    h_hi, h_mid, _ = _split3(h2)
    w_hi, w_mid, _ = _split3(rw_ref[...])
    logits = (jnp.dot(h_hi, w_hi, preferred_element_type=F32)
              + jnp.dot(h_hi, w_mid, preferred_element_type=F32)
              + jnp.dot(h_mid, w_hi, preferred_element_type=F32))
    lane = lax.broadcasted_iota(jnp.int32, logits.shape, 1)
    lg = jnp.where(lane < N_EXPERTS, logits, NEG)
    v1 = jnp.max(lg, axis=1, keepdims=True)
    i1 = jnp.min(jnp.where(lg == v1, lane, LANES), axis=1, keepdims=True)
    lg2 = jnp.where(lane == i1, NEG, lg)
    v2 = jnp.max(lg2, axis=1, keepdims=True)
    i2 = jnp.min(jnp.where(lg2 == v2, lane, LANES), axis=1, keepdims=True)
    e2 = jnp.exp(v2 - v1)
    w1 = 1.0 / (1.0 + e2)
    gates_ref[...] = jnp.where(lane == i1, w1, 0.0) + jnp.where(lane == i2, e2 * w1, 0.0)
    sel = jnp.logical_or(lane == i1, lane == i2)
    t = MOE_BLOCK
    r = lax.broadcasted_iota(jnp.int32, (t, t), 0)
    cidx = lax.broadcasted_iota(jnp.int32, (t, t), 1)
    lower = (r >= cidx).astype(BF16)
    for blk in range(logits.shape[0] // t):
        sel_b = sel[blk * t:(blk + 1) * t]
        rank = jnp.dot(lower, sel_b.astype(BF16), preferred_element_type=F32)
        pos = jnp.where(sel_b, rank, 0.0)
        pos_ref[blk * t:(blk + 1) * t, :] = pos
        post_ref[blk] = pos.T[:N_EXPERTS]
        cnt_ref[blk] = rank[t - 1:t, :]


def _outproj(a_out, h_m, xf, mod, ng, w_top, w_bot, router_w, batch, seq):
    n, d = xf.shape
    tm = ROW_TILE
    tpb = seq // tm
    bpt = tm // MOE_BLOCK
    const = lambda i: (0, 0)
    half = pl.BlockSpec((tm, HEAD_GROUP_W), lambda i: (i, 0))
    full = pl.BlockSpec((tm, d), lambda i: (i, 0))
    lanes = pl.BlockSpec((tm, LANES), lambda i: (i, 0))
    return pl.pallas_call(
        _outproj_kernel,
        out_shape=[jax.ShapeDtypeStruct((n, d), F32), jax.ShapeDtypeStruct((n, d), BF16),
                   jax.ShapeDtypeStruct((n, LANES), F32), jax.ShapeDtypeStruct((n, LANES), F32),
                   jax.ShapeDtypeStruct((n // MOE_BLOCK, N_EXPERTS, MOE_BLOCK), F32),
                   jax.ShapeDtypeStruct((n // MOE_BLOCK, 1, LANES), F32)],
        grid=(n // tm,),
        in_specs=[half, half, full,
                  pl.BlockSpec((1, 6, d), lambda i: (i // tpb, 0, 0)),
                  pl.BlockSpec(ng.shape, const), pl.BlockSpec(w_top.shape, const),
                  pl.BlockSpec(w_bot.shape, const), pl.BlockSpec(router_w.shape, const)],
        out_specs=[full, full, lanes, lanes,
                   pl.BlockSpec((bpt, N_EXPERTS, MOE_BLOCK), lambda i: (i, 0, 0)),
                   pl.BlockSpec((bpt, 1, LANES), lambda i: (i, 0, 0))],
        compiler_params=_params("parallel"),
        name="out_proj_moe",
    )(a_out, h_m, xf, mod, ng, w_top, w_bot, router_w)


def _mixer_ffn_kernel(a_ref, hm_ref, x_ref, mod_ref, ng_ref, wt_ref, wb_ref, w1_ref, w3_ref, w2_ref, o_ref,
                      x1_sc, h2_sc, acc_sc, *, n_col):
    f = pl.program_id(1)
    mod = mod_ref[0]

    @pl.when(f == 0)
    def _():
        y = (jnp.dot(a_ref[...], wt_ref[...], preferred_element_type=F32)
             + jnp.dot(hm_ref[...], wb_ref[...], preferred_element_type=F32))
        x1 = x_ref[...] + mod[2:3] * _rms(y, ng_ref[1:2, :])
        x1_sc[...] = x1
        h2_sc[...] = (_rms(x1, ng_ref[2:3, :]) * (1.0 + mod[4:5]) + mod[3:4]).astype(BF16)
        acc_sc[...] = jnp.zeros(acc_sc.shape, F32)

    hb = h2_sc[...]
    a = jnp.dot(hb, w1_ref[0], preferred_element_type=F32)
    b = jnp.dot(hb, w3_ref[0], preferred_element_type=F32)
    hh = a * _sigmoid(a) * b
    acc_sc[...] += jnp.dot(hh.astype(BF16), w2_ref[0], preferred_element_type=F32)

    @pl.when(f == n_col - 1)
    def _():
        o_ref[...] = x1_sc[...] + mod[5:6] * _rms(acc_sc[...], ng_ref[3:4, :])


def _mixer_ffn(a_out, h_m, xf, mod, ng, w_top, w_bot, w1, w3, w2, layer, batch, seq):
    n, d = xf.shape
    dff = w1.shape[2]
    tm, tf = FFN_ROW_TILE, FFN_COL_TILE
    tpb = seq // tm
    n_col = dff // tf
    row = lambda i, f: (i, 0)
    const = lambda i, f: (0, 0)
    half = pl.BlockSpec((tm, HEAD_GROUP_W), row)
    return pl.pallas_call(
        functools.partial(_mixer_ffn_kernel, n_col=n_col),
        out_shape=jax.ShapeDtypeStruct((n, d), F32),
        grid=(n // tm, n_col),
        in_specs=[half, half, pl.BlockSpec((tm, d), row),
                  pl.BlockSpec((1, 6, d), lambda i, f: (i // tpb, 0, 0)),
                  pl.BlockSpec(ng.shape, const),
                  pl.BlockSpec(w_top.shape, const), pl.BlockSpec(w_bot.shape, const),
                  pl.BlockSpec((1, d, tf), lambda i, f: (layer, 0, f)),
                  pl.BlockSpec((1, d, tf), lambda i, f: (layer, 0, f)),
                  pl.BlockSpec((1, tf, d), lambda i, f: (layer, f, 0))],
        out_specs=pl.BlockSpec((tm, d), row),
        scratch_shapes=[pltpu.VMEM((tm, d), F32), pltpu.VMEM((tm, d), BF16), pltpu.VMEM((tm, d), F32)],
        compiler_params=_params("parallel", "arbitrary"),
        name="mixer_ffn",
    )(a_out, h_m, xf, mod, ng, w_top, w_bot, w1, w3, w2)


def _moe_tables(cnt, batch, seq):
    t = MOE_BLOCK
    e_n = N_EXPERTS
    nbb = seq // t
    ntb = 2 * nbb + e_n
    c = cnt.reshape(batch, nbb, e_n)
    cum = jnp.cumsum(c, axis=1) - c
    tot = jnp.sum(c, axis=1)
    ntile = (tot + t - 1) // t
    tile_end = jnp.cumsum(ntile, axis=1)
    tile_base = tile_end - ntile
    tl = jnp.arange(ntb, dtype=jnp.int32)
    e_t = jnp.sum((tl[None, :, None] >= tile_end[:, None, :]).astype(jnp.int32), axis=2)
    valid = e_t < e_n
    e_c = jnp.minimum(e_t, e_n - 1)
    s0 = (tl[None, :] - jnp.take_along_axis(tile_base, e_c, axis=1)) * t
    bidx = jnp.arange(batch)[:, None]
    cum_sel = cum.transpose(0, 2, 1)[bidx, e_c]
    end_sel = cum_sel + c.transpose(0, 2, 1)[bidx, e_c]
    lo = jnp.sum((end_sel <= s0[..., None]).astype(jnp.int32), axis=2)
    hi = jnp.sum((cum_sel < (s0 + t)[..., None]).astype(jnp.int32), axis=2)

    def flat(a, tail):
        return jnp.concatenate([a.reshape(-1).astype(jnp.int32), jnp.full((1,), tail, jnp.int32)])

    tile_e = flat(e_c, e_n - 1)
    tile_v = flat(valid, 0)
    order = jnp.arange(tile_v.shape[0], dtype=jnp.int32)
    last_valid = lax.cummax(jnp.where(tile_v == 1, order, -1), axis=0)
    prev_valid = jnp.concatenate([jnp.full((1,), -1, jnp.int32), last_valid[:-1]])
    prev_e = jnp.where(prev_valid >= 0, tile_e[jnp.maximum(prev_valid, 0)], -1)
    tile_new = jnp.logical_and(tile_v == 1, tile_e != prev_e).astype(jnp.int32)
    tile_tables = (flat(jnp.broadcast_to(bidx, (batch, ntb)), batch - 1), tile_e, tile_v, tile_new,
                   flat(lo, 0), flat(hi, 0), flat(s0, 0), cum.reshape(-1).astype(jnp.int32))
    row0 = (bidx[:, :, None] * ntb + tile_base[:, None, :]) * t + cum
    seg_tables = ((row0 // t).reshape(-1).astype(jnp.int32), (row0 % t).reshape(-1).astype(jnp.int32))
    return tile_tables, seg_tables


def _moe_expert_kernel(tb_ref, te_ref, tv_ref, tn_ref, lo_ref, hi_ref, s0_ref, cum_ref,
                       h_ref, post_ref, w1_hbm, w3_hbm, w2_hbm, o_ref,
                       xg_sc, w1_sc, w3_sc, w2_sc, sem, *, layer, n_col, tf, nbb):
    ti = pl.program_id(0)
    t = MOE_BLOCK
    e = te_ref[ti]

    def chunk_copies(f):
        cols = pl.ds(f * tf, tf)
        return (pltpu.make_async_copy(w1_hbm.at[layer, e, :, cols], w1_sc.at[f], sem.at[0, f]),
                pltpu.make_async_copy(w3_hbm.at[layer, e, :, cols], w3_sc.at[f], sem.at[1, f]),
                pltpu.make_async_copy(w2_hbm.at[layer, e, cols, :], w2_sc.at[f], sem.at[2, f]))

    def gather_rows():
        b = tb_ref[ti]
        s0 = s0_ref[ti]
        xg_sc[...] = jnp.zeros(xg_sc.shape, F32)
        slot = lax.broadcasted_iota(jnp.int32, (t, t), 0).astype(F32)

        def gather(j, carry):
            blk = b * nbb + j
            rank = post_ref[blk, pl.ds(e, 1), :]
            shift = (cum_ref[blk * N_EXPERTS + e] - s0 - 1).astype(F32)
            onehot = jnp.logical_and(rank > 0.0, rank + shift == slot).astype(BF16)
            rows = h_ref[0, pl.ds(pl.multiple_of(j * t, t), t), :]
            xg_sc[...] += jnp.dot(onehot, rows, preferred_element_type=F32)
            return carry

        lax.fori_loop(lo_ref[ti], hi_ref[ti], gather, 0)
        return xg_sc[...].astype(BF16)

    def swiglu(x, wait_chunks):
        acc = jnp.zeros(xg_sc.shape, F32)
        for f in range(n_col):
            if wait_chunks:
                for cp in chunk_copies(f):
                    cp.wait()
            a = jnp.dot(x, w1_sc[f], preferred_element_type=F32)
            g = jnp.dot(x, w3_sc[f], preferred_element_type=F32)
            hh = (a * _sigmoid(a) * g).astype(BF16)
            acc = acc + jnp.dot(hh, w2_sc[f], preferred_element_type=F32)
        o_ref[...] = acc.astype(BF16)

    @pl.when(tv_ref[ti] == 0)
    def _():
        o_ref[...] = jnp.zeros(o_ref.shape, o_ref.dtype)

    @pl.when(jnp.logical_and(tv_ref[ti] == 1, tn_ref[ti] == 1))
    def _():
        for f in range(n_col):
            for cp in chunk_copies(f):
                cp.start()
        swiglu(gather_rows(), wait_chunks=True)

    @pl.when(jnp.logical_and(tv_ref[ti] == 1, tn_ref[ti] == 0))
    def _():
        swiglu(gather_rows(), wait_chunks=False)


def _moe_experts(h2, post, tables, w1, w3, w2, layer, batch, seq):
    n, d = h2.shape
    t = MOE_BLOCK
    nbb = seq // t
    n_tiles = tables[0].shape[0]
    dff = w1.shape[3]
    tf = MOE_COL_TILE
    n_col = dff // tf
    hbm = pl.BlockSpec(memory_space=pl.ANY)
    grid_spec = pltpu.PrefetchScalarGridSpec(
        num_scalar_prefetch=len(tables),
        grid=(n_tiles,),
        in_specs=[
            pl.BlockSpec((1, seq, d), lambda i, tb, *_: (tb[i], 0, 0), pipeline_mode=pl.Buffered(1)),
            pl.BlockSpec(post.shape, lambda i, *_: (0, 0, 0)),
            hbm, hbm, hbm,
        ],
        out_specs=pl.BlockSpec((t, d), lambda i, *_: (i, 0)),
        scratch_shapes=[pltpu.VMEM((t, d), F32),
                        pltpu.VMEM((n_col, d, tf), BF16), pltpu.VMEM((n_col, d, tf), BF16),
                        pltpu.VMEM((n_col, tf, d), BF16), pltpu.SemaphoreType.DMA((3, n_col))],
    )
    return pl.pallas_call(
        functools.partial(_moe_expert_kernel, layer=layer, n_col=n_col, tf=tf, nbb=nbb),
        out_shape=jax.ShapeDtypeStruct((n_tiles * t, d), BF16),
        grid_spec=grid_spec,
        compiler_params=_params("arbitrary"),
        name="moe_experts",
    )(*tables, h2.reshape(batch, seq, d), post, w1, w3, w2)


def _moe_combine_kernel(st_ref, so_ref, *refs):
    y_refs = refs[:2 * N_EXPERTS]
    pos_ref, gates_ref, x_ref, mod_ref, ng_ref, o_ref = refs[2 * N_EXPERTS:]
    i = pl.program_id(0)
    t = MOE_BLOCK
    pos = pos_ref[...]
    gates = gates_ref[...]
    slot = lax.broadcasted_iota(jnp.int32, (t, 2 * t), 1).astype(F32)
    acc = jnp.zeros(x_ref.shape, F32)
    for e in range(N_EXPERTS):
        rank = pos[:, e:e + 1]
        off = so_ref[i * N_EXPERTS + e].astype(F32)
        onehot = jnp.logical_and(rank > 0.0, rank - 1.0 + off == slot).astype(BF16)
        ys = jnp.concatenate([y_refs[2 * e][...], y_refs[2 * e + 1][...]], axis=0)
        acc = acc + gates[:, e:e + 1] * jnp.dot(onehot, ys, preferred_element_type=F32)
    mod = mod_ref[0]
    o_ref[...] = x_ref[...] + mod[5:6] * _rms(acc, ng_ref[3:4, :])


def _moe_combine(ys, pos, gates, x1, mod, ng, seg_tables, batch, seq):
    n, d = x1.shape
    t = MOE_BLOCK
    nbb = seq // t
    y_specs = []
    for e in range(N_EXPERTS):
        for nxt in range(2):
            y_specs.append(pl.BlockSpec((t, d), functools.partial(
                lambda i, st, so, e, nxt: (st[i * N_EXPERTS + e] + nxt, 0), e=e, nxt=nxt)))
    grid_spec = pltpu.PrefetchScalarGridSpec(
        num_scalar_prefetch=2,
        grid=(n // t,),
        in_specs=y_specs + [
            pl.BlockSpec((t, LANES), lambda i, st, so: (i, 0)),
            pl.BlockSpec((t, LANES), lambda i, st, so: (i, 0)),
            pl.BlockSpec((t, d), lambda i, st, so: (i, 0)),
            pl.BlockSpec((1, 6, d), lambda i, st, so: (i // nbb, 0, 0)),
            pl.BlockSpec(ng.shape, lambda i, st, so: (0, 0)),
        ],
        out_specs=pl.BlockSpec((t, d), lambda i, st, so: (i, 0)),
    )
    return pl.pallas_call(
        _moe_combine_kernel,
        out_shape=jax.ShapeDtypeStruct((n, d), F32),
        grid_spec=grid_spec,
        compiler_params=_params("parallel"),
        name="moe_combine",
    )(*seg_tables, *([ys] * (2 * N_EXPERTS)), pos, gates, x1, mod, ng)


@jax.jit
def _forward(x, c, rel_bias, ada_w, ada_b, norm_g, w_in, gate_b, conv_w, conv_b, lam_vec,
             attn_norm_g, mlstm_norm_g, w_out, ffn_w1, ffn_w3, ffn_w2, router_w, moe_w1, moe_w3, moe_w2):
    batch, seq, d = x.shape
    depth = ada_w.shape[0]
    n = batch * seq
    n_main = N_SLABS * HEAD_GROUP_W
    assert seq % ATTN_BLOCK == 0 and seq % FFN_ROW_TILE == 0 and seq % MLSTM_CHUNK == 0
    assert ROW_TILE % MOE_BLOCK == 0
    assert seq >= 2 * ATTN_BLOCK and ROW_TILE == ATTN_BLOCK
    assert w_in.shape[2] == n_main + 2 * M_HEADS

    mod_all = _ada(c, ada_w, ada_b)
    bias_tiles = _bias_tables(rel_bias, seq)

    ffn_w = tuple(w.astype(BF16) for w in (ffn_w1, ffn_w3, ffn_w2))
    moe_w = tuple(w.astype(BF16) for w in (moe_w1, moe_w3, moe_w2))
    xf = x.reshape(n, d)
    for i in range(depth):
        mod = mod_all[i]
        ng = norm_g[i]
        lam_init = 0.8 - 0.6 * math.exp(-0.3 * i)

        w_gate = w_in[i][:, n_main:]
        w_bf = w_in[i][:, :n_main].astype(BF16)
        gw = HEAD_GROUP_W
        slabs = _inproj(
            xf, mod, ng[0:1], jnp.concatenate([w_bf[:, gw:2 * gw], w_bf[:, 3 * gw:]], axis=1),
            w_bf[:, :gw].T, w_bf[:, 2 * gw:3 * gw].T,
            jnp.zeros((d, LANES), BF16).at[:, :2 * M_HEADS].set(w_gate.astype(BF16)),
            w_gate.T.astype(BF16),
            jnp.zeros((1, LANES), F32).at[0, :2 * M_HEADS].set(gate_b[i].reshape(-1)),
            gate_b[i].reshape(2 * M_HEADS, 1), batch, seq)
        ak, mq, mk, mv, mo = (s.reshape(batch, seq, HEAD_GROUP_W) for s in slabs[:N_SLABS - 2])
        aqt, avt, grow, gcol = slabs[N_SLABS - 2:]

        scal = jnp.full((1,), lam_init, F32)
        a_out = _attention(aqt, ak, avt, scal, bias_tiles, lam_vec[i], attn_norm_g[i].reshape(LANES, 1))

        h_m = _mlstm(mq, mk, mv, mo, grow, gcol,
                     conv_w[i].reshape(CONV_K, 2 * M_HEADS, LANES).transpose(1, 0, 2),
                     conv_b[i].reshape(2 * M_HEADS, 1, LANES),
                     mlstm_norm_g[i].reshape(M_HEADS, 1, LANES))

        w_o = w_out[i].astype(BF16)
        j = i // 2
        if i % 2 == 0:
            xf = _mixer_ffn(a_out.reshape(n, -1), h_m.reshape(n, -1), xf, mod, ng,
                            w_o[:HEAD_GROUP_W], w_o[HEAD_GROUP_W:], *ffn_w, j, batch, seq)
        else:
            rw = jnp.zeros((d, LANES), F32).at[:, :N_EXPERTS].set(router_w[j])
            x1, h2, gates, pos, post, cnt = _outproj(a_out.reshape(n, -1), h_m.reshape(n, -1), xf, mod, ng,
                                                     w_o[:HEAD_GROUP_W], w_o[HEAD_GROUP_W:], rw, batch, seq)
            tile_tables, seg_tables = _moe_tables(cnt[:, 0, :N_EXPERTS].astype(jnp.int32), batch, seq)
            ys = _moe_experts(h2, post, tile_tables, *moe_w, j, batch, seq)
            xf = _moe_combine(ys, pos, gates, x1, mod, ng, seg_tables, batch, seq)
    return xf.reshape(batch, seq, d)


def kernel(x, c, rel_bias, ada_w, ada_b, norm_g, w_in, gate_b, conv_w, conv_b, lam_vec, attn_norm_g,
           mlstm_norm_g, w_out, ffn_w1, ffn_w3, ffn_w2, router_w, moe_w1, moe_w3, moe_w2):
    return _forward(x, c, rel_bias, ada_w, ada_b, norm_g, w_in, gate_b, conv_w, conv_b, lam_vec,
                    attn_norm_g, mlstm_norm_g, w_out, ffn_w1, ffn_w3, ffn_w2, router_w, moe_w1, moe_w3, moe_w2)
```

```python
import functools
import math

import numpy as np
import jax
import jax.numpy as jnp
from jax import lax
from jax.experimental import pallas as pl
from jax.experimental.pallas import tpu as pltpu

F32 = jnp.float32
BF16 = jnp.bfloat16
HIGHEST = lax.Precision.HIGHEST

A_HEADS = 4
A_DH = 64
M_HEADS = 4
M_DH = 128
CONV_K = 4
N_BUCKETS = 32
MAX_DIST = 128
N_EXPERTS = 8
EPS = 1e-6

LANES = 128
HEAD_GROUP_W = 512
N_SLABS = 7
NEG = -1e30
LOG2E = math.log2(math.e)
Q_SCALE = A_DH ** -0.5 * LOG2E
VMEM_LIMIT_BYTES = 56 * 1024 * 1024

ATTN_BLOCK = 512
MLSTM_CHUNK = 256
MLSTM_HEADS_PER_STEP = 4
MLSTM_CHUNKS_PER_STEP = 2
ROW_TILE = 512
OUTPROJ_ROW_TILE = 1024
FFN_ROW_TILE = 512
FFN_COL_TILE = 1792
MOE_COL_TILE = 512
ADA_COL_TILE = 1536
MOE_BLOCK = 256
COMBINE_WINDOW_TILE = MOE_BLOCK
COMBINE_WINDOW_TILES = MOE_BLOCK // COMBINE_WINDOW_TILE + 1


def _params(*sem, flags=None):
    return pltpu.CompilerParams(dimension_semantics=sem, vmem_limit_bytes=VMEM_LIMIT_BYTES, flags=flags)


def _rms(x, g):
    return x * lax.rsqrt(jnp.mean(x * x, axis=-1, keepdims=True) + EPS) * g


def _sigmoid(x):
    return 1.0 / (1.0 + jnp.exp(-x))


def _log_sigmoid(x):
    return jnp.minimum(x, 0.0) - jnp.log(1.0 + jnp.exp(-jnp.abs(x)))


def _split3(x):
    hi = x.astype(BF16)
    r1 = x - hi.astype(F32)
    mid = r1.astype(BF16)
    lo = (r1 - mid.astype(F32)).astype(BF16)
    return hi, mid, lo


def _pick_lane(x, idx):
    lane = lax.broadcasted_iota(jnp.int32, x.shape, 1)
    return jnp.sum(jnp.where(lane == idx, x, 0.0), axis=1, keepdims=True)


def _pick_row(x, idx):
    row = lax.broadcasted_iota(jnp.int32, x.shape, 0)
    return jnp.sum(jnp.where(row == idx, x, 0.0), axis=0, keepdims=True)


def _ada_kernel(c_ref, w_ref, b_ref, o_ref):
    c = c_ref[...]
    ca = c * _sigmoid(c)
    o_ref[0] = jnp.dot(ca, w_ref[0], precision=HIGHEST, preferred_element_type=F32) + b_ref[0]


def _ada(c, ada_w, ada_b):
    depth, d, d6 = ada_w.shape
    b = c.shape[0]
    rows = 8
    cp = jnp.zeros((rows, d), F32).at[:b].set(c)
    out = pl.pallas_call(
        _ada_kernel,
        out_shape=jax.ShapeDtypeStruct((depth, rows, d6), F32),
        grid=(depth, d6 // ADA_COL_TILE),
        in_specs=[
            pl.BlockSpec((rows, d), lambda l, j: (0, 0)),
            pl.BlockSpec((1, d, ADA_COL_TILE), lambda l, j: (l, 0, j)),
            pl.BlockSpec((1, 1, ADA_COL_TILE), lambda l, j: (l, 0, j)),
        ],
        out_specs=pl.BlockSpec((1, rows, ADA_COL_TILE), lambda l, j: (l, 0, j)),
        compiler_params=_params("parallel", "parallel"),
        name="ada_mod",
    )(cp, ada_w, ada_b.reshape(depth, 1, d6))
    return out[:, :b].reshape(depth, b, 6, d)


def _inproj_kernel(x_ref, mod_ref, g_ref, w_ref, wqt_ref, wvt_ref, wg_ref, wgt_ref, gb_ref, gbt_ref,
                   ak, mq, mk, mv, mo, aqt, avt, grow, gcol):
    mod = mod_ref[0]
    h = _rms(x_ref[...], g_ref[...]) * (1.0 + mod[1:2]) + mod[0:1]
    hb = h.astype(BF16)
    for s, o in enumerate((ak, mq, mk, mv, mo)):
        w = w_ref[:, s * HEAD_GROUP_W:(s + 1) * HEAD_GROUP_W]
        o[...] = jnp.dot(hb, w, preferred_element_type=F32).astype(BF16)
    nt = (((1,), (1,)), ((), ()))
    aqt[0, 0] = (lax.dot_general(wqt_ref[...], hb, nt, preferred_element_type=F32) * Q_SCALE).astype(BF16)
    avt[0, 0] = lax.dot_general(wvt_ref[...], hb, nt, preferred_element_type=F32).astype(BF16)
    grow[...] = jnp.dot(hb, wg_ref[...], preferred_element_type=F32) + gb_ref[...]
    gcol[0] = lax.dot_general(wgt_ref[...], hb, (((1,), (1,)), ((), ())),
                              preferred_element_type=F32) + gbt_ref[...]


def _inproj(xf, mod, g, w_main, w_qt, w_vt, wg, wgt, gb, gbt, batch, seq):
    n, d = xf.shape
    tm = ROW_TILE
    tpb = seq // tm
    n_row_slabs = N_SLABS - 2
    slab = jax.ShapeDtypeStruct((n, HEAD_GROUP_W), BF16)
    slab_t = jax.ShapeDtypeStruct((batch, tpb, HEAD_GROUP_W, tm), BF16)
    row_spec = pl.BlockSpec((tm, HEAD_GROUP_W), lambda i: (i, 0))
    t_spec = pl.BlockSpec((1, 1, HEAD_GROUP_W, tm), lambda i: (i // tpb, i % tpb, 0, 0))
    const = lambda i: (0, 0)
    return pl.pallas_call(
        _inproj_kernel,
        out_shape=[slab] * n_row_slabs + [slab_t, slab_t, jax.ShapeDtypeStruct((n, LANES), F32),
                                          jax.ShapeDtypeStruct((batch, 8, seq), F32)],
        grid=(n // tm,),
        in_specs=[
            pl.BlockSpec((tm, d), lambda i: (i, 0)),
            pl.BlockSpec((1, 6, d), lambda i: (i // tpb, 0, 0)),
            pl.BlockSpec((1, d), const),
            pl.BlockSpec(w_main.shape, const),
            pl.BlockSpec(w_qt.shape, const),
            pl.BlockSpec(w_vt.shape, const),
            pl.BlockSpec(wg.shape, const),
            pl.BlockSpec(wgt.shape, const),
            pl.BlockSpec(gb.shape, const),
            pl.BlockSpec(gbt.shape, const),
        ],
        out_specs=[row_spec] * n_row_slabs + [t_spec, t_spec,
                                              pl.BlockSpec((tm, LANES), lambda i: (i, 0)),
                                              pl.BlockSpec((1, 8, tm), lambda i: (i // tpb, 0, i % tpb))],
        compiler_params=_params("parallel"),
        name="in_proj",
    )(xf, mod, g, w_main, w_qt, w_vt, wg, wgt, gb, gbt)


ACC_ROWS = LANES + 16


def _attn_kernel(scal_ref, qt_ref, k_ref, vt_ref, bias_ref, lv_ref, g_ref, o_ref, m_sc, acc_sc,
                 sa_sc, sb_sc, mxa_sc, mxb_sc, *, blk):
    qi = pl.program_id(2)
    lam_init = scal_ref[0]

    feat = lax.broadcasted_iota(jnp.int32, (LANES, blk), 0)
    qt = qt_ref[0, 0]
    zero = jnp.zeros_like(qt)
    qz = (jnp.where(feat < A_DH, qt, zero), jnp.where(feat >= A_DH, qt, zero))
    ones_rows = jnp.ones((ACC_ROWS - LANES, blk), BF16)

    def scores(kj, s_sc, mx_sc, near=None):
        start = pl.multiple_of(kj * blk, blk)
        kt = k_ref[0, pl.ds(start, blk), :]
        for c in range(2):
            s = jnp.dot(kt, qz[c], preferred_element_type=F32)
            if near is not None:
                s = s + bias_ref[0, near]
            s_sc[c] = s
            mx_sc[c] = jnp.max(s, axis=0, keepdims=True)

    def accumulate(kj, s_sc, mx_sc, first=False):
        vext = jnp.concatenate([vt_ref[0, kj], ones_rows], axis=0)
        for c in range(2):
            m_new = mx_sc[c] if first else jnp.maximum(m_sc[c], mx_sc[c])
            p = jnp.exp2(s_sc[c] - m_new).astype(BF16)
            pv = jnp.dot(vext, p, preferred_element_type=F32)
            acc_sc[c] = pv if first else jnp.exp2(m_sc[c] - m_new) * acc_sc[c] + pv
            m_sc[c] = m_new

    n_far = jnp.maximum(qi - 1, 0)
    scores(qi, sa_sc, mxa_sc, near=0)

    @pl.when(qi == 0)
    def _():
        accumulate(qi, sa_sc, mxa_sc, first=True)

    @pl.when(qi == 1)
    def _():
        scores(qi - 1, sb_sc, mxb_sc, near=1)
        accumulate(qi, sa_sc, mxa_sc, first=True)
        accumulate(qi - 1, sb_sc, mxb_sc)

    @pl.when(qi >= 2)
    def _():
        scores(qi - 1, sb_sc, mxb_sc, near=1)
        accumulate(qi, sa_sc, mxa_sc, first=True)
        scores(0, sa_sc, mxa_sc)
        accumulate(qi - 1, sb_sc, mxb_sc)

    def pair_body(u, carry):
        scores(2 * u + 1, sb_sc, mxb_sc)
        accumulate(2 * u, sa_sc, mxa_sc)
        scores(jnp.minimum(2 * u + 2, n_far - 1), sa_sc, mxa_sc)
        accumulate(2 * u + 1, sb_sc, mxb_sc)
        return carry

    lax.fori_loop(0, n_far // 2, pair_body, 0)

    @pl.when(n_far % 2 == 1)
    def _():
        accumulate(n_far - 1, sa_sc, mxa_sc)

    acc0 = acc_sc[0]
    acc1 = acc_sc[1]
    o0 = acc0[:LANES] * (1.0 / acc0[LANES:LANES + 1])
    o1 = acc1[:LANES] * (1.0 / acc1[LANES:LANES + 1])
    lv = lv_ref[...]
    lam = (jnp.exp(jnp.sum(lv[0:1] * lv[1:2], axis=1, keepdims=True))
           - jnp.exp(jnp.sum(lv[2:3] * lv[3:4], axis=1, keepdims=True)) + lam_init)
    a = o0 - lam * o1
    y = a * lax.rsqrt(jnp.mean(a * a, axis=0, keepdims=True) + EPS) * g_ref[...] * (1.0 - lam_init)
    o_ref[0] = y.T.astype(BF16)


def _attention(aqt, ak, avt, scal, bias_tiles, lam_vec, norm_g):
    batch, seq, _ = ak.shape
    blk = ATTN_BLOCK
    nk = seq // blk
    kernel = functools.partial(_attn_kernel, blk=blk)
    grid_spec = pltpu.PrefetchScalarGridSpec(
        num_scalar_prefetch=1,
        grid=(batch, A_HEADS, nk),
        in_specs=[
            pl.BlockSpec((1, 1, LANES, blk), lambda b, h, i, s: (b, i, h, 0)),
            pl.BlockSpec((1, seq, LANES), lambda b, h, i, s: (b, 0, h)),
            pl.BlockSpec((1, nk, LANES, blk), lambda b, h, i, s: (b, 0, h, 0)),
            pl.BlockSpec((1, 2, blk, blk), lambda b, h, i, s: (h, 0, 0, 0)),
            pl.BlockSpec(lam_vec.shape, lambda b, h, i, s: (0, 0)),
            pl.BlockSpec(norm_g.shape, lambda b, h, i, s: (0, 0)),
        ],
        out_specs=pl.BlockSpec((1, blk, LANES), lambda b, h, i, s: (b, i, h)),
        scratch_shapes=[pltpu.VMEM((2, 1, blk), F32), pltpu.VMEM((2, ACC_ROWS, blk), F32),
                        pltpu.VMEM((2, blk, blk), F32), pltpu.VMEM((2, blk, blk), F32),
                        pltpu.VMEM((2, 1, blk), F32), pltpu.VMEM((2, 1, blk), F32)],
    )
    return pl.pallas_call(
        kernel,
        out_shape=jax.ShapeDtypeStruct(ak.shape, BF16),
        grid_spec=grid_spec,
        compiler_params=_params("parallel", "parallel", "arbitrary"),
        name="diff_attn",
    )(scal, aqt, ak, avt, bias_tiles, lam_vec, norm_g)


def _t5_bucket(n):
    max_exact = N_BUCKETS // 2
    nf = jnp.maximum(n, 1).astype(F32)
    large = max_exact + (jnp.log(nf / max_exact) / math.log(MAX_DIST / max_exact)
                         * (N_BUCKETS - max_exact)).astype(jnp.int32)
    large = jnp.minimum(large, N_BUCKETS - 1)
    return jnp.where(n < max_exact, n, large)


def _bias_tables(rel_bias, seq):
    blk = ATTN_BLOCK
    far = np.arange(blk + 1, max(seq, blk + 2), dtype=np.float64)
    sat = (N_BUCKETS // 2) + np.log(far / (N_BUCKETS // 2)) / math.log(MAX_DIST / (N_BUCKETS // 2)) * (N_BUCKETS // 2)
    assert np.all(sat >= N_BUCKETS - 0.5), "far tiles need a saturated distance bucket"
    dist_bias = rel_bias[_t5_bucket(jnp.arange(seq, dtype=jnp.int32))].T.astype(F32)
    heads = dist_bias.shape[0]
    last = dist_bias[:, seq - 1:seq]
    padded = jnp.concatenate([jnp.full((heads, blk), NEG, F32), dist_bias,
                              jnp.broadcast_to(last, (heads, blk))], axis=1)
    tiles = []
    for d in (0, 1):
        base = blk + d * blk
        w = jnp.concatenate([padded[:, base:base + blk + 1], padded[:, base - blk + 1:base]], axis=1)
        t = jnp.tile(w, (1, blk))[:, :blk * (2 * blk - 1)].reshape(heads, blk, 2 * blk - 1)[:, :, :blk]
        tiles.append(t - last[:, :, None])
    return jnp.stack(tiles, axis=1) * LOG2E


def _mlstm_kernel(mq_ref, mk_ref, mv_ref, mo_ref, grow_ref, gcol_ref, cwq_ref, cwk_ref, cbq_ref, cbk_ref,
                  ng_ref, o_ref, xq_sc, xk_sc, c_sc, n_sc, m_sc, *, chunk, chunks, heads):
    L = chunk
    hp = pl.program_id(1)
    ci = pl.program_id(2)

    @pl.when(ci == 0)
    def _():
        xq_sc[:, 0:8, :] = jnp.zeros((heads, 8, LANES), F32)
        xk_sc[:, 0:8, :] = jnp.zeros((heads, 8, LANES), F32)
        c_sc[...] = jnp.zeros(c_sc.shape, F32)
        n_sc[...] = jnp.zeros(n_sc.shape, F32)
        m_sc[...] = jnp.zeros(m_sc.shape, F32)

    def conv_silu(x, sc, w, b):
        sc[8:8 + L, :] = x.astype(F32)
        y = b
        for j in range(CONV_K):
            y = y + sc[5 + j:5 + j + L, :] * w[j:j + 1]
        sc[0:8, :] = sc[L:L + 8, :]
        return y * _sigmoid(y)

    r = lax.broadcasted_iota(jnp.int32, (L, L), 0)
    cidx = lax.broadcasted_iota(jnp.int32, (L, L), 1)
    causal = r >= cidx
    lower = causal.astype(BF16)
    upper = (r <= cidx).astype(BF16)

    for cc in range(chunks):
        rows = slice(cc * L, (cc + 1) * L)
        g_rows = grow_ref[rows, :]
        cum_rows = sum(jnp.dot(lower, part, preferred_element_type=F32) for part in _split3(_log_sigmoid(g_rows)))
        g_cols = gcol_ref[0, :, rows]
        cum_cols = sum(jnp.dot(part, upper, preferred_element_type=F32) for part in _split3(_log_sigmoid(g_cols)))

        for hh in range(heads):
            h = hp * heads + hh
            cols = slice(hh * LANES, (hh + 1) * LANES)
            q = conv_silu(mq_ref[0, rows, cols], xq_sc.at[hh], cwq_ref[hh], cbq_ref[hh])
            k = conv_silu(mk_ref[0, rows, cols], xk_sc.at[hh], cwk_ref[hh], cbk_ref[hh]) * (M_DH ** -0.5)
            v = mv_ref[0, rows, cols]

            ig_col = _pick_lane(g_rows, h)
            b_col = _pick_lane(cum_rows, M_HEADS + h)
            ig_row = _pick_row(g_cols, h)
            b_row = _pick_row(cum_cols, M_HEADS + h)
            b_last = b_col[L - 1:L, :]
            m_prev = m_sc[hh]

            dlog = jnp.where(causal, b_col - b_row + ig_row, NEG)
            inter = b_col + m_prev
            m_t = jnp.maximum(inter, jnp.max(dlog, axis=1, keepdims=True))
            qb = q.astype(BF16)
            kb = k.astype(BF16)
            sw = (lax.dot_general(qb, kb, (((1,), (1,)), ((), ())), preferred_element_type=F32)
                  * jnp.exp(dlog - m_t))
            dec = jnp.exp(inter - m_t)
            num = (dec * jnp.dot(qb, c_sc[hh].astype(BF16), preferred_element_type=F32)
                   + jnp.dot(sw.astype(BF16), v, preferred_element_type=F32))
            den = dec * jnp.sum(q * n_sc[hh], axis=1, keepdims=True) + jnp.sum(sw, axis=1, keepdims=True)
            h_t = num * (1.0 / jnp.maximum(jnp.abs(den), jnp.exp(-m_t)))

            a_col = b_last - b_col + ig_col
            m_new = jnp.maximum(b_last + m_prev, jnp.max(a_col, axis=0, keepdims=True))
            decay = jnp.exp(b_last + m_prev - m_new)
            kw = k * jnp.exp(a_col - m_new)
            c_sc[hh] = decay * c_sc[hh] + lax.dot_general(kw.astype(BF16), v, (((0,), (0,)), ((), ())),
                                                          preferred_element_type=F32)
            n_sc[hh] = decay * n_sc[hh] + jnp.sum(kw, axis=0, keepdims=True)
            m_sc[hh] = m_new

            hm = _sigmoid(mo_ref[0, rows, cols].astype(F32)) * h_t
            o_ref[0, rows, cols] = _rms(hm, ng_ref[hh]).astype(BF16)


def _mlstm(mq, mk, mv, mo, grow, gcol, conv_w, conv_b, norm_g):
    batch, seq, _ = mq.shape
    L = MLSTM_CHUNK
    hps = MLSTM_HEADS_PER_STEP
    groups = M_HEADS // hps
    cps = MLSTM_CHUNKS_PER_STEP
    rows = cps * L
    kernel = functools.partial(_mlstm_kernel, chunk=L, chunks=cps, heads=hps)
    slab = pl.BlockSpec((1, rows, hps * LANES), lambda b, h, c: (b, c, h))
    nchunk = seq // rows
    return pl.pallas_call(
        kernel,
        out_shape=jax.ShapeDtypeStruct(mq.shape, BF16),
        grid=(batch, groups, nchunk),
        in_specs=[
            slab, slab, slab, slab,
            pl.BlockSpec((rows, LANES), lambda b, h, c: (b * nchunk + c, 0)),
            pl.BlockSpec((1, 8, rows), lambda b, h, c: (b, 0, c)),
            pl.BlockSpec((hps, CONV_K, LANES), lambda b, h, c: (h, 0, 0)),
            pl.BlockSpec((hps, CONV_K, LANES), lambda b, h, c: (groups + h, 0, 0)),
            pl.BlockSpec((hps, 1, LANES), lambda b, h, c: (h, 0, 0)),
            pl.BlockSpec((hps, 1, LANES), lambda b, h, c: (groups + h, 0, 0)),
            pl.BlockSpec((hps, 1, LANES), lambda b, h, c: (h, 0, 0)),
        ],
        out_specs=slab,
        scratch_shapes=[pltpu.VMEM((hps, L + 8, LANES), F32), pltpu.VMEM((hps, L + 8, LANES), F32),
                        pltpu.VMEM((hps, M_DH, M_DH), F32), pltpu.VMEM((hps, 1, M_DH), F32),
                        pltpu.VMEM((hps, 1, 1), F32)],
        compiler_params=_params("parallel", "parallel", "arbitrary"),
        name="mlstm",
    )(mq, mk, mv, mo, grow, gcol, conv_w, conv_w, conv_b, conv_b, norm_g)


def _outproj_kernel(a_ref, hm_ref, x_ref, mod_ref, ng_ref, wt_ref, wb_ref, rw_ref,
                    x1_ref, h2_ref, gates_ref, pos_ref, post_ref, cnt_ref):
    y = (jnp.dot(a_ref[...], wt_ref[...], preferred_element_type=F32)
         + jnp.dot(hm_ref[...], wb_ref[...], preferred_element_type=F32))
    mod = mod_ref[0]
    ng = ng_ref[...]
    x1 = x_ref[...] + mod[2:3] * _rms(y, ng[1:2])
    x1_ref[...] = x1
    h2 = _rms(x1, ng[2:3]) * (1.0 + mod[4:5]) + mod[3:4]
    h2_ref[...] = h2.astype(BF16)
    h_hi, h_mid, _ = _split3(h2)
    w_hi, w_mid, _ = _split3(rw_ref[...])
    logits = (jnp.dot(h_hi, w_hi, preferred_element_type=F32)
              + jnp.dot(h_hi, w_mid, preferred_element_type=F32)
              + jnp.dot(h_mid, w_hi, preferred_element_type=F32))
    lane = lax.broadcasted_iota(jnp.int32, logits.shape, 1)
    lg = jnp.where(lane < N_EXPERTS, logits, NEG)
    v1 = jnp.max(lg, axis=1, keepdims=True)
    i1 = jnp.min(jnp.where(lg == v1, lane, LANES), axis=1, keepdims=True)
    lg2 = jnp.where(lane == i1, NEG, lg)
    v2 = jnp.max(lg2, axis=1, keepdims=True)
    i2 = jnp.min(jnp.where(lg2 == v2, lane, LANES), axis=1, keepdims=True)
    e2 = jnp.exp(v2 - v1)
    w1 = 1.0 / (1.0 + e2)
    gates_ref[...] = jnp.where(lane == i1, w1, 0.0) + jnp.where(lane == i2, e2 * w1, 0.0)
    sel = jnp.logical_or(lane == i1, lane == i2)
    t = MOE_BLOCK
    r = lax.broadcasted_iota(jnp.int32, (t, t), 0)
    cidx = lax.broadcasted_iota(jnp.int32, (t, t), 1)
    lower = (r >= cidx).astype(BF16)
    for blk in range(logits.shape[0] // t):
        sel_b = sel[blk * t:(blk + 1) * t]
        rank = jnp.dot(lower, sel_b.astype(BF16), preferred_element_type=F32)
        pos = jnp.where(sel_b, rank, 0.0)
        pos_ref[blk * t:(blk + 1) * t, :] = pos
        post_ref[blk] = pos.T[:N_EXPERTS]
        cnt_ref[blk] = rank[t - 1:t, :]


def _outproj(a_out, h_m, xf, mod, ng, w_top, w_bot, router_w, batch, seq):
    n, d = xf.shape
    tm = OUTPROJ_ROW_TILE
    tpb = seq // tm
    bpt = tm // MOE_BLOCK
    const = lambda i: (0, 0)
    half = pl.BlockSpec((tm, HEAD_GROUP_W), lambda i: (i, 0))
    full = pl.BlockSpec((tm, d), lambda i: (i, 0))
    lanes = pl.BlockSpec((tm, LANES), lambda i: (i, 0))
    return pl.pallas_call(
        _outproj_kernel,
        out_shape=[jax.ShapeDtypeStruct((n, d), F32), jax.ShapeDtypeStruct((n, d), BF16),
                   jax.ShapeDtypeStruct((n, LANES), F32), jax.ShapeDtypeStruct((n, LANES), F32),
                   jax.ShapeDtypeStruct((n // MOE_BLOCK, N_EXPERTS, MOE_BLOCK), F32),
                   jax.ShapeDtypeStruct((n // MOE_BLOCK, 1, LANES), F32)],
        grid=(n // tm,),
        in_specs=[half, half, full,
                  pl.BlockSpec((1, 6, d), lambda i: (i // tpb, 0, 0)),
                  pl.BlockSpec(ng.shape, const), pl.BlockSpec(w_top.shape, const),
                  pl.BlockSpec(w_bot.shape, const), pl.BlockSpec(router_w.shape, const)],
        out_specs=[full, full, lanes, lanes,
                   pl.BlockSpec((bpt, N_EXPERTS, MOE_BLOCK), lambda i: (i, 0, 0)),
                   pl.BlockSpec((bpt, 1, LANES), lambda i: (i, 0, 0))],
        compiler_params=_params("parallel"),
        name="out_proj_moe",
    )(a_out, h_m, xf, mod, ng, w_top, w_bot, router_w)


def _mixer_ffn_kernel(a_ref, hm_ref, x_ref, mod_ref, ng_ref, wt_ref, wb_ref, w1_ref, w3_ref, w2_ref, o_ref,
                      x1_sc, h2_sc, acc_sc, *, n_col):
    f = pl.program_id(1)
    mod = mod_ref[0]

    @pl.when(f == 0)
    def _():
        y = (jnp.dot(a_ref[...], wt_ref[...], preferred_element_type=F32)
             + jnp.dot(hm_ref[...], wb_ref[...], preferred_element_type=F32))
        x1 = x_ref[...] + mod[2:3] * _rms(y, ng_ref[1:2, :])
        x1_sc[...] = x1
        h2_sc[...] = (_rms(x1, ng_ref[2:3, :]) * (1.0 + mod[4:5]) + mod[3:4]).astype(BF16)
        acc_sc[...] = jnp.zeros(acc_sc.shape, F32)

    hb = h2_sc[...]
    a = jnp.dot(hb, w1_ref[0], preferred_element_type=F32)
    b = jnp.dot(hb, w3_ref[0], preferred_element_type=F32)
    hh = a * _sigmoid(a) * b
    acc_sc[...] += jnp.dot(hh.astype(BF16), w2_ref[0], preferred_element_type=F32)

    @pl.when(f == n_col - 1)
    def _():
        o_ref[...] = x1_sc[...] + mod[5:6] * _rms(acc_sc[...], ng_ref[3:4, :])


def _mixer_ffn(a_out, h_m, xf, mod, ng, w_top, w_bot, w1, w3, w2, layer, batch, seq):
    n, d = xf.shape
    dff = w1.shape[2]
    tm, tf = FFN_ROW_TILE, FFN_COL_TILE
    tpb = seq // tm
    n_col = dff // tf
    row = lambda i, f: (i, 0)
    const = lambda i, f: (0, 0)
    half = pl.BlockSpec((tm, HEAD_GROUP_W), row)
    return pl.pallas_call(
        functools.partial(_mixer_ffn_kernel, n_col=n_col),
        out_shape=jax.ShapeDtypeStruct((n, d), F32),
        grid=(n // tm, n_col),
        in_specs=[half, half, pl.BlockSpec((tm, d), row),
                  pl.BlockSpec((1, 6, d), lambda i, f: (i // tpb, 0, 0)),
                  pl.BlockSpec(ng.shape, const),
                  pl.BlockSpec(w_top.shape, const), pl.BlockSpec(w_bot.shape, const),
                  pl.BlockSpec((1, d, tf), lambda i, f: (layer, 0, f)),
                  pl.BlockSpec((1, d, tf), lambda i, f: (layer, 0, f)),
                  pl.BlockSpec((1, tf, d), lambda i, f: (layer, f, 0))],
        out_specs=pl.BlockSpec((tm, d), row),
        scratch_shapes=[pltpu.VMEM((tm, d), F32), pltpu.VMEM((tm, d), BF16), pltpu.VMEM((tm, d), F32)],
        compiler_params=_params("parallel", "arbitrary"),
        name="mixer_ffn",
    )(a_out, h_m, xf, mod, ng, w_top, w_bot, w1, w3, w2)


def _moe_tables(cnt, batch, seq):
    t = MOE_BLOCK
    e_n = N_EXPERTS
    nbb = seq // t
    ntb = 2 * nbb + e_n
    c = cnt.reshape(batch, nbb, e_n)
    cum = jnp.cumsum(c, axis=1) - c
    tot = jnp.sum(c, axis=1)
    ntile = (tot + t - 1) // t
    tile_end = jnp.cumsum(ntile, axis=1)
    tile_base = tile_end - ntile
    tl = jnp.arange(ntb, dtype=jnp.int32)
    e_t = jnp.sum((tl[None, :, None] >= tile_end[:, None, :]).astype(jnp.int32), axis=2)
    valid = e_t < e_n
    e_c = jnp.minimum(e_t, e_n - 1)
    s0 = (tl[None, :] - jnp.take_along_axis(tile_base, e_c, axis=1)) * t
    bidx = jnp.arange(batch)[:, None]
    cum_sel = cum.transpose(0, 2, 1)[bidx, e_c]
    end_sel = cum_sel + c.transpose(0, 2, 1)[bidx, e_c]
    lo = jnp.sum((end_sel <= s0[..., None]).astype(jnp.int32), axis=2)
    hi = jnp.sum((cum_sel < (s0 + t)[..., None]).astype(jnp.int32), axis=2)

    def flat(a, tail):
        return jnp.concatenate([a.reshape(-1).astype(jnp.int32), jnp.full((1,), tail, jnp.int32)])

    tile_e = flat(e_c, e_n - 1)
    tile_v = flat(valid, 0)
    order = jnp.arange(tile_v.shape[0], dtype=jnp.int32)
    last_valid = lax.cummax(jnp.where(tile_v == 1, order, -1), axis=0)
    prev_valid = jnp.concatenate([jnp.full((1,), -1, jnp.int32), last_valid[:-1]])
    prev_e = jnp.where(prev_valid >= 0, tile_e[jnp.maximum(prev_valid, 0)], -1)
    tile_new = jnp.logical_and(tile_v == 1, tile_e != prev_e).astype(jnp.int32)
    tile_tables = (flat(jnp.broadcast_to(bidx, (batch, ntb)), batch - 1), tile_e, tile_v, tile_new,
                   flat(lo, 0), flat(hi, 0), flat(s0, 0), cum.reshape(-1).astype(jnp.int32))
    row0 = (bidx[:, :, None] * ntb + tile_base[:, None, :]) * t + cum
    w = COMBINE_WINDOW_TILE
    seg_tables = ((row0 // w).reshape(-1).astype(jnp.int32), (row0 % w).reshape(-1).astype(jnp.int32))
    return tile_tables, seg_tables


def _moe_expert_kernel(tb_ref, te_ref, tv_ref, tn_ref, lo_ref, hi_ref, s0_ref, cum_ref,
                       h_ref, post_ref, w1_hbm, w3_hbm, w2_hbm, o_ref,
                       xg_sc, w1_sc, w3_sc, w2_sc, sem, *, layer, n_col, tf, nbb):
    ti = pl.program_id(0)
    t = MOE_BLOCK
    e = te_ref[ti]

    def chunk_copies(f):
        cols = pl.ds(f * tf, tf)
        return (pltpu.make_async_copy(w1_hbm.at[layer, e, :, cols], w1_sc.at[f], sem.at[0, f]),
                pltpu.make_async_copy(w3_hbm.at[layer, e, :, cols], w3_sc.at[f], sem.at[1, f]),
                pltpu.make_async_copy(w2_hbm.at[layer, e, cols, :], w2_sc.at[f], sem.at[2, f]))

    def gather_rows():
        b = tb_ref[ti]
        s0 = s0_ref[ti]
        xg_sc[...] = jnp.zeros(xg_sc.shape, F32)
        slot = lax.broadcasted_iota(jnp.int32, (t, t), 0).astype(F32)

        def gather(j, carry):
            blk = b * nbb + j
            rank = post_ref[blk, pl.ds(e, 1), :]
            shift = (cum_ref[blk * N_EXPERTS + e] - s0 - 1).astype(F32)
            onehot = jnp.logical_and(rank > 0.0, rank + shift == slot).astype(BF16)
            rows = h_ref[0, pl.ds(pl.multiple_of(j * t, t), t), :]
            xg_sc[...] += jnp.dot(onehot, rows, preferred_element_type=F32)
            return carry

        lax.fori_loop(lo_ref[ti], hi_ref[ti], gather, 0)
        return xg_sc[...].astype(BF16)

    def swiglu(x, wait_chunks):
        acc = jnp.zeros(xg_sc.shape, F32)
        for f in range(n_col):
            if wait_chunks:
                for cp in chunk_copies(f):
                    cp.wait()
            a = jnp.dot(x, w1_sc[f], preferred_element_type=F32)
            g = jnp.dot(x, w3_sc[f], preferred_element_type=F32)
            hh = (a * _sigmoid(a) * g).astype(BF16)
            acc = acc + jnp.dot(hh, w2_sc[f], preferred_element_type=F32)
        o_ref[...] = acc.astype(BF16)

    @pl.when(tv_ref[ti] == 0)
    def _():
        o_ref[...] = jnp.zeros(o_ref.shape, o_ref.dtype)

    @pl.when(jnp.logical_and(tv_ref[ti] == 1, tn_ref[ti] == 1))
    def _():
        for f in range(n_col):
            for cp in chunk_copies(f):
                cp.start()
        swiglu(gather_rows(), wait_chunks=True)

    @pl.when(jnp.logical_and(tv_ref[ti] == 1, tn_ref[ti] == 0))
    def _():
        swiglu(gather_rows(), wait_chunks=False)


def _moe_experts(h2, post, tables, w1, w3, w2, layer, batch, seq):
    n, d = h2.shape
    t = MOE_BLOCK
    nbb = seq // t
    n_tiles = tables[0].shape[0]
    dff = w1.shape[3]
    tf = MOE_COL_TILE
    n_col = dff // tf
    hbm = pl.BlockSpec(memory_space=pl.ANY)
    grid_spec = pltpu.PrefetchScalarGridSpec(
        num_scalar_prefetch=len(tables),
        grid=(n_tiles,),
        in_specs=[
            pl.BlockSpec((1, seq, d), lambda i, tb, *_: (tb[i], 0, 0), pipeline_mode=pl.Buffered(1)),
            pl.BlockSpec(post.shape, lambda i, *_: (0, 0, 0)),
            hbm, hbm, hbm,
        ],
        out_specs=pl.BlockSpec((t, d), lambda i, *_: (i, 0)),
        scratch_shapes=[pltpu.VMEM((t, d), F32),
                        pltpu.VMEM((n_col, d, tf), BF16), pltpu.VMEM((n_col, d, tf), BF16),
                        pltpu.VMEM((n_col, tf, d), BF16), pltpu.SemaphoreType.DMA((3, n_col))],
    )
    return pl.pallas_call(
        functools.partial(_moe_expert_kernel, layer=layer, n_col=n_col, tf=tf, nbb=nbb),
        out_shape=jax.ShapeDtypeStruct((n_tiles * t, d), BF16),
        grid_spec=grid_spec,
        compiler_params=_params("arbitrary"),
        name="moe_experts",
    )(*tables, h2.reshape(batch, seq, d), post, w1, w3, w2)


def _moe_combine_kernel(st_ref, so_ref, *refs):
    nw = COMBINE_WINDOW_TILES
    y_refs = refs[:nw * N_EXPERTS]
    pos_ref, gates_ref, x_ref, mod_ref, ng_ref, o_ref = refs[nw * N_EXPERTS:]
    i = pl.program_id(0)
    t = MOE_BLOCK
    pos = pos_ref[...]
    gates = gates_ref[...]
    slot = lax.broadcasted_iota(jnp.int32, (t, nw * COMBINE_WINDOW_TILE), 1).astype(F32)
    acc = jnp.zeros(x_ref.shape, F32)
    for e in range(N_EXPERTS):
        rank = pos[:, e:e + 1]
        off = so_ref[i * N_EXPERTS + e].astype(F32)
        onehot = jnp.logical_and(rank > 0.0, rank - 1.0 + off == slot).astype(BF16)
        ys = jnp.concatenate([y_refs[nw * e + k][...] for k in range(nw)], axis=0)
        acc = acc + gates[:, e:e + 1] * jnp.dot(onehot, ys, preferred_element_type=F32)
    mod = mod_ref[0]
    o_ref[...] = x_ref[...] + mod[5:6] * _rms(acc, ng_ref[3:4, :])


def _moe_combine(ys, pos, gates, x1, mod, ng, seg_tables, batch, seq):
    n, d = x1.shape
    t = MOE_BLOCK
    nbb = seq // t
    y_specs = []
    for e in range(N_EXPERTS):
        for nxt in range(COMBINE_WINDOW_TILES):
            y_specs.append(pl.BlockSpec((COMBINE_WINDOW_TILE, d), functools.partial(
                lambda i, st, so, e, nxt: (st[i * N_EXPERTS + e] + nxt, 0), e=e, nxt=nxt)))
    grid_spec = pltpu.PrefetchScalarGridSpec(
        num_scalar_prefetch=2,
        grid=(n // t,),
        in_specs=y_specs + [
            pl.BlockSpec((t, LANES), lambda i, st, so: (i, 0)),
            pl.BlockSpec((t, LANES), lambda i, st, so: (i, 0)),
            pl.BlockSpec((t, d), lambda i, st, so: (i, 0)),
            pl.BlockSpec((1, 6, d), lambda i, st, so: (i // nbb, 0, 0)),
            pl.BlockSpec(ng.shape, lambda i, st, so: (0, 0)),
        ],
        out_specs=pl.BlockSpec((t, d), lambda i, st, so: (i, 0)),
    )
    return pl.pallas_call(
        _moe_combine_kernel,
        out_shape=jax.ShapeDtypeStruct((n, d), F32),
        grid_spec=grid_spec,
        compiler_params=_params("parallel"),
        name="moe_combine",
    )(*seg_tables, *([ys] * (COMBINE_WINDOW_TILES * N_EXPERTS)), pos, gates, x1, mod, ng)


@jax.jit
def _forward(x, c, rel_bias, ada_w, ada_b, norm_g, w_in, gate_b, conv_w, conv_b, lam_vec,
             attn_norm_g, mlstm_norm_g, w_out, ffn_w1, ffn_w3, ffn_w2, router_w, moe_w1, moe_w3, moe_w2):
    batch, seq, d = x.shape
    depth = ada_w.shape[0]
    n = batch * seq
    n_main = N_SLABS * HEAD_GROUP_W
    assert seq % ATTN_BLOCK == 0 and seq % FFN_ROW_TILE == 0 and seq % (MLSTM_CHUNK * MLSTM_CHUNKS_PER_STEP) == 0
    assert ROW_TILE % MOE_BLOCK == 0 and OUTPROJ_ROW_TILE % MOE_BLOCK == 0 and seq % OUTPROJ_ROW_TILE == 0
    assert seq >= 2 * ATTN_BLOCK and ROW_TILE == ATTN_BLOCK
    assert w_in.shape[2] == n_main + 2 * M_HEADS

    mod_all = _ada(c, ada_w, ada_b)
    bias_tiles = _bias_tables(rel_bias, seq)

    ffn_w = tuple(w.astype(BF16) for w in (ffn_w1, ffn_w3, ffn_w2))
    moe_w = tuple(w.astype(BF16) for w in (moe_w1, moe_w3, moe_w2))
    xf = x.reshape(n, d)
    for i in range(depth):
        mod = mod_all[i]
        ng = norm_g[i]
        lam_init = 0.8 - 0.6 * math.exp(-0.3 * i)

        w_gate = w_in[i][:, n_main:]
        w_bf = w_in[i][:, :n_main].astype(BF16)
        gw = HEAD_GROUP_W
        slabs = _inproj(
            xf, mod, ng[0:1], jnp.concatenate([w_bf[:, gw:2 * gw], w_bf[:, 3 * gw:]], axis=1),
            w_bf[:, :gw].T, w_bf[:, 2 * gw:3 * gw].T,
            jnp.zeros((d, LANES), BF16).at[:, :2 * M_HEADS].set(w_gate.astype(BF16)),
            w_gate.T.astype(BF16),
            jnp.zeros((1, LANES), F32).at[0, :2 * M_HEADS].set(gate_b[i].reshape(-1)),
            gate_b[i].reshape(2 * M_HEADS, 1), batch, seq)
        ak, mq, mk, mv, mo = (s.reshape(batch, seq, HEAD_GROUP_W) for s in slabs[:N_SLABS - 2])
        aqt, avt, grow, gcol = slabs[N_SLABS - 2:]

        scal = jnp.full((1,), lam_init, F32)
        a_out = _attention(aqt, ak, avt, scal, bias_tiles, lam_vec[i], attn_norm_g[i].reshape(LANES, 1))

        h_m = _mlstm(mq, mk, mv, mo, grow, gcol,
                     conv_w[i].reshape(CONV_K, 2 * M_HEADS, LANES).transpose(1, 0, 2),
                     conv_b[i].reshape(2 * M_HEADS, 1, LANES),
                     mlstm_norm_g[i].reshape(M_HEADS, 1, LANES))

        w_o = w_out[i].astype(BF16)
        j = i // 2
        if i % 2 == 0:
            xf = _mixer_ffn(a_out.reshape(n, -1), h_m.reshape(n, -1), xf, mod, ng,
                            w_o[:HEAD_GROUP_W], w_o[HEAD_GROUP_W:], *ffn_w, j, batch, seq)
        else:
            rw = jnp.zeros((d, LANES), F32).at[:, :N_EXPERTS].set(router_w[j])
            x1, h2, gates, pos, post, cnt = _outproj(a_out.reshape(n, -1), h_m.reshape(n, -1), xf, mod, ng,
                                                     w_o[:HEAD_GROUP_W], w_o[HEAD_GROUP_W:], rw, batch, seq)
            tile_tables, seg_tables = _moe_tables(cnt[:, 0, :N_EXPERTS].astype(jnp.int32), batch, seq)
            ys = _moe_experts(h2, post, tile_tables, *moe_w, j, batch, seq)
            xf = _moe_combine(ys, pos, gates, x1, mod, ng, seg_tables, batch, seq)
    return xf.reshape(batch, seq, d)


def kernel(x, c, rel_bias, ada_w, ada_b, norm_g, w_in, gate_b, conv_w, conv_b, lam_vec, attn_norm_g,
           mlstm_norm_g, w_out, ffn_w1, ffn_w3, ffn_w2, router_w, moe_w1, moe_w3, moe_w2):
    return _forward(x, c, rel_bias, ada_w, ada_b, norm_g, w_in, gate_b, conv_w, conv_b, lam_vec,
                    attn_norm_g, mlstm_norm_g, w_out, ffn_w1, ffn_w3, ffn_w2, router_w, moe_w1, moe_w3, moe_w2)
```
